```python
import jax, jax.numpy as jnp
from jax import lax
import numpy as np

D_MODEL = 4096
BATCH = 1
SEQ = 8192
DEPTH = 1

HEAD_DIM = 64
SWA_Q_HEADS = 32
SWA_KV_HEADS = 4
SWA_GROUP = SWA_Q_HEADS // SWA_KV_HEADS
WINDOW = 128
FOX_HEADS = 32
Q_BLOCK = 128
ROPE_THETA = 10000.0
N_EXPERTS = 32
TOP_K = 4
D_FF = 1536
SWIGLU_LIMIT = 7.0
SWIGLU_ALPHA = 1.702
RMS_EPS = 1e-6
N_MOD = 6

SWA_Q_W = SWA_Q_HEADS * HEAD_DIM
SWA_KV_W = SWA_KV_HEADS * HEAD_DIM
FOX_W = FOX_HEADS * HEAD_DIM
MIX_W = SWA_Q_W + FOX_W
IN_W = SWA_Q_W + 2 * SWA_KV_W + 3 * FOX_W + FOX_HEADS
_SPLITS = (SWA_Q_W,
           SWA_Q_W + SWA_KV_W,
           SWA_Q_W + 2 * SWA_KV_W,
           SWA_Q_W + 2 * SWA_KV_W + FOX_W,
           SWA_Q_W + 2 * SWA_KV_W + 2 * FOX_W,
           SWA_Q_W + 2 * SWA_KV_W + 3 * FOX_W)

kernel_name = 'hymba_swa_sink_fox_moe_layer'


def rmsnorm(x, g):
    xf = x.astype(jnp.float32)
    inv = lax.rsqrt(jnp.mean(xf * xf, axis=-1, keepdims=True) + RMS_EPS)
    return (xf * inv * g.astype(jnp.float32)).astype(x.dtype)


def rope(t, positions):
    half = HEAD_DIM // 2
    inv_freq = jnp.power(ROPE_THETA, -jnp.arange(half, dtype=jnp.float32) * (2.0 / HEAD_DIM))
    ang = positions.astype(jnp.float32)[..., None] * inv_freq
    cos = jnp.cos(ang)[:, :, None, :]
    sin = jnp.sin(ang)[:, :, None, :]
    tf = t.astype(jnp.float32)
    t1, t2 = tf[..., :half], tf[..., half:]
    out = jnp.concatenate([t1 * cos - t2 * sin, t2 * cos + t1 * sin], axis=-1)
    return out.astype(t.dtype)


def sliding_window_sink_attention(q, k, v, sinks):
    B, S = q.shape[0], q.shape[1]
    nb = S // WINDOW
    scale = HEAD_DIM ** -0.5
    qb = q.astype(jnp.float32).reshape(B, nb, WINDOW, SWA_KV_HEADS, SWA_GROUP, HEAD_DIM)

    def band(t):
        cur = t.astype(jnp.float32).reshape(B, nb, WINDOW, SWA_KV_HEADS, HEAD_DIM)
        prev = jnp.pad(cur, ((0, 0), (1, 0), (0, 0), (0, 0), (0, 0)))[:, :-1]
        return jnp.concatenate([prev, cur], axis=2)

    kb, vb = band(k), band(v)
    s = jnp.einsum('bnqhgd,bnkhd->bnhgqk', qb, kb) * scale
    qi = jnp.arange(WINDOW)[:, None]
    kj = jnp.arange(2 * WINDOW)[None, :]
    diff = qi + WINDOW - kj
    in_band = (diff >= 0) & (diff < WINDOW)
    blk = jnp.arange(nb)[:, None, None]
    key_valid = (blk * WINDOW - WINDOW + kj[None]) >= 0
    mask = in_band[None] & key_valid
    s = jnp.where(mask[None, :, None, None], s, -jnp.inf)
    sink = jnp.broadcast_to(sinks.astype(jnp.float32).reshape(1, 1, SWA_KV_HEADS, SWA_GROUP, 1, 1),
                            s.shape[:-1] + (1,))
    p = jax.nn.softmax(jnp.concatenate([s, sink], axis=-1), axis=-1)[..., :-1]
    o = jnp.einsum('bnhgqk,bnkhd->bnqhgd', p, vb)
    return o.reshape(B, S, SWA_Q_W).astype(q.dtype)


def forgetting_attention(q, k, v, f_logit):
    B, S = q.shape[0], q.shape[1]
    nb = S // Q_BLOCK
    scale = HEAD_DIM ** -0.5
    cum = lax.cumsum(jax.nn.log_sigmoid(f_logit.astype(jnp.float32)), axis=1)
    cum_h = jnp.transpose(cum, (0, 2, 1))
    kf = k.astype(jnp.float32)
    vf = v.astype(jnp.float32)
    qb = jnp.transpose(q.astype(jnp.float32).reshape(B, nb, Q_BLOCK, FOX_HEADS, HEAD_DIM),
                       (1, 0, 2, 3, 4))
    cqb = jnp.transpose(cum_h.reshape(B, FOX_HEADS, nb, Q_BLOCK), (2, 0, 1, 3))
    key_pos = jnp.arange(S)

    def one_block(args):
        q_blk, c_blk, n = args
        s = jnp.einsum('bqhd,bkhd->bhqk', q_blk, kf) * scale
        s = s + (c_blk[..., :, None] - cum_h[:, :, None, :])
        q_pos = n * Q_BLOCK + jnp.arange(Q_BLOCK)
        s = jnp.where(key_pos[None, :] <= q_pos[:, None], s, -jnp.inf)
        p = jax.nn.softmax(s, axis=-1)
        return jnp.einsum('bhqk,bkhd->bqhd', p, vf)

    o = lax.map(one_block, (qb, cqb, jnp.arange(nb)))
    return jnp.transpose(o, (1, 0, 2, 3, 4)).reshape(B, S, FOX_W).astype(q.dtype)


def clamped_swiglu(gu):
    x_glu = jnp.minimum(gu[..., ::2], SWIGLU_LIMIT)
    x_lin = jnp.clip(gu[..., 1::2], -SWIGLU_LIMIT, SWIGLU_LIMIT)
    return x_glu * jax.nn.sigmoid(SWIGLU_ALPHA * x_glu) * (x_lin + 1.0)


def moe(h, w_router, b_router, w_gate_up, b_gate_up, w_down, b_down):
    B, S, D = h.shape
    ht = h.reshape(B * S, D)
    logits = (ht @ w_router + b_router).astype(jnp.float32)
    top_vals, top_idx = lax.top_k(logits, TOP_K)
    probs = jax.nn.softmax(top_vals, axis=-1)
    combine = jnp.einsum('tk,tke->te', probs,
                         jax.nn.one_hot(top_idx, N_EXPERTS, dtype=jnp.float32)).astype(h.dtype)
    out = jnp.zeros_like(ht)
    for e in range(N_EXPERTS):
        act = clamped_swiglu(ht @ w_gate_up[e] + b_gate_up[e])
        out = out + combine[:, e:e + 1] * (act @ w_down[e] + b_down[e])
    return out.reshape(B, S, D)


def setup_inputs(seed: int = 0) -> dict:
    key = jax.random.key(seed)
    ks = jax.random.split(key, 24)
    f32 = jnp.float32
    nrm = lambda k, shape, s: jax.random.normal(k, shape, f32) * s
    x = nrm(ks[0], (BATCH, SEQ, D_MODEL), 1.0)
    c = nrm(ks[1], (BATCH, D_MODEL), 1.0)
    start = jax.random.randint(ks[2], (BATCH, 1), 0, 1024, dtype=jnp.int32)
    positions = start + jnp.arange(SEQ, dtype=jnp.int32)[None, :]
    w_mod = nrm(ks[3], (DEPTH, D_MODEL, N_MOD * D_MODEL), 0.5 * D_MODEL ** -0.5)
    b_mod = nrm(ks[4], (DEPTH, N_MOD * D_MODEL), 0.02)
    g_pre_mix = 1.0 + nrm(ks[5], (DEPTH, D_MODEL), 0.05)
    g_post_mix = 1.0 + nrm(ks[6], (DEPTH, D_MODEL), 0.05)
    g_pre_ffn = 1.0 + nrm(ks[7], (DEPTH, D_MODEL), 0.05)
    g_post_ffn = 1.0 + nrm(ks[8], (DEPTH, D_MODEL), 0.05)
    w_in = nrm(ks[9], (DEPTH, D_MODEL, IN_W), D_MODEL ** -0.5)
    b_in = nrm(ks[10], (DEPTH, IN_W), 0.02)
    sinks = nrm(ks[11], (DEPTH, SWA_Q_HEADS), 0.5)
    w_out = nrm(ks[12], (DEPTH, MIX_W, D_MODEL), MIX_W ** -0.5)
    b_out = nrm(ks[13], (DEPTH, D_MODEL), 0.02)
    w_router = nrm(ks[14], (DEPTH, D_MODEL, N_EXPERTS), D_MODEL ** -0.5)
    b_router = nrm(ks[15], (DEPTH, N_EXPERTS), 0.01)
    w_gate_up = nrm(ks[16], (DEPTH, N_EXPERTS, D_MODEL, 2 * D_FF), D_MODEL ** -0.5)
    b_gate_up = nrm(ks[17], (DEPTH, N_EXPERTS, 2 * D_FF), 0.02)
    w_down = nrm(ks[18], (DEPTH, N_EXPERTS, D_FF, D_MODEL), D_FF ** -0.5)
    b_down = nrm(ks[19], (DEPTH, N_EXPERTS, D_MODEL), 0.02)
    return {'x': x, 'c': c, 'positions': positions, 'w_mod': w_mod, 'b_mod': b_mod,
            'g_pre_mix': g_pre_mix, 'g_post_mix': g_post_mix,
            'g_pre_ffn': g_pre_ffn, 'g_post_ffn': g_post_ffn,
            'w_in': w_in, 'b_in': b_in, 'sinks': sinks, 'w_out': w_out, 'b_out': b_out,
            'w_router': w_router, 'b_router': b_router,
            'w_gate_up': w_gate_up, 'b_gate_up': b_gate_up,
            'w_down': w_down, 'b_down': b_down}


def reference(x, c, positions, w_mod, b_mod, g_pre_mix, g_post_mix, g_pre_ffn, g_post_ffn,
              w_in, b_in, sinks, w_out, b_out, w_router, b_router,
              w_gate_up, b_gate_up, w_down, b_down):
    B, S, _ = x.shape
    for l in range(DEPTH):
        mod = (jax.nn.silu(c) @ w_mod[l] + b_mod[l])[:, None, :]
        shift_m, scale_m, gate_m, shift_f, scale_f, gate_f = jnp.split(mod, N_MOD, axis=-1)

        h = rmsnorm(x, g_pre_mix[l]) * (1.0 + scale_m) + shift_m
        proj = h @ w_in[l] + b_in[l]
        q_a, k_a, v_a, q_f, k_f, v_f, f_logit = jnp.split(proj, _SPLITS, axis=-1)
        q_a = rope(q_a.reshape(B, S, SWA_Q_HEADS, HEAD_DIM), positions)
        k_a = rope(k_a.reshape(B, S, SWA_KV_HEADS, HEAD_DIM), positions)
        v_a = v_a.reshape(B, S, SWA_KV_HEADS, HEAD_DIM)
        o_a = sliding_window_sink_attention(q_a, k_a, v_a, sinks[l])
        o_f = forgetting_attention(q_f.reshape(B, S, FOX_HEADS, HEAD_DIM),
                                   k_f.reshape(B, S, FOX_HEADS, HEAD_DIM),
                                   v_f.reshape(B, S, FOX_HEADS, HEAD_DIM),
                                   f_logit)
        y = jnp.concatenate([o_a, o_f], axis=-1) @ w_out[l] + b_out[l]
        x = x + gate_m * rmsnorm(y, g_post_mix[l])

        h = rmsnorm(x, g_pre_ffn[l]) * (1.0 + scale_f) + shift_f
        y = moe(h, w_router[l], b_router[l], w_gate_up[l], b_gate_up[l], w_down[l], b_down[l])
        x = x + gate_f * rmsnorm(y, g_post_ffn[l])
    return x
```

```python
import functools

import jax
import jax.numpy as jnp
import numpy as np
from jax import lax
from jax.experimental import pallas as pl
from jax.experimental.pallas import tpu as pltpu

HEAD_DIM = 64
SWA_Q_HEADS = 32
SWA_KV_HEADS = 4
SWA_GROUP = SWA_Q_HEADS // SWA_KV_HEADS
WINDOW = 128
FOX_HEADS = 32
ROPE_THETA = 10000.0
N_EXPERTS = 32
TOP_K = 4
D_FF = 1536
SWIGLU_LIMIT = 7.0
SWIGLU_ALPHA = 1.702
RMS_EPS = 1e-6
N_MOD = 6

SWA_Q_W = SWA_Q_HEADS * HEAD_DIM
SWA_KV_W = SWA_KV_HEADS * HEAD_DIM
FOX_W = FOX_HEADS * HEAD_DIM
MIX_W = SWA_Q_W + FOX_W
IN_W = SWA_Q_W + 2 * SWA_KV_W + 3 * FOX_W + FOX_HEADS
OFF_QA = 0
OFF_KA = SWA_Q_W
OFF_VA = OFF_KA + SWA_KV_W
OFF_QF = OFF_VA + SWA_KV_W
OFF_KF = OFF_QF + FOX_W
OFF_VF = OFF_KF + FOX_W
OFF_FL = OFF_VF + FOX_W

LANES = 128
SUBLANES = 8
V7X_VMEM_BYTES = 64 * 1024 * 1024
VMEM_LIMIT = V7X_VMEM_BYTES - 8 * 1024 * 1024

NEG = -1e30

F32 = jnp.float32
BF16 = jnp.bfloat16


def _params(sem, vmem=VMEM_LIMIT):
    return pltpu.CompilerParams(dimension_semantics=sem, vmem_limit_bytes=vmem)


MOD_ROWS = 256


def _mod_kernel(c_ref, w_ref, b_ref, o_ref):
    d, tn = w_ref.shape

    def body(r, acc):
        rows = pl.ds(pl.multiple_of(r * MOD_ROWS, MOD_ROWS), MOD_ROWS)
        c = c_ref[rows, :]
        sc = c * jax.nn.sigmoid(c)
        prod = w_ref[rows, :] * sc
        return acc + jnp.sum(prod.reshape(MOD_ROWS // SUBLANES, SUBLANES, tn), axis=0)

    acc = lax.fori_loop(0, d // MOD_ROWS, body, jnp.zeros((SUBLANES, tn), F32))
    o_ref[...] = jnp.sum(acc, axis=0, keepdims=True) + b_ref[...]


def _modulation(c, w_mod, b_mod):
    d, n = w_mod.shape
    tn = min(1024, n)
    return pl.pallas_call(
        _mod_kernel,
        out_shape=jax.ShapeDtypeStruct((1, n), F32),
        grid=(n // tn,),
        in_specs=[pl.BlockSpec((d, 1), lambda j: (0, 0)),
                  pl.BlockSpec((d, tn), lambda j: (0, j)),
                  pl.BlockSpec((1, tn), lambda j: (0, j))],
        out_specs=pl.BlockSpec((1, tn), lambda j: (0, j)),
        compiler_params=_params(("arbitrary",)),
        name="modulation",
    )(c.reshape(d, 1), w_mod, b_mod.reshape(1, n))


def _rms(x, g):
    inv = lax.rsqrt(jnp.mean(x * x, axis=-1, keepdims=True) + RMS_EPS)
    return x * inv * g


def _prenorm_kernel(x_ref, g_ref, scale_ref, shift_ref, o_ref):
    h = _rms(x_ref[...], g_ref[...]) * (1.0 + scale_ref[...]) + shift_ref[...]
    o_ref[...] = h.astype(o_ref.dtype)


def _prenorm(x, g, mod, scale_idx, shift_idx):
    s, d = x.shape
    tm = min(256, s)
    return pl.pallas_call(
        _prenorm_kernel,
        out_shape=jax.ShapeDtypeStruct((s, d), BF16),
        grid=(s // tm,),
        in_specs=[pl.BlockSpec((tm, d), lambda i: (i, 0)),
                  pl.BlockSpec((1, d), lambda i: (0, 0)),
                  pl.BlockSpec((1, d), lambda i: (0, scale_idx)),
                  pl.BlockSpec((1, d), lambda i: (0, shift_idx))],
        out_specs=pl.BlockSpec((tm, d), lambda i: (i, 0)),
        compiler_params=_params(("arbitrary",)),
        name="prenorm",
    )(x, g.reshape(1, d), mod, mod)


def _inproj_kernel(a_ref, w_ref, b_ref, o_ref):
    acc = jnp.dot(a_ref[...], w_ref[...].astype(BF16), preferred_element_type=F32)
    o_ref[...] = acc + b_ref[...]


def _inproj(h, w, b):
    m, k = h.shape
    n = w.shape[1]
    tm = min(1024, m)
    tn = 512
    return pl.pallas_call(
        _inproj_kernel,
        out_shape=jax.ShapeDtypeStruct((m, n), F32),
        grid=(m // tm, pl.cdiv(n, tn)),
        in_specs=[pl.BlockSpec((tm, k), lambda i, j: (i, 0)),
                  pl.BlockSpec((k, tn), lambda i, j: (0, j)),
                  pl.BlockSpec((1, tn), lambda i, j: (0, j))],
        out_specs=pl.BlockSpec((tm, tn), lambda i, j: (i, j)),
        compiler_params=_params(("arbitrary", "arbitrary")),
        name="in_projection",
    )(h, w, b.reshape(1, n))


def _outproj_kernel(a1_ref, a2_ref, w_ref, b_ref, o_ref):
    k1 = a1_ref.shape[1]
    w = w_ref[...].astype(BF16)
    acc = jnp.dot(a1_ref[...], w[:k1], preferred_element_type=F32)
    acc = acc + jnp.dot(a2_ref[...], w[k1:], preferred_element_type=F32)
    o_ref[...] = acc + b_ref[...]


def _outproj(o_a, o_f, w, b):
    m, k1 = o_a.shape
    k2 = o_f.shape[1]
    n = w.shape[1]
    tm = min(1024, m)
    tn = min(512, n)
    return pl.pallas_call(
        _outproj_kernel,
        out_shape=jax.ShapeDtypeStruct((m, n), F32),
        grid=(m // tm, n // tn),
        in_specs=[pl.BlockSpec((tm, k1), lambda i, j: (i, 0)),
                  pl.BlockSpec((tm, k2), lambda i, j: (i, 0)),
                  pl.BlockSpec((k1 + k2, tn), lambda i, j: (0, j)),
                  pl.BlockSpec((1, tn), lambda i, j: (0, j))],
        out_specs=pl.BlockSpec((tm, tn), lambda i, j: (i, j)),
        compiler_params=_params(("arbitrary", "arbitrary")),
        name="out_projection",
    )(o_a, o_f, w, b.reshape(1, n))


def _swap_halves(x, first_half):
    return jnp.where(first_half, pltpu.roll(x, LANES - HEAD_DIM // 2, 1),
                     pltpu.roll(x, HEAD_DIM // 2, 1))


def _swa_kernel(sinks_ref, posc_ref, posp_ref, invf_ref, q_ref, kp_ref, kc_ref, vp_ref, vc_ref, o_ref):
    n = pl.program_id(0)
    w = WINDOW
    lane = lax.broadcasted_iota(jnp.int32, (w, LANES), 1)
    first_half = (lane % HEAD_DIM) < (HEAD_DIM // 2)
    lo64 = lane < HEAD_DIM
    lane2 = lax.broadcasted_iota(jnp.int32, (2 * w, LANES), 1)
    lo64_2 = lane2 < HEAD_DIM

    def tables(pos_ref):
        ang = pos_ref[...].astype(F32) * invf_ref[...]
        sin = jnp.sin(ang)
        return jnp.cos(ang), jnp.where(first_half, -sin, sin)

    cos_c, sin_c = tables(posc_ref)
    cos_p, sin_p = tables(posp_ref)

    def rope(t, cos, sin):
        return t * cos + _swap_halves(t, first_half) * sin

    qi = lax.broadcasted_iota(jnp.int32, (w, 2 * w), 0)
    kj = lax.broadcasted_iota(jnp.int32, (w, 2 * w), 1)
    diff = qi + w - kj
    valid = (diff >= 0) & (diff < w) & ((n * w - w + kj) >= 0)

    scale = HEAD_DIM ** -0.5
    nt_dims = (((1,), (1,)), ((), ()))
    for kvp in range(SWA_KV_HEADS // 2):
        cols = slice(kvp * LANES, (kvp + 1) * LANES)
        k2 = jnp.concatenate([rope(kp_ref[:, cols], cos_p, sin_p),
                              rope(kc_ref[:, cols], cos_c, sin_c)], axis=0)
        v2 = jnp.concatenate([vp_ref[:, cols], vc_ref[:, cols]], axis=0)
        k2r = pltpu.roll(k2, HEAD_DIM, 1)
        v2r = pltpu.roll(v2, HEAD_DIM, 1)
        for sub in range(2):
            hkv = 2 * kvp + sub
            src_k, alt_k = (k2, k2r) if sub == 0 else (k2r, k2)
            src_v, alt_v = (v2, v2r) if sub == 0 else (v2r, v2)
            ka = jnp.where(lo64_2, src_k, 0.0).astype(BF16)
            kb = jnp.where(lo64_2, 0.0, alt_k).astype(BF16)
            va = jnp.where(lo64_2, src_v, 0.0).astype(BF16)
            vb = jnp.where(lo64_2, 0.0, alt_v).astype(BF16)
            for gp in range(SWA_GROUP // 2):
                pair = hkv * (SWA_GROUP // 2) + gp
                qcols = slice(pair * LANES, (pair + 1) * LANES)
                q = (rope(q_ref[:, qcols], cos_c, sin_c) * scale).astype(BF16)
                o = jnp.zeros((w, LANES), F32)
                for which, (kk, vv) in enumerate(((ka, va), (kb, vb))):
                    sink = sinks_ref[2 * pair + which]
                    s = lax.dot_general(q, kk, nt_dims, preferred_element_type=F32)
                    s = jnp.where(valid, s, NEG)
                    m = jnp.maximum(jnp.max(s, axis=-1, keepdims=True), sink)
                    p = jnp.exp(s - m)
                    den = jnp.sum(p, axis=-1, keepdims=True) + jnp.exp(sink - m)
                    pv = jnp.dot(p.astype(BF16), vv, preferred_element_type=F32)
                    o = o + pv * (1.0 / den)
                o_ref[:, qcols] = o.astype(o_ref.dtype)


def _swa_attention(proj, positions, sinks):
    s = proj.shape[0]
    w = WINDOW
    nb = s // w
    half = HEAD_DIM // 2
    inv_freq = jnp.power(ROPE_THETA, -jnp.arange(half, dtype=F32) * (2.0 / HEAD_DIM))
    invf = jnp.tile(inv_freq, LANES // half).reshape(1, LANES)
    pos_col = positions.reshape(s, 1)
    prev = lambda n: jnp.maximum(n - 1, 0)
    ka_blk = OFF_KA // SWA_KV_W
    va_blk = OFF_VA // SWA_KV_W
    return pl.pallas_call(
        _swa_kernel,
        out_shape=jax.ShapeDtypeStruct((s, SWA_Q_W), BF16),
        grid=(nb,),
        in_specs=[pl.BlockSpec(memory_space=pltpu.SMEM),
                  pl.BlockSpec((w, 1), lambda n: (n, 0)),
                  pl.BlockSpec((w, 1), lambda n: (prev(n), 0)),
                  pl.BlockSpec((1, LANES), lambda n: (0, 0)),
                  pl.BlockSpec((w, SWA_Q_W), lambda n: (n, 0)),
                  pl.BlockSpec((w, SWA_KV_W), lambda n: (prev(n), ka_blk)),
                  pl.BlockSpec((w, SWA_KV_W), lambda n: (n, ka_blk)),
                  pl.BlockSpec((w, SWA_KV_W), lambda n: (prev(n), va_blk)),
                  pl.BlockSpec((w, SWA_KV_W), lambda n: (n, va_blk))],
        out_specs=pl.BlockSpec((w, SWA_Q_W), lambda n: (n, 0)),
        compiler_params=_params(("arbitrary",)),
        name="swa_sink_attention",
    )(sinks, pos_col, pos_col, invf, proj, proj, proj, proj, proj)


CUM_ROWS = 256


def _cum_kernel(f_ref, o_ref, carry_ref):
    @pl.when(pl.program_id(0) == 0)
    def _():
        carry_ref[...] = jnp.zeros_like(carry_ref)

    f = f_ref[:, :FOX_HEADS]
    ls = jnp.minimum(f, 0.0) - jnp.log1p(jnp.exp(-jnp.abs(f)))
    r = lax.broadcasted_iota(jnp.int32, (CUM_ROWS, CUM_ROWS), 0)
    c = lax.broadcasted_iota(jnp.int32, (CUM_ROWS, CUM_ROWS), 1)
    tri = (c <= r).astype(F32)
    cum = jnp.dot(tri, ls, preferred_element_type=F32, precision=lax.Precision.HIGHEST) + carry_ref[...]
    o_ref[...] = cum
    carry_ref[...] = cum[CUM_ROWS - 1:CUM_ROWS, :]


def _fox_cum(proj):
    s = proj.shape[0]
    return pl.pallas_call(
        _cum_kernel,
        out_shape=jax.ShapeDtypeStruct((s, FOX_HEADS), F32),
        grid=(s // CUM_ROWS,),
        in_specs=[pl.BlockSpec((CUM_ROWS, LANES), lambda i: (i, OFF_FL // LANES))],
        out_specs=pl.BlockSpec((CUM_ROWS, FOX_HEADS), lambda i: (i, 0)),
        scratch_shapes=[pltpu.VMEM((1, FOX_HEADS), F32)],
        compiler_params=_params(("arbitrary",)),
        name="fox_decay_cumsum",
    )(proj)


FOX_BLK = 512
AUG_CQ = HEAD_DIM
AUG_ONE = HEAD_DIM + 3
AUG_END = HEAD_DIM + 6


def _split3(c):
    hi = c.astype(BF16).astype(F32)
    r = c - hi
    mid = r.astype(BF16).astype(F32)
    lo = (r - mid).astype(BF16).astype(F32)
    return hi, mid, lo


def _fox_prep_kernel(q_ref, k_ref, v_ref, cum_ref, qo_ref, ko_ref, vo_ref):
    j = pl.program_id(0)
    tb = q_ref.shape[0]
    lane = lax.broadcasted_iota(jnp.int32, (tb, LANES), 1)
    lo64 = lane < HEAD_DIM
    lane_h = lax.broadcasted_iota(jnp.int32, (tb, FOX_HEADS), 1)
    cum = cum_ref[...]
    scale = HEAD_DIM ** -0.5
    for hh in range(2):
        h = 2 * j + hh
        col = jnp.sum(jnp.where(lane_h == h, cum, 0.0), axis=-1, keepdims=True)
        hi, mid, lo = _split3(col - col[0:1, :])
        q = q_ref[...] * scale
        k = k_ref[...]
        v = v_ref[...]
        if hh == 1:
            q = pltpu.roll(q, HEAD_DIM, 1)
            k = pltpu.roll(k, HEAD_DIM, 1)
            v = pltpu.roll(v, HEAD_DIM, 1)
        q_aug = jnp.where(lane == AUG_CQ, hi, jnp.where(lane == AUG_CQ + 1, mid,
                jnp.where(lane == AUG_CQ + 2, lo, jnp.where(lane < AUG_END, 1.0, 0.0))))
        k_aug = jnp.where(lane < AUG_ONE, 1.0, jnp.where(lane == AUG_ONE, -hi,
                jnp.where(lane == AUG_ONE + 1, -mid, jnp.where(lane == AUG_ONE + 2, -lo, 0.0))))
        v_aug = jnp.where(lane == HEAD_DIM, 1.0, 0.0)
        qo_ref[hh] = jnp.where(lo64, q, q_aug).astype(BF16)
        ko_ref[hh] = jnp.where(lo64, k, k_aug).astype(BF16)
        vo_ref[hh] = jnp.where(lo64, v, v_aug).astype(BF16)


def _fox_prep(proj, cum, tb):
    s = proj.shape[0]
    npairs = FOX_HEADS // 2
    shp = jax.ShapeDtypeStruct((FOX_HEADS, s, LANES), BF16)
    ospec = pl.BlockSpec((2, tb, LANES), lambda j, i: (j, i, 0))
    return pl.pallas_call(
        _fox_prep_kernel,
        out_shape=(shp, shp, shp),
        grid=(npairs, s // tb),
        in_specs=[pl.BlockSpec((tb, LANES), lambda j, i: (i, OFF_QF // LANES + j)),
                  pl.BlockSpec((tb, LANES), lambda j, i: (i, OFF_KF // LANES + j)),
                  pl.BlockSpec((tb, LANES), lambda j, i: (i, OFF_VF // LANES + j)),
                  pl.BlockSpec((tb, FOX_HEADS), lambda j, i: (i, 0))],
        out_specs=(ospec, ospec, ospec),
        compiler_params=_params(("arbitrary", "arbitrary")),
        name="fox_prepare",
    )(proj, proj, proj, cum)


def _fox_kernel(a_ref, q_ref, k_ref, v_ref, o_ref):
    j = pl.program_id(0)
    i = pl.program_id(1)
    tb = q_ref.shape[1]
    nt_dims = (((1,), (1,)), ((), ()))
    row = lax.broadcasted_iota(jnp.int32, (tb, tb), 0)
    col = lax.broadcasted_iota(jnp.int32, (tb, tb), 1)
    causal = col <= row

    def one_head(hh, kt, m, acc, masked):
        h = 2 * j + hh
        rows = pl.ds(pl.multiple_of(kt * tb, tb), tb)
        s = lax.dot_general(q_ref[hh], k_ref[hh, rows, :], nt_dims, preferred_element_type=F32)
        if masked:
            s = jnp.where(causal, s, NEG)
        delta = a_ref[h, i] - a_ref[h, kt]
        m_new = jnp.maximum(m, jnp.max(s, axis=-1, keepdims=True) + delta)
        p = jnp.exp(s - (m_new - delta))
        alpha = jnp.exp(m - m_new)
        acc = alpha * acc + jnp.dot(p.astype(BF16), v_ref[hh, rows, :], preferred_element_type=F32)
        return m_new, acc

    def step(kt, carry, masked=False):
        m0, acc0, m1, acc1 = carry
        m0, acc0 = one_head(0, kt, m0, acc0, masked)
        m1, acc1 = one_head(1, kt, m1, acc1, masked)
        return m0, acc0, m1, acc1

    init = (jnp.full((tb, 1), NEG, F32), jnp.zeros((tb, LANES), F32),
            jnp.full((tb, 1), NEG, F32), jnp.zeros((tb, LANES), F32))
    carry = lax.fori_loop(0, i, step, init)
    _, acc0, _, acc1 = step(i, carry, masked=True)
    lo64 = lax.broadcasted_iota(jnp.int32, (tb, LANES), 1) < HEAD_DIM
    o0 = acc0 * (1.0 / acc0[:, HEAD_DIM:HEAD_DIM + 1])
    o1 = acc1 * (1.0 / acc1[:, HEAD_DIM:HEAD_DIM + 1])
    o_ref[...] = jnp.where(lo64, o0, pltpu.roll(o1, HEAD_DIM, 1)).astype(o_ref.dtype)


def _fox_attention(proj):
    s = proj.shape[0]
    tb = min(FOX_BLK, s)
    cum = _fox_cum(proj)
    q_aug, k_aug, v_aug = _fox_prep(proj, cum, tb)
    block_ref = cum[::tb].T
    npairs = FOX_HEADS // 2
    return pl.pallas_call(
        _fox_kernel,
        out_shape=jax.ShapeDtypeStruct((s, FOX_W), BF16),
        grid=(npairs, s // tb),
        in_specs=[pl.BlockSpec(memory_space=pltpu.SMEM),
                  pl.BlockSpec((2, tb, LANES), lambda j, i: (j, i, 0)),
                  pl.BlockSpec((2, s, LANES), lambda j, i: (j, 0, 0)),
                  pl.BlockSpec((2, s, LANES), lambda j, i: (j, 0, 0))],
        out_specs=pl.BlockSpec((tb, LANES), lambda j, i: (i, j)),
        compiler_params=_params(("arbitrary", "arbitrary")),
        name="fox_attention",
    )(block_ref, q_aug, k_aug, v_aug)


def _lane_pack(cols, width, dtype):
    rows = cols[0].shape[0]
    lane = lax.broadcasted_iota(jnp.int32, (rows, width), 1)
    out = jnp.zeros((rows, width), dtype)
    for k, cval in enumerate(cols):
        out = jnp.where(lane == k, cval.astype(dtype), out)
    return out


def _postmix_kernel(x_ref, y_ref, gpost_ref, gate_ref, gpre_ref, scale_ref, shift_ref, wr_ref, br_ref,
                    x1_ref, h2_ref, idx_ref, prob_ref, rank_ref, cnt_ref, carry_ref):
    step = pl.program_id(0)

    @pl.when(step == 0)
    def _():
        carry_ref[...] = jnp.zeros_like(carry_ref)

    x1 = x_ref[...] + gate_ref[...] * _rms(y_ref[...], gpost_ref[...])
    x1_ref[...] = x1
    h2 = _rms(x1, gpre_ref[...]) * (1.0 + scale_ref[...]) + shift_ref[...]
    h2_ref[...] = h2
    logits = jnp.dot(h2, wr_ref[...], preferred_element_type=F32,
                     precision=lax.Precision.HIGHEST) + br_ref[...]
    tm = logits.shape[0]
    lane_e = lax.broadcasted_iota(jnp.int32, (tm, N_EXPERTS), 1).astype(F32)
    vals, idxs, sels = [], [], []
    cur = logits
    for _ in range(TOP_K):
        mk = jnp.max(cur, axis=-1, keepdims=True)
        ik = jnp.min(jnp.where(cur == mk, lane_e, float(N_EXPERTS)), axis=-1, keepdims=True)
        sel = lane_e == ik
        vals.append(mk)
        idxs.append(ik)
        sels.append(sel)
        cur = jnp.where(sel, -jnp.inf, cur)
    exps = [jnp.exp(v - vals[0]) for v in vals]
    inv = 1.0 / functools.reduce(lambda a, b: a + b, exps)
    probs = [e * inv for e in exps]

    onehot = functools.reduce(lambda a, b: a | b, sels).astype(F32)
    r = lax.broadcasted_iota(jnp.int32, (tm, tm), 0)
    c = lax.broadcasted_iota(jnp.int32, (tm, tm), 1)
    strict = (c < r).astype(BF16)
    before = jnp.dot(strict, onehot.astype(BF16), preferred_element_type=F32) + carry_ref[...]
    ranks = [jnp.sum(jnp.where(sel, before, 0.0), axis=-1, keepdims=True) for sel in sels]
    carry_ref[...] = carry_ref[...] + jnp.sum(onehot, axis=0, keepdims=True)

    idx_ref[...] = _lane_pack(idxs, LANES, jnp.int32)
    prob_ref[...] = _lane_pack(probs, LANES, F32)
    rank_ref[...] = _lane_pack([rk.astype(jnp.int32) for rk in ranks], LANES, jnp.int32)
    cnt_ref[...] = carry_ref[...].astype(jnp.int32)


def _postmix_router(x, y, mod, g_post, g_pre, w_router, b_router):
    s, d = x.shape
    tm = min(256, s)
    row = pl.BlockSpec((tm, d), lambda i: (i, 0))
    vec = lambda idx: pl.BlockSpec((1, d), lambda i: (0, idx))
    lanes = pl.BlockSpec((tm, LANES), lambda i: (i, 0))
    return pl.pallas_call(
        _postmix_kernel,
        out_shape=(jax.ShapeDtypeStruct((s, d), F32), jax.ShapeDtypeStruct((s, d), F32),
                   jax.ShapeDtypeStruct((s, LANES), jnp.int32), jax.ShapeDtypeStruct((s, LANES), F32),
                   jax.ShapeDtypeStruct((s, LANES), jnp.int32),
                   jax.ShapeDtypeStruct((1, N_EXPERTS), jnp.int32)),
        grid=(s // tm,),
        in_specs=[row, row, vec(0), vec(2), vec(0), vec(4), vec(3),
                  pl.BlockSpec((d, N_EXPERTS), lambda i: (0, 0)),
                  pl.BlockSpec((1, N_EXPERTS), lambda i: (0, 0))],
        out_specs=(row, row, lanes, lanes, lanes, pl.BlockSpec((1, N_EXPERTS), lambda i: (0, 0))),
        scratch_shapes=[pltpu.VMEM((1, N_EXPERTS), F32)],
        compiler_params=_params(("arbitrary",)),
        name="postmix_router",
    )(x, y, g_post.reshape(1, d), mod, g_pre.reshape(1, d), mod, mod, w_router, b_router.reshape(1, N_EXPERTS))


MOE_TM = 512
GATHER_TOKENS = 512


def _row_copy(src, src_row, dst, dst_row, sem):
    return pltpu.make_async_copy(src.at[pl.ds(src_row, 1)], dst.at[pl.ds(dst_row, 1)], sem)


def _moe_gather_kernel(pos_ref, cnt_ref, off_ref, nt_ref, h_ref, xs_ref, sem):
    i = pl.program_id(0)
    n_tiles_max = xs_ref.shape[0] // MOE_TM

    def tile_copy(t):
        return pltpu.make_async_copy(h_ref.at[pl.ds(0, MOE_TM)], xs_ref.at[pl.ds(t * MOE_TM, MOE_TM)], sem)

    @pl.when(i == 0)
    def _():
        def start_tail(t, carry):
            tile_copy(t).start()
            return carry

        def wait_tail(t, carry):
            tile_copy(t).wait()
            return carry

        lax.fori_loop(nt_ref[0], n_tiles_max, start_tail, 0)
        lax.fori_loop(nt_ref[0], n_tiles_max, wait_tail, 0)

    n_entries = pos_ref.shape[-1]
    tokens = n_entries // TOP_K

    def issue(e, carry):
        _row_copy(h_ref, i * tokens + e // TOP_K, xs_ref, pos_ref[0, 0, e], sem).start()
        return carry

    lax.fori_loop(0, n_entries, issue, 0)

    def drain(e, carry):
        _row_copy(h_ref, 0, xs_ref, 0, sem).wait()
        return carry

    lax.fori_loop(0, n_entries, drain, 0)

    @pl.when(i == pl.num_programs(0) - 1)
    def _():
        def per_expert(e, carry):
            cnt = cnt_ref[e]
            pad_end = ((cnt + MOE_TM - 1) // MOE_TM) * MOE_TM
            base = off_ref[e]

            def fill(r, c2):
                _row_copy(h_ref, 0, xs_ref, base + r, sem).start()
                return c2

            lax.fori_loop(cnt, pad_end, fill, 0)
            lax.fori_loop(cnt, pad_end, drain, 0)
            return carry

        lax.fori_loop(0, N_EXPERTS, per_expert, 0)


def _moe_gather(h2, pos, counts, offs, num_tiles, n_rows):
    s, d = h2.shape
    assert s >= MOE_TM
    tokens = min(GATHER_TOKENS, s)
    steps = s // tokens
    pos3 = pos.reshape(steps, 1, tokens * TOP_K)
    smem = pl.BlockSpec(memory_space=pltpu.SMEM)
    return pl.pallas_call(
        _moe_gather_kernel,
        out_shape=jax.ShapeDtypeStruct((n_rows, d), F32),
        grid=(steps,),
        in_specs=[pl.BlockSpec((1, 1, tokens * TOP_K), lambda i: (i, 0, 0), memory_space=pltpu.SMEM),
                  smem, smem, smem, pl.BlockSpec(memory_space=pl.ANY)],
        out_specs=pl.BlockSpec(memory_space=pl.ANY),
        scratch_shapes=[pltpu.SemaphoreType.DMA(())],
        compiler_params=_params(("arbitrary",)),
        name="moe_gather",
    )(pos3, counts, offs, num_tiles, h2)


GU_TN = 512
DEINT = 256


def _gateup_kernel(te_ref, nt_ref, x_ref, w_ref, b_ref, o_ref, xb_ref):
    i = pl.program_id(0)
    j = pl.program_id(1)

    @pl.when(i >= nt_ref[0])
    def _():
        o_ref[...] = jnp.zeros_like(o_ref)

    @pl.when(i < nt_ref[0])
    def _():
        @pl.when(j == 0)
        def _():
            xb_ref[...] = x_ref[...].astype(BF16)

        z = jnp.dot(xb_ref[...], w_ref[...].astype(BF16), preferred_element_type=F32) + b_ref[...]
        r = lax.broadcasted_iota(jnp.int32, (DEINT, DEINT), 0)
        c = lax.broadcasted_iota(jnp.int32, (DEINT, DEINT), 1)
        half = DEINT // 2
        perm = (r == jnp.where(c < half, 2 * c, 2 * (c - half) + 1)).astype(BF16)
        for t in range(GU_TN // DEINT):
            zz = jnp.dot(z[:, t * DEINT:(t + 1) * DEINT].astype(BF16), perm, preferred_element_type=F32)
            glu = jnp.minimum(zz[:, :half], SWIGLU_LIMIT)
            lin = jnp.clip(zz[:, half:], -SWIGLU_LIMIT, SWIGLU_LIMIT)
            act = glu * jax.nn.sigmoid(SWIGLU_ALPHA * glu) * (lin + 1.0)
            o_ref[:, t * half:(t + 1) * half] = act.astype(o_ref.dtype)


def _tile_maps(n_j):
    def clamp(i, j, nt):
        used = i < nt[0]
        return jnp.where(used, i, nt[0] - 1), jnp.where(used, j, n_j - 1)

    def rows(i, j, te, nt):
        ii, _ = clamp(i, j, nt)
        return ii, 0

    def weights(i, j, te, nt):
        ii, jj = clamp(i, j, nt)
        return te[ii], 0, jj

    def out(i, j, te, nt):
        return i, j

    return rows, weights, out


def _moe_gateup(xs, tile_expert, num_tiles, w_gate_up, b_gate_up):
    p, d = xs.shape
    nt = p // MOE_TM
    n_j = (2 * D_FF) // GU_TN
    rows, weights, out = _tile_maps(n_j)
    grid_spec = pltpu.PrefetchScalarGridSpec(
        num_scalar_prefetch=2,
        grid=(nt, n_j),
        in_specs=[pl.BlockSpec((MOE_TM, d), rows),
                  pl.BlockSpec((None, d, GU_TN), weights),
                  pl.BlockSpec((None, 1, GU_TN), weights)],
        out_specs=pl.BlockSpec((MOE_TM, GU_TN // 2), out),
        scratch_shapes=[pltpu.VMEM((MOE_TM, d), BF16)],
    )
    return pl.pallas_call(
        _gateup_kernel,
        out_shape=jax.ShapeDtypeStruct((p, D_FF), BF16),
        grid_spec=grid_spec,
        compiler_params=_params(("arbitrary", "arbitrary")),
        name="moe_gate_up",
    )(tile_expert, num_tiles, xs, w_gate_up, b_gate_up.reshape(N_EXPERTS, 1, 2 * D_FF))


DOWN_TN = 1024


def _down_kernel(te_ref, nt_ref, a_ref, w_ref, b_ref, o_ref):
    @pl.when(pl.program_id(0) >= nt_ref[0])
    def _():
        o_ref[...] = jnp.zeros_like(o_ref)

    @pl.when(pl.program_id(0) < nt_ref[0])
    def _():
        o_ref[...] = jnp.dot(a_ref[...], w_ref[...].astype(BF16), preferred_element_type=F32) + b_ref[...]


def _moe_down(act, tile_expert, num_tiles, w_down, b_down):
    p = act.shape[0]
    d = w_down.shape[2]
    tn = min(DOWN_TN, d)
    nt = p // MOE_TM
    n_j = d // tn
    rows, weights, out = _tile_maps(n_j)
    grid_spec = pltpu.PrefetchScalarGridSpec(
        num_scalar_prefetch=2,
        grid=(nt, n_j),
        in_specs=[pl.BlockSpec((MOE_TM, D_FF), rows),
                  pl.BlockSpec((None, D_FF, tn), weights),
                  pl.BlockSpec((None, 1, tn), weights)],
        out_specs=pl.BlockSpec((MOE_TM, tn), out),
    )
    return pl.pallas_call(
        _down_kernel,
        out_shape=jax.ShapeDtypeStruct((p, d), F32),
        grid_spec=grid_spec,
        compiler_params=_params(("arbitrary", "arbitrary")),
        name="moe_down",
    )(tile_expert, num_tiles, act, w_down, b_down.reshape(N_EXPERTS, 1, d))


COMBINE_TOKENS = 128


def _combine_kernel(pos_ref, prob_ref, x1_ref, g_ref, gate_ref, ys_ref, o_ref, buf_ref, sem):
    tc = x1_ref.shape[0]

    def issue(t, carry):
        for k in range(TOP_K):
            pltpu.make_async_copy(ys_ref.at[pl.ds(pos_ref[0, 0, t * TOP_K + k], 1)],
                                  buf_ref.at[k, pl.ds(t, 1)], sem).start()
        return carry

    lax.fori_loop(0, tc, issue, 0)

    def drain(t, carry):
        for k in range(TOP_K):
            pltpu.make_async_copy(ys_ref.at[pl.ds(0, 1)], buf_ref.at[k, pl.ds(0, 1)], sem).wait()
        return carry

    lax.fori_loop(0, tc, drain, 0)

    prob = prob_ref[...]
    y = prob[:, 0:1] * buf_ref[0]
    for k in range(1, TOP_K):
        y = y + prob[:, k:k + 1] * buf_ref[k]
    o_ref[...] = x1_ref[...] + gate_ref[...] * _rms(y, g_ref[...])


def _moe_combine(ys, pos, probs, x1, g_post, mod, gate_idx):
    s, d = x1.shape
    tc = min(COMBINE_TOKENS, s)
    steps = s // tc
    pos3 = pos.reshape(steps, 1, tc * TOP_K)
    return pl.pallas_call(
        _combine_kernel,
        out_shape=jax.ShapeDtypeStruct((s, d), F32),
        grid=(steps,),
        in_specs=[pl.BlockSpec((1, 1, tc * TOP_K), lambda i: (i, 0, 0), memory_space=pltpu.SMEM),
                  pl.BlockSpec((tc, LANES), lambda i: (i, 0)),
                  pl.BlockSpec((tc, d), lambda i: (i, 0)),
                  pl.BlockSpec((1, d), lambda i: (0, 0)),
                  pl.BlockSpec((1, d), lambda i: (0, gate_idx)),
                  pl.BlockSpec(memory_space=pl.ANY)],
        out_specs=pl.BlockSpec((tc, d), lambda i: (i, 0)),
        scratch_shapes=[pltpu.VMEM((TOP_K, tc, d), F32), pltpu.SemaphoreType.DMA(())],
        compiler_params=_params(("arbitrary",)),
        name="moe_combine",
    )(pos3, probs, x1, g_post.reshape(1, d), mod, ys)


def _moe(h2, idx, probs, rank, counts, x1, mod, g_post, w_gate_up, b_gate_up, w_down, b_down):
    s, d = h2.shape
    counts = counts.reshape(N_EXPERTS)
    padded = ((counts + MOE_TM - 1) // MOE_TM) * MOE_TM
    ends = jnp.cumsum(padded)
    offs = (ends - padded).astype(jnp.int32)
    n_tiles_max = (s * TOP_K + N_EXPERTS * (MOE_TM - 1)) // MOE_TM
    num_tiles = (ends[-1] // MOE_TM).astype(jnp.int32).reshape(1)
    tile_start = jnp.arange(n_tiles_max, dtype=jnp.int32) * MOE_TM
    tile_expert = jnp.minimum(jnp.sum(ends[None, :] <= tile_start[:, None], axis=1),
                              N_EXPERTS - 1).astype(jnp.int32)
    idx4 = idx[:, :TOP_K]
    pos = (offs[idx4] + rank[:, :TOP_K]).astype(jnp.int32)

    xs = _moe_gather(h2, pos, counts, offs, num_tiles, n_tiles_max * MOE_TM)
    act = _moe_gateup(xs, tile_expert, num_tiles, w_gate_up, b_gate_up)
    ys = _moe_down(act, tile_expert, num_tiles, w_down, b_down)
    return _moe_combine(ys, pos, probs, x1, g_post, mod, 5)


def _layer(x, c, positions, w_mod, b_mod, g_pre_mix, g_post_mix, g_pre_ffn, g_post_ffn,
           w_in, b_in, sinks, w_out, b_out, w_router, b_router, w_gate_up, b_gate_up, w_down, b_down):
    mod = _modulation(c, w_mod, b_mod)
    h = _prenorm(x, g_pre_mix, mod, 1, 0)
    proj = _inproj(h, w_in, b_in)
    o_a = _swa_attention(proj, positions, sinks)
    o_f = _fox_attention(proj)
    y = _outproj(o_a, o_f, w_out, b_out)
    x1, h2, idx, probs, rank, counts = _postmix_router(x, y, mod, g_post_mix, g_pre_ffn, w_router, b_router)
    return _moe(h2, idx, probs, rank, counts, x1, mod, g_post_ffn, w_gate_up, b_gate_up, w_down, b_down)


def kernel(x, c, positions, w_mod, b_mod, g_pre_mix, g_post_mix, g_pre_ffn, g_post_ffn, w_in, b_in, sinks,
           w_out, b_out, w_router, b_router, w_gate_up, b_gate_up, w_down, b_down):
    batch, seq, d = x.shape
    assert batch == 1 and w_mod.shape[0] == 1, "one sequence, one layer"
    assert seq % FOX_BLK == 0 or seq < FOX_BLK
    out = _layer(x[0], c[0], positions[0], w_mod[0], b_mod[0], g_pre_mix[0], g_post_mix[0], g_pre_ffn[0],
                 g_post_ffn[0], w_in[0], b_in[0], sinks[0], w_out[0], b_out[0], w_router[0], b_router[0],
                 w_gate_up[0], b_gate_up[0], w_down[0], b_down[0])
    return out[None]
```

```python
import functools

import jax
import jax.numpy as jnp
import numpy as np
from jax import lax
from jax.experimental import pallas as pl
from jax.experimental.pallas import tpu as pltpu

HEAD_DIM = 64
SWA_Q_HEADS = 32
SWA_KV_HEADS = 4
SWA_GROUP = SWA_Q_HEADS // SWA_KV_HEADS
WINDOW = 128
FOX_HEADS = 32
ROPE_THETA = 10000.0
N_EXPERTS = 32
TOP_K = 4
D_FF = 1536
SWIGLU_LIMIT = 7.0
SWIGLU_ALPHA = 1.702
RMS_EPS = 1e-6
N_MOD = 6

SWA_Q_W = SWA_Q_HEADS * HEAD_DIM
SWA_KV_W = SWA_KV_HEADS * HEAD_DIM
FOX_W = FOX_HEADS * HEAD_DIM
MIX_W = SWA_Q_W + FOX_W
IN_W = SWA_Q_W + 2 * SWA_KV_W + 3 * FOX_W + FOX_HEADS
OFF_QA = 0
OFF_KA = SWA_Q_W
OFF_VA = OFF_KA + SWA_KV_W
OFF_QF = OFF_VA + SWA_KV_W
OFF_KF = OFF_QF + FOX_W
OFF_VF = OFF_KF + FOX_W
OFF_FL = OFF_VF + FOX_W

LANES = 128
SUBLANES = 8
V7X_VMEM_BYTES = 64 * 1024 * 1024
VMEM_LIMIT = V7X_VMEM_BYTES - 8 * 1024 * 1024

NEG = -1e30

F32 = jnp.float32
BF16 = jnp.bfloat16


def _params(sem, vmem=VMEM_LIMIT):
    return pltpu.CompilerParams(dimension_semantics=sem, vmem_limit_bytes=vmem)


MOD_ROWS = 256


def _mod_kernel(c_ref, w_ref, b_ref, o_ref):
    d, tn = w_ref.shape

    def body(r, acc):
        rows = pl.ds(pl.multiple_of(r * MOD_ROWS, MOD_ROWS), MOD_ROWS)
        c = c_ref[rows, :]
        sc = c * jax.nn.sigmoid(c)
        prod = w_ref[rows, :] * sc
        return acc + jnp.sum(prod.reshape(MOD_ROWS // SUBLANES, SUBLANES, tn), axis=0)

    acc = lax.fori_loop(0, d // MOD_ROWS, body, jnp.zeros((SUBLANES, tn), F32))
    o_ref[...] = jnp.sum(acc, axis=0, keepdims=True) + b_ref[...]


def _modulation(c, w_mod, b_mod):
    d, n = w_mod.shape
    tn = min(1024, n)
    assert n % tn == 0 and d % MOD_ROWS == 0
    return pl.pallas_call(
        _mod_kernel,
        out_shape=jax.ShapeDtypeStruct((1, n), F32),
        grid=(n // tn,),
        in_specs=[pl.BlockSpec((d, 1), lambda j: (0, 0)),
                  pl.BlockSpec((d, tn), lambda j: (0, j)),
                  pl.BlockSpec((1, tn), lambda j: (0, j))],
        out_specs=pl.BlockSpec((1, tn), lambda j: (0, j)),
        compiler_params=_params(("arbitrary",)),
        name="modulation",
    )(c.reshape(d, 1), w_mod, b_mod.reshape(1, n))


def _rms(x, g):
    inv = lax.rsqrt(jnp.mean(x * x, axis=-1, keepdims=True) + RMS_EPS)
    return x * inv * g


def _prenorm_kernel(x_ref, g_ref, scale_ref, shift_ref, o_ref):
    h = _rms(x_ref[...], g_ref[...]) * (1.0 + scale_ref[...]) + shift_ref[...]
    o_ref[...] = h.astype(o_ref.dtype)


def _prenorm(x, g, mod, scale_idx, shift_idx):
    s, d = x.shape
    tm = min(256, s)
    return pl.pallas_call(
        _prenorm_kernel,
        out_shape=jax.ShapeDtypeStruct((s, d), BF16),
        grid=(s // tm,),
        in_specs=[pl.BlockSpec((tm, d), lambda i: (i, 0)),
                  pl.BlockSpec((1, d), lambda i: (0, 0)),
                  pl.BlockSpec((1, d), lambda i: (0, scale_idx)),
                  pl.BlockSpec((1, d), lambda i: (0, shift_idx))],
        out_specs=pl.BlockSpec((tm, d), lambda i: (i, 0)),
        compiler_params=_params(("arbitrary",)),
        name="prenorm",
    )(x, g.reshape(1, d), mod, mod)


def _inproj_kernel(a_ref, w_ref, b_ref, o_ref):
    acc = jnp.dot(a_ref[...], w_ref[...].astype(BF16), preferred_element_type=F32)
    o_ref[...] = acc + b_ref[...]


def _inproj(h, w, b):
    m, k = h.shape
    n = w.shape[1]
    tm = min(1024, m)
    tn = 512
    return pl.pallas_call(
        _inproj_kernel,
        out_shape=jax.ShapeDtypeStruct((m, n), F32),
        grid=(m // tm, pl.cdiv(n, tn)),
        in_specs=[pl.BlockSpec((tm, k), lambda i, j: (i, 0)),
                  pl.BlockSpec((k, tn), lambda i, j: (0, j)),
                  pl.BlockSpec((1, tn), lambda i, j: (0, j))],
        out_specs=pl.BlockSpec((tm, tn), lambda i, j: (i, j)),
        compiler_params=_params(("arbitrary", "arbitrary")),
        name="in_projection",
    )(h, w, b.reshape(1, n))


def _outproj_kernel(a1_ref, a2_ref, w_ref, b_ref, o_ref):
    k1 = a1_ref.shape[1]
    w = w_ref[...].astype(BF16)
    acc = jnp.dot(a1_ref[...], w[:k1], preferred_element_type=F32)
    acc = acc + jnp.dot(a2_ref[...], w[k1:], preferred_element_type=F32)
    o_ref[...] = acc + b_ref[...]


def _outproj(o_a, o_f, w, b):
    m, k1 = o_a.shape
    k2 = o_f.shape[1]
    n = w.shape[1]
    tm = min(1024, m)
    tn = min(512, n)
    return pl.pallas_call(
        _outproj_kernel,
        out_shape=jax.ShapeDtypeStruct((m, n), F32),
        grid=(m // tm, n // tn),
        in_specs=[pl.BlockSpec((tm, k1), lambda i, j: (i, 0)),
                  pl.BlockSpec((tm, k2), lambda i, j: (i, 0)),
                  pl.BlockSpec((k1 + k2, tn), lambda i, j: (0, j)),
                  pl.BlockSpec((1, tn), lambda i, j: (0, j))],
        out_specs=pl.BlockSpec((tm, tn), lambda i, j: (i, j)),
        compiler_params=_params(("arbitrary", "arbitrary")),
        name="out_projection",
    )(o_a, o_f, w, b.reshape(1, n))


def _swap_halves(x, first_half):
    return jnp.where(first_half, pltpu.roll(x, LANES - HEAD_DIM // 2, 1),
                     pltpu.roll(x, HEAD_DIM // 2, 1))


def _swa_kernel(sinks_ref, posc_ref, posp_ref, invf_ref, q_ref, kp_ref, kc_ref, vp_ref, vc_ref, o_ref):
    n = pl.program_id(0)
    w = WINDOW
    lane = lax.broadcasted_iota(jnp.int32, (w, LANES), 1)
    first_half = (lane % HEAD_DIM) < (HEAD_DIM // 2)
    lo64 = lane < HEAD_DIM
    lane2 = lax.broadcasted_iota(jnp.int32, (2 * w, LANES), 1)
    lo64_2 = lane2 < HEAD_DIM

    def tables(pos_ref):
        ang = pos_ref[...].astype(F32) * invf_ref[...]
        sin = jnp.sin(ang)
        return jnp.cos(ang), jnp.where(first_half, -sin, sin)

    cos_c, sin_c = tables(posc_ref)
    cos_p, sin_p = tables(posp_ref)

    def rope(t, cos, sin):
        return t * cos + _swap_halves(t, first_half) * sin

    qi = lax.broadcasted_iota(jnp.int32, (w, 2 * w), 0)
    kj = lax.broadcasted_iota(jnp.int32, (w, 2 * w), 1)
    diff = qi + w - kj
    valid = (diff >= 0) & (diff < w) & ((n * w - w + kj) >= 0)

    scale = HEAD_DIM ** -0.5
    nt_dims = (((1,), (1,)), ((), ()))
    for kvp in range(SWA_KV_HEADS // 2):
        cols = slice(kvp * LANES, (kvp + 1) * LANES)
        k2 = jnp.concatenate([rope(kp_ref[:, cols], cos_p, sin_p),
                              rope(kc_ref[:, cols], cos_c, sin_c)], axis=0)
        v2 = jnp.concatenate([vp_ref[:, cols], vc_ref[:, cols]], axis=0)
        k2r = pltpu.roll(k2, HEAD_DIM, 1)
        v2r = pltpu.roll(v2, HEAD_DIM, 1)
        for sub in range(2):
            hkv = 2 * kvp + sub
            src_k, alt_k = (k2, k2r) if sub == 0 else (k2r, k2)
            src_v, alt_v = (v2, v2r) if sub == 0 else (v2r, v2)
            ka = jnp.where(lo64_2, src_k, 0.0).astype(BF16)
            kb = jnp.where(lo64_2, 0.0, alt_k).astype(BF16)
            va = jnp.where(lo64_2, src_v, 0.0).astype(BF16)
            vb = jnp.where(lo64_2, 0.0, alt_v).astype(BF16)
            for gp in range(SWA_GROUP // 2):
                pair = hkv * (SWA_GROUP // 2) + gp
                qcols = slice(pair * LANES, (pair + 1) * LANES)
                q = (rope(q_ref[:, qcols], cos_c, sin_c) * scale).astype(BF16)
                o = jnp.zeros((w, LANES), F32)
                for which, (kk, vv) in enumerate(((ka, va), (kb, vb))):
                    sink = sinks_ref[2 * pair + which]
                    s = lax.dot_general(q, kk, nt_dims, preferred_element_type=F32)
                    s = jnp.where(valid, s, NEG)
                    m = jnp.maximum(jnp.max(s, axis=-1, keepdims=True), sink)
                    p = jnp.exp(s - m)
                    den = jnp.sum(p, axis=-1, keepdims=True) + jnp.exp(sink - m)
                    pv = jnp.dot(p.astype(BF16), vv, preferred_element_type=F32)
                    o = o + pv * (1.0 / den)
                o_ref[:, qcols] = o.astype(o_ref.dtype)


def _swa_attention(proj, positions, sinks):
    s = proj.shape[0]
    w = WINDOW
    nb = s // w
    half = HEAD_DIM // 2
    inv_freq = jnp.power(ROPE_THETA, -jnp.arange(half, dtype=F32) * (2.0 / HEAD_DIM))
    invf = jnp.tile(inv_freq, LANES // half).reshape(1, LANES)
    pos_col = positions.reshape(s, 1)
    prev = lambda n: jnp.maximum(n - 1, 0)
    ka_blk = OFF_KA // SWA_KV_W
    va_blk = OFF_VA // SWA_KV_W
    return pl.pallas_call(
        _swa_kernel,
        out_shape=jax.ShapeDtypeStruct((s, SWA_Q_W), BF16),
        grid=(nb,),
        in_specs=[pl.BlockSpec(memory_space=pltpu.SMEM),
                  pl.BlockSpec((w, 1), lambda n: (n, 0)),
                  pl.BlockSpec((w, 1), lambda n: (prev(n), 0)),
                  pl.BlockSpec((1, LANES), lambda n: (0, 0)),
                  pl.BlockSpec((w, SWA_Q_W), lambda n: (n, 0)),
                  pl.BlockSpec((w, SWA_KV_W), lambda n: (prev(n), ka_blk)),
                  pl.BlockSpec((w, SWA_KV_W), lambda n: (n, ka_blk)),
                  pl.BlockSpec((w, SWA_KV_W), lambda n: (prev(n), va_blk)),
                  pl.BlockSpec((w, SWA_KV_W), lambda n: (n, va_blk))],
        out_specs=pl.BlockSpec((w, SWA_Q_W), lambda n: (n, 0)),
        compiler_params=_params(("arbitrary",)),
        name="swa_sink_attention",
    )(sinks, pos_col, pos_col, invf, proj, proj, proj, proj, proj)


CUM_ROWS = 256


def _cum_kernel(f_ref, o_ref, carry_ref):
    @pl.when(pl.program_id(0) == 0)
    def _():
        carry_ref[...] = jnp.zeros_like(carry_ref)

    f = f_ref[:, :FOX_HEADS]
    ls = jnp.minimum(f, 0.0) - jnp.log1p(jnp.exp(-jnp.abs(f)))
    r = lax.broadcasted_iota(jnp.int32, (CUM_ROWS, CUM_ROWS), 0)
    c = lax.broadcasted_iota(jnp.int32, (CUM_ROWS, CUM_ROWS), 1)
    tri = (c <= r).astype(F32)
    cum = jnp.dot(tri, ls, preferred_element_type=F32, precision=lax.Precision.HIGHEST) + carry_ref[...]
    o_ref[...] = cum
    carry_ref[...] = cum[CUM_ROWS - 1:CUM_ROWS, :]


def _fox_cum(proj):
    s = proj.shape[0]
    return pl.pallas_call(
        _cum_kernel,
        out_shape=jax.ShapeDtypeStruct((s, FOX_HEADS), F32),
        grid=(s // CUM_ROWS,),
        in_specs=[pl.BlockSpec((CUM_ROWS, LANES), lambda i: (i, OFF_FL // LANES))],
        out_specs=pl.BlockSpec((CUM_ROWS, FOX_HEADS), lambda i: (i, 0)),
        scratch_shapes=[pltpu.VMEM((1, FOX_HEADS), F32)],
        compiler_params=_params(("arbitrary",)),
        name="fox_decay_cumsum",
    )(proj)


FOX_BLK = 512
AUG_CQ = HEAD_DIM
AUG_ONE = HEAD_DIM + 3
AUG_END = HEAD_DIM + 6


def _split3(c):
    hi = c.astype(BF16).astype(F32)
    r = c - hi
    mid = r.astype(BF16).astype(F32)
    lo = (r - mid).astype(BF16).astype(F32)
    return hi, mid, lo


def _fox_prep_kernel(q_ref, k_ref, v_ref, cum_ref, qo_ref, ko_ref, vo_ref):
    j = pl.program_id(0)
    tb = q_ref.shape[0]
    lane = lax.broadcasted_iota(jnp.int32, (tb, LANES), 1)
    lo64 = lane < HEAD_DIM
    lane_h = lax.broadcasted_iota(jnp.int32, (tb, FOX_HEADS), 1)
    cum = cum_ref[...]
    scale = HEAD_DIM ** -0.5
    for hh in range(2):
        h = 2 * j + hh
        col = jnp.sum(jnp.where(lane_h == h, cum, 0.0), axis=-1, keepdims=True)
        hi, mid, lo = _split3(col - col[0:1, :])
        q = q_ref[...] * scale
        k = k_ref[...]
        v = v_ref[...]
        if hh == 1:
            q = pltpu.roll(q, HEAD_DIM, 1)
            k = pltpu.roll(k, HEAD_DIM, 1)
            v = pltpu.roll(v, HEAD_DIM, 1)
        q_aug = jnp.where(lane == AUG_CQ, hi, jnp.where(lane == AUG_CQ + 1, mid,
                jnp.where(lane == AUG_CQ + 2, lo, jnp.where(lane < AUG_END, 1.0, 0.0))))
        k_aug = jnp.where(lane < AUG_ONE, 1.0, jnp.where(lane == AUG_ONE, -hi,
                jnp.where(lane == AUG_ONE + 1, -mid, jnp.where(lane == AUG_ONE + 2, -lo, 0.0))))
        v_aug = jnp.where(lane == HEAD_DIM, 1.0, 0.0)
        qo_ref[hh] = jnp.where(lo64, q, q_aug).astype(BF16)
        ko_ref[hh] = jnp.where(lo64, k, k_aug).astype(BF16)
        vo_ref[hh] = jnp.where(lo64, v, v_aug).astype(BF16)


def _fox_prep(proj, cum, tb):
    s = proj.shape[0]
    npairs = FOX_HEADS // 2
    shp = jax.ShapeDtypeStruct((FOX_HEADS, s, LANES), BF16)
    ospec = pl.BlockSpec((2, tb, LANES), lambda j, i: (j, i, 0))
    return pl.pallas_call(
        _fox_prep_kernel,
        out_shape=(shp, shp, shp),
        grid=(npairs, s // tb),
        in_specs=[pl.BlockSpec((tb, LANES), lambda j, i: (i, OFF_QF // LANES + j)),
                  pl.BlockSpec((tb, LANES), lambda j, i: (i, OFF_KF // LANES + j)),
                  pl.BlockSpec((tb, LANES), lambda j, i: (i, OFF_VF // LANES + j)),
                  pl.BlockSpec((tb, FOX_HEADS), lambda j, i: (i, 0))],
        out_specs=(ospec, ospec, ospec),
        compiler_params=_params(("arbitrary", "arbitrary")),
        name="fox_prepare",
    )(proj, proj, proj, cum)


def _fox_kernel(a_ref, q_ref, k_ref, v_ref, o_ref):
    j = pl.program_id(0)
    i = pl.program_id(1)
    tb = q_ref.shape[1]
    nt_dims = (((1,), (1,)), ((), ()))
    row = lax.broadcasted_iota(jnp.int32, (tb, tb), 0)
    col = lax.broadcasted_iota(jnp.int32, (tb, tb), 1)
    causal = col <= row

    def one_head(hh, kt, m, acc, masked):
        h = 2 * j + hh
        rows = pl.ds(pl.multiple_of(kt * tb, tb), tb)
        s = lax.dot_general(q_ref[hh], k_ref[hh, rows, :], nt_dims, preferred_element_type=F32)
        if masked:
            s = jnp.where(causal, s, NEG)
        delta = a_ref[h, i] - a_ref[h, kt]
        m_new = jnp.maximum(m, jnp.max(s, axis=-1, keepdims=True) + delta)
        p = jnp.exp(s - (m_new - delta))
        alpha = jnp.exp(m - m_new)
        acc = alpha * acc + jnp.dot(p.astype(BF16), v_ref[hh, rows, :], preferred_element_type=F32)
        return m_new, acc

    def step(kt, carry, masked=False):
        m0, acc0, m1, acc1 = carry
        m0, acc0 = one_head(0, kt, m0, acc0, masked)
        m1, acc1 = one_head(1, kt, m1, acc1, masked)
        return m0, acc0, m1, acc1

    init = (jnp.full((tb, 1), NEG, F32), jnp.zeros((tb, LANES), F32),
            jnp.full((tb, 1), NEG, F32), jnp.zeros((tb, LANES), F32))
    carry = lax.fori_loop(0, i, step, init)
    _, acc0, _, acc1 = step(i, carry, masked=True)
    lo64 = lax.broadcasted_iota(jnp.int32, (tb, LANES), 1) < HEAD_DIM
    o0 = acc0 * (1.0 / acc0[:, HEAD_DIM:HEAD_DIM + 1])
    o1 = acc1 * (1.0 / acc1[:, HEAD_DIM:HEAD_DIM + 1])
    o_ref[...] = jnp.where(lo64, o0, pltpu.roll(o1, HEAD_DIM, 1)).astype(o_ref.dtype)


def _fox_attention(proj):
    s = proj.shape[0]
    tb = min(FOX_BLK, s)
    cum = _fox_cum(proj)
    q_aug, k_aug, v_aug = _fox_prep(proj, cum, tb)
    block_ref = cum[::tb].T
    npairs = FOX_HEADS // 2
    return pl.pallas_call(
        _fox_kernel,
        out_shape=jax.ShapeDtypeStruct((s, FOX_W), BF16),
        grid=(npairs, s // tb),
        in_specs=[pl.BlockSpec(memory_space=pltpu.SMEM),
                  pl.BlockSpec((2, tb, LANES), lambda j, i: (j, i, 0)),
                  pl.BlockSpec((2, s, LANES), lambda j, i: (j, 0, 0)),
                  pl.BlockSpec((2, s, LANES), lambda j, i: (j, 0, 0))],
        out_specs=pl.BlockSpec((tb, LANES), lambda j, i: (i, j)),
        compiler_params=_params(("arbitrary", "arbitrary")),
        name="fox_attention",
    )(block_ref, q_aug, k_aug, v_aug)


def _lane_pack(cols, width, dtype):
    rows = cols[0].shape[0]
    lane = lax.broadcasted_iota(jnp.int32, (rows, width), 1)
    out = jnp.zeros((rows, width), dtype)
    for k, cval in enumerate(cols):
        out = jnp.where(lane == k, cval.astype(dtype), out)
    return out


def _postmix_kernel(x_ref, y_ref, gpost_ref, gate_ref, gpre_ref, scale_ref, shift_ref, wr_ref, br_ref,
                    x1_ref, h2_ref, idx_ref, prob_ref, rank_ref, cnt_ref, carry_ref):
    step = pl.program_id(0)

    @pl.when(step == 0)
    def _():
        carry_ref[...] = jnp.zeros_like(carry_ref)

    x1 = x_ref[...] + gate_ref[...] * _rms(y_ref[...], gpost_ref[...])
    x1_ref[...] = x1
    h2 = _rms(x1, gpre_ref[...]) * (1.0 + scale_ref[...]) + shift_ref[...]
    h2_ref[...] = h2
    logits = jnp.dot(h2, wr_ref[...], preferred_element_type=F32,
                     precision=lax.Precision.HIGHEST) + br_ref[...]
    tm = logits.shape[0]
    lane_e = lax.broadcasted_iota(jnp.int32, (tm, N_EXPERTS), 1).astype(F32)
    vals, idxs, sels = [], [], []
    cur = logits
    for _ in range(TOP_K):
        mk = jnp.max(cur, axis=-1, keepdims=True)
        ik = jnp.min(jnp.where(cur == mk, lane_e, float(N_EXPERTS)), axis=-1, keepdims=True)
        sel = lane_e == ik
        vals.append(mk)
        idxs.append(ik)
        sels.append(sel)
        cur = jnp.where(sel, -jnp.inf, cur)
    exps = [jnp.exp(v - vals[0]) for v in vals]
    inv = 1.0 / functools.reduce(lambda a, b: a + b, exps)
    probs = [e * inv for e in exps]

    onehot = functools.reduce(lambda a, b: a | b, sels).astype(F32)
    r = lax.broadcasted_iota(jnp.int32, (tm, tm), 0)
    c = lax.broadcasted_iota(jnp.int32, (tm, tm), 1)
    strict = (c < r).astype(BF16)
    before = jnp.dot(strict, onehot.astype(BF16), preferred_element_type=F32) + carry_ref[...]
    ranks = [jnp.sum(jnp.where(sel, before, 0.0), axis=-1, keepdims=True) for sel in sels]
    carry_ref[...] = carry_ref[...] + jnp.sum(onehot, axis=0, keepdims=True)

    idx_ref[...] = _lane_pack(idxs, LANES, jnp.int32)
    prob_ref[...] = _lane_pack(probs, LANES, F32)
    rank_ref[...] = _lane_pack([rk.astype(jnp.int32) for rk in ranks], LANES, jnp.int32)
    cnt_ref[...] = carry_ref[...].astype(jnp.int32)


def _postmix_router(x, y, mod, g_post, g_pre, w_router, b_router):
    s, d = x.shape
    tm = min(256, s)
    row = pl.BlockSpec((tm, d), lambda i: (i, 0))
    vec = lambda idx: pl.BlockSpec((1, d), lambda i: (0, idx))
    lanes = pl.BlockSpec((tm, LANES), lambda i: (i, 0))
    return pl.pallas_call(
        _postmix_kernel,
        out_shape=(jax.ShapeDtypeStruct((s, d), F32), jax.ShapeDtypeStruct((s, d), F32),
                   jax.ShapeDtypeStruct((s, LANES), jnp.int32), jax.ShapeDtypeStruct((s, LANES), F32),
                   jax.ShapeDtypeStruct((s, LANES), jnp.int32),
                   jax.ShapeDtypeStruct((1, N_EXPERTS), jnp.int32)),
        grid=(s // tm,),
        in_specs=[row, row, vec(0), vec(2), vec(0), vec(4), vec(3),
                  pl.BlockSpec((d, N_EXPERTS), lambda i: (0, 0)),
                  pl.BlockSpec((1, N_EXPERTS), lambda i: (0, 0))],
        out_specs=(row, row, lanes, lanes, lanes, pl.BlockSpec((1, N_EXPERTS), lambda i: (0, 0))),
        scratch_shapes=[pltpu.VMEM((1, N_EXPERTS), F32)],
        compiler_params=_params(("arbitrary",)),
        name="postmix_router",
    )(x, y, g_post.reshape(1, d), mod, g_pre.reshape(1, d), mod, mod, w_router, b_router.reshape(1, N_EXPERTS))


MOE_CAP = 1280
MOE_RB = 256
GU_TN = 512
DEINT = 256
DOWN_TN = 1024


def _item_maps(n_j):
    def clamp(i, j, ni):
        used = i < ni[0]
        return jnp.where(used, i, ni[0] - 1), jnp.where(used, j, n_j - 1)

    def item_rows(i, j, ie, ir, ni):
        return clamp(i, j, ni)[0], 0, 0

    def act_rows(i, j, ie, ir, ni):
        return clamp(i, j, ni)[0], 0

    def weights(i, j, ie, ir, ni):
        ii, jj = clamp(i, j, ni)
        return ie[ii], 0, jj

    return item_rows, act_rows, weights


def _gateup_kernel(ie_ref, ir_ref, ni_ref, src_ref, h_ref, w_ref, b_ref, o_ref, stg_ref, xb_ref, wb_ref, sem):
    i = pl.program_id(0)
    j = pl.program_id(1)
    cap = o_ref.shape[0]

    @pl.when(i >= ni_ref[0])
    def _():
        o_ref[...] = jnp.zeros_like(o_ref)

    @pl.when(i < ni_ref[0])
    def _():
        rows = ir_ref[i]
        nblk = (rows + MOE_RB - 1) // MOE_RB

        def row_copy(c, r, slot):
            return pltpu.make_async_copy(h_ref.at[pl.ds(src_ref[0, 0, c * MOE_RB + r], 1)],
                                         stg_ref.at[slot, pl.ds(r, 1)], sem.at[slot])

        def start_chunk(c, slot):
            n = jnp.minimum(rows - c * MOE_RB, MOE_RB)
            lax.fori_loop(0, n, lambda r, carry: (row_copy(c, r, slot).start(), carry)[1], 0)

        def wait_chunk(c, slot):
            n = jnp.minimum(rows - c * MOE_RB, MOE_RB)
            lax.fori_loop(0, n, lambda r, carry: (row_copy(c, 0, slot).wait(), carry)[1], 0)

        @pl.when(j == 0)
        def _():
            @pl.when(i == 0)
            def _():
                stg_ref[...] = jnp.zeros_like(stg_ref)

            start_chunk(0, 0)

            def chunk(c, carry):
                slot = c % 2

                @pl.when(c + 1 < nblk)
                def _():
                    start_chunk(c + 1, 1 - slot)

                wait_chunk(c, slot)
                xb_ref[pl.ds(pl.multiple_of(c * MOE_RB, MOE_RB), MOE_RB), :] = stg_ref[slot].astype(BF16)
                return carry

            lax.fori_loop(0, nblk, chunk, 0)

        wb_ref[...] = w_ref[...].astype(BF16)
        r = lax.broadcasted_iota(jnp.int32, (DEINT, DEINT), 0)
        c = lax.broadcasted_iota(jnp.int32, (DEINT, DEINT), 1)
        half = DEINT // 2
        perm = (r == jnp.where(c < half, 2 * c, 2 * (c - half) + 1)).astype(BF16)

        def block(blk, carry):
            rs = pl.ds(pl.multiple_of(blk * MOE_RB, MOE_RB), MOE_RB)
            z = jnp.dot(xb_ref[rs, :], wb_ref[...], preferred_element_type=F32) + b_ref[...]
            for t in range(GU_TN // DEINT):
                zz = jnp.dot(z[:, t * DEINT:(t + 1) * DEINT].astype(BF16), perm, preferred_element_type=F32)
                glu = jnp.minimum(zz[:, :half], SWIGLU_LIMIT)
                lin = jnp.clip(zz[:, half:], -SWIGLU_LIMIT, SWIGLU_LIMIT)
                act = glu * jax.nn.sigmoid(SWIGLU_ALPHA * glu) * (lin + 1.0)
                o_ref[rs, t * half:(t + 1) * half] = act.astype(o_ref.dtype)
            return carry

        lax.fori_loop(0, nblk, block, 0)

        def zero_block(blk, carry):
            rs = pl.ds(pl.multiple_of(blk * MOE_RB, MOE_RB), MOE_RB)
            o_ref[rs, :] = jnp.zeros((MOE_RB, o_ref.shape[1]), o_ref.dtype)
            return carry

        lax.fori_loop(nblk, cap // MOE_RB, zero_block, 0)


def _moe_gateup(h2, src, item_expert, item_rows, num_items, w_gate_up, b_gate_up, cap):
    s, d = h2.shape
    n_items = src.shape[0]
    n_j = (2 * D_FF) // GU_TN
    item_map, _, weights = _item_maps(n_j)
    grid_spec = pltpu.PrefetchScalarGridSpec(
        num_scalar_prefetch=3,
        grid=(n_items, n_j),
        in_specs=[pl.BlockSpec((1, 1, cap), item_map, memory_space=pltpu.SMEM),
                  pl.BlockSpec(memory_space=pl.ANY),
                  pl.BlockSpec((None, d, GU_TN), weights),
                  pl.BlockSpec((None, 1, GU_TN), weights)],
        out_specs=pl.BlockSpec((cap, GU_TN // 2), lambda i, j, ie, ir, ni: (i, j)),
        scratch_shapes=[pltpu.VMEM((2, MOE_RB, d), F32), pltpu.VMEM((cap, d), BF16),
                        pltpu.VMEM((d, GU_TN), BF16), pltpu.SemaphoreType.DMA((2,))],
    )
    return pl.pallas_call(
        _gateup_kernel,
        out_shape=jax.ShapeDtypeStruct((n_items * cap, D_FF), BF16),
        grid_spec=grid_spec,
        compiler_params=_params(("arbitrary", "arbitrary")),
        name="moe_gate_up",
    )(item_expert, item_rows, num_items, src, h2, w_gate_up, b_gate_up.reshape(N_EXPERTS, 1, 2 * D_FF))


def _down_kernel(ie_ref, ir_ref, ni_ref, dst_ref, a_ref, w_ref, b_ref, o_ref, y_ref, wb_ref, sem):
    i = pl.program_id(0)
    j = pl.program_id(1)
    tn = w_ref.shape[1]
    n_j = y_ref.shape[1] // tn

    @pl.when(i < ni_ref[0])
    def _():
        rows = ir_ref[i]
        nblk = (rows + MOE_RB - 1) // MOE_RB
        wb_ref[...] = w_ref[...].astype(BF16)

        for jj in range(n_j):
            @pl.when(j == jj)
            def _(jj=jj):
                def block(blk, carry):
                    rs = pl.ds(pl.multiple_of(blk * MOE_RB, MOE_RB), MOE_RB)
                    y_ref[rs, jj * tn:(jj + 1) * tn] = (
                        jnp.dot(a_ref[rs, :], wb_ref[...], preferred_element_type=F32) + b_ref[...])
                    return carry

                lax.fori_loop(0, nblk, block, 0)

        @pl.when(j == n_j - 1)
        def _():
            def row_copy(r, dst_row):
                return pltpu.make_async_copy(y_ref.at[pl.ds(r, 1)], o_ref.at[pl.ds(dst_row, 1)], sem)

            lax.fori_loop(0, rows, lambda r, carry: (row_copy(r, dst_ref[0, 0, r]).start(), carry)[1], 0)
            lax.fori_loop(0, rows, lambda r, carry: (row_copy(0, 0).wait(), carry)[1], 0)


def _moe_down(act, dst, item_expert, item_rows, num_items, w_down, b_down, n_out_rows, cap):
    d = w_down.shape[2]
    tn = min(DOWN_TN, d)
    n_items = dst.shape[0]
    n_j = d // tn
    item_map, act_rows, weights = _item_maps(n_j)
    grid_spec = pltpu.PrefetchScalarGridSpec(
        num_scalar_prefetch=3,
        grid=(n_items, n_j),
        in_specs=[pl.BlockSpec((1, 1, cap), item_map, memory_space=pltpu.SMEM),
                  pl.BlockSpec((cap, D_FF), act_rows),
                  pl.BlockSpec((None, D_FF, tn), weights),
                  pl.BlockSpec((None, 1, tn), weights)],
        out_specs=pl.BlockSpec(memory_space=pl.ANY),
        scratch_shapes=[pltpu.VMEM((cap, d), F32), pltpu.VMEM((D_FF, tn), BF16), pltpu.SemaphoreType.DMA(())],
    )
    return pl.pallas_call(
        _down_kernel,
        out_shape=jax.ShapeDtypeStruct((n_out_rows, d), F32),
        grid_spec=grid_spec,
        compiler_params=_params(("arbitrary", "arbitrary")),
        name="moe_down",
    )(item_expert, item_rows, num_items, dst, act, w_down, b_down.reshape(N_EXPERTS, 1, d))


COMBINE_TOKENS = 128


def _combine_kernel(prob_ref, x1_ref, g_ref, gate_ref, ys_ref, o_ref):
    prob = prob_ref[...]
    y = prob[:, 0:1] * ys_ref[0]
    for k in range(1, TOP_K):
        y = y + prob[:, k:k + 1] * ys_ref[k]
    o_ref[...] = x1_ref[...] + gate_ref[...] * _rms(y, g_ref[...])


def _moe_combine(ys, probs, x1, g_post, mod, gate_idx):
    s, d = x1.shape
    tc = min(COMBINE_TOKENS, s)
    return pl.pallas_call(
        _combine_kernel,
        out_shape=jax.ShapeDtypeStruct((s, d), F32),
        grid=(s // tc,),
        in_specs=[pl.BlockSpec((tc, LANES), lambda i: (i, 0)),
                  pl.BlockSpec((tc, d), lambda i: (i, 0)),
                  pl.BlockSpec((1, d), lambda i: (0, 0)),
                  pl.BlockSpec((1, d), lambda i: (0, gate_idx)),
                  pl.BlockSpec((TOP_K, tc, d), lambda i: (0, i, 0))],
        out_specs=pl.BlockSpec((tc, d), lambda i: (i, 0)),
        compiler_params=_params(("arbitrary",)),
        name="moe_combine",
    )(probs, x1, g_post.reshape(1, d), mod, ys.reshape(TOP_K, s, d))


def _moe(h2, idx, probs, rank, counts, x1, mod, g_post, w_gate_up, b_gate_up, w_down, b_down):
    s, d = h2.shape
    cap = MOE_CAP
    i32 = jnp.int32
    counts = counts.reshape(N_EXPERTS)
    items_per_expert = (counts + cap - 1) // cap
    item_end = jnp.cumsum(items_per_expert)
    first_item = (item_end - items_per_expert).astype(i32)
    n_items_max = (s * TOP_K + N_EXPERTS * (cap - 1)) // cap
    num_items = item_end[-1].astype(i32).reshape(1)
    item_ids = jnp.arange(n_items_max, dtype=i32)
    item_expert = jnp.minimum(jnp.sum(item_end[None, :] <= item_ids[:, None], axis=1), N_EXPERTS - 1).astype(i32)
    item_rows = jnp.clip(counts[item_expert] - cap * (item_ids - first_item[item_expert]), 0, cap)
    item_rows = jnp.where(item_ids < num_items[0], item_rows, 0).astype(i32)

    pos = (first_item[idx[:, :TOP_K]] * cap + rank[:, :TOP_K]).reshape(-1)
    tok = jnp.repeat(jnp.arange(s, dtype=i32), TOP_K)
    slot = jnp.tile(jnp.arange(TOP_K, dtype=i32), s)
    blank = jnp.zeros((n_items_max * cap,), i32)
    src = blank.at[pos].set(tok, unique_indices=True).reshape(n_items_max, 1, cap)
    dst = blank.at[pos].set(slot * s + tok, unique_indices=True).reshape(n_items_max, 1, cap)

    act = _moe_gateup(h2, src, item_expert, item_rows, num_items, w_gate_up, b_gate_up, cap)
    ys = _moe_down(act, dst, item_expert, item_rows, num_items, w_down, b_down, TOP_K * s, cap)
    return _moe_combine(ys, probs, x1, g_post, mod, 5)


def _layer(x, c, positions, w_mod, b_mod, g_pre_mix, g_post_mix, g_pre_ffn, g_post_ffn,
           w_in, b_in, sinks, w_out, b_out, w_router, b_router, w_gate_up, b_gate_up, w_down, b_down):
    mod = _modulation(c, w_mod, b_mod)
    h = _prenorm(x, g_pre_mix, mod, 1, 0)
    proj = _inproj(h, w_in, b_in)
    o_a = _swa_attention(proj, positions, sinks)
    o_f = _fox_attention(proj)
    y = _outproj(o_a, o_f, w_out, b_out)
    x1, h2, idx, probs, rank, counts = _postmix_router(x, y, mod, g_post_mix, g_pre_ffn, w_router, b_router)
    return _moe(h2, idx, probs, rank, counts, x1, mod, g_post_ffn, w_gate_up, b_gate_up, w_down, b_down)


def kernel(x, c, positions, w_mod, b_mod, g_pre_mix, g_post_mix, g_pre_ffn, g_post_ffn, w_in, b_in, sinks,
           w_out, b_out, w_router, b_router, w_gate_up, b_gate_up, w_down, b_down):
    batch, seq, d = x.shape
    assert batch == 1 and w_mod.shape[0] == 1, "one sequence, one layer"
    assert seq % FOX_BLK == 0 or seq < FOX_BLK
    out = _layer(x[0], c[0], positions[0], w_mod[0], b_mod[0], g_pre_mix[0], g_post_mix[0], g_pre_ffn[0],
                 g_post_ffn[0], w_in[0], b_in[0], sinks[0], w_out[0], b_out[0], w_router[0], b_router[0],
                 w_gate_up[0], b_gate_up[0], w_down[0], b_down[0])
    return out[None]
```

```python
import functools

import jax
import jax.numpy as jnp
import numpy as np
from jax import lax
from jax.experimental import pallas as pl
from jax.experimental.pallas import tpu as pltpu

HEAD_DIM = 64
SWA_Q_HEADS = 32
SWA_KV_HEADS = 4
SWA_GROUP = SWA_Q_HEADS // SWA_KV_HEADS
WINDOW = 128
FOX_HEADS = 32
ROPE_THETA = 10000.0
N_EXPERTS = 32
TOP_K = 4
D_FF = 1536
SWIGLU_LIMIT = 7.0
SWIGLU_ALPHA = 1.702
RMS_EPS = 1e-6
N_MOD = 6

SWA_Q_W = SWA_Q_HEADS * HEAD_DIM
SWA_KV_W = SWA_KV_HEADS * HEAD_DIM
FOX_W = FOX_HEADS * HEAD_DIM
MIX_W = SWA_Q_W + FOX_W
IN_W = SWA_Q_W + 2 * SWA_KV_W + 3 * FOX_W + FOX_HEADS
OFF_QA = 0
OFF_KA = SWA_Q_W
OFF_VA = OFF_KA + SWA_KV_W
OFF_QF = OFF_VA + SWA_KV_W
OFF_KF = OFF_QF + FOX_W
OFF_VF = OFF_KF + FOX_W
OFF_FL = OFF_VF + FOX_W

LANES = 128
SUBLANES = 8
V7X_VMEM_BYTES = 64 * 1024 * 1024
VMEM_LIMIT = V7X_VMEM_BYTES - 8 * 1024 * 1024

NEG = -1e30
LOG2E = 1.4426950408889634

F32 = jnp.float32
BF16 = jnp.bfloat16


def _params(sem, vmem=VMEM_LIMIT):
    return pltpu.CompilerParams(dimension_semantics=sem, vmem_limit_bytes=vmem)


MOD_ROWS = 256


def _mod_kernel(c_ref, w_ref, b_ref, o_ref):
    d, tn = w_ref.shape

    def body(r, acc):
        rows = pl.ds(pl.multiple_of(r * MOD_ROWS, MOD_ROWS), MOD_ROWS)
        c = c_ref[rows, :]
        sc = c * jax.nn.sigmoid(c)
        prod = w_ref[rows, :] * sc
        return acc + jnp.sum(prod.reshape(MOD_ROWS // SUBLANES, SUBLANES, tn), axis=0)

    acc = lax.fori_loop(0, d // MOD_ROWS, body, jnp.zeros((SUBLANES, tn), F32))
    o_ref[...] = jnp.sum(acc, axis=0, keepdims=True) + b_ref[...]


def _modulation(c, w_mod, b_mod):
    d, n = w_mod.shape
    tn = min(1024, n)
    assert n % tn == 0 and d % MOD_ROWS == 0
    return pl.pallas_call(
        _mod_kernel,
        out_shape=jax.ShapeDtypeStruct((1, n), F32),
        grid=(n // tn,),
        in_specs=[pl.BlockSpec((d, 1), lambda j: (0, 0)),
                  pl.BlockSpec((d, tn), lambda j: (0, j)),
                  pl.BlockSpec((1, tn), lambda j: (0, j))],
        out_specs=pl.BlockSpec((1, tn), lambda j: (0, j)),
        compiler_params=_params(("arbitrary",)),
        name="modulation",
    )(c.reshape(d, 1), w_mod, b_mod.reshape(1, n))


def _rms(x, g):
    inv = lax.rsqrt(jnp.mean(x * x, axis=-1, keepdims=True) + RMS_EPS)
    return x * inv * g


def _prenorm_kernel(x_ref, g_ref, scale_ref, shift_ref, o_ref):
    h = _rms(x_ref[...], g_ref[...]) * (1.0 + scale_ref[...]) + shift_ref[...]
    o_ref[...] = h.astype(o_ref.dtype)


def _prenorm(x, g, mod, scale_idx, shift_idx):
    s, d = x.shape
    tm = min(256, s)
    return pl.pallas_call(
        _prenorm_kernel,
        out_shape=jax.ShapeDtypeStruct((s, d), BF16),
        grid=(s // tm,),
        in_specs=[pl.BlockSpec((tm, d), lambda i: (i, 0)),
                  pl.BlockSpec((1, d), lambda i: (0, 0)),
                  pl.BlockSpec((1, d), lambda i: (0, scale_idx)),
                  pl.BlockSpec((1, d), lambda i: (0, shift_idx))],
        out_specs=pl.BlockSpec((tm, d), lambda i: (i, 0)),
        compiler_params=_params(("arbitrary",)),
        name="prenorm",
    )(x, g.reshape(1, d), mod, mod)


def _inproj_kernel(a_ref, w_ref, b_ref, o_ref):
    acc = jnp.dot(a_ref[...], w_ref[...].astype(BF16), preferred_element_type=F32)
    o_ref[...] = acc + b_ref[...]


def _inproj(h, w, b):
    m, k = h.shape
    n = w.shape[1]
    tm = min(1024, m)
    tn = 512
    return pl.pallas_call(
        _inproj_kernel,
        out_shape=jax.ShapeDtypeStruct((m, n), F32),
        grid=(m // tm, pl.cdiv(n, tn)),
        in_specs=[pl.BlockSpec((tm, k), lambda i, j: (i, 0)),
                  pl.BlockSpec((k, tn), lambda i, j: (0, j)),
                  pl.BlockSpec((1, tn), lambda i, j: (0, j))],
        out_specs=pl.BlockSpec((tm, tn), lambda i, j: (i, j)),
        compiler_params=_params(("arbitrary", "arbitrary")),
        name="in_projection",
    )(h, w, b.reshape(1, n))


def _outproj_kernel(a1_ref, a2_ref, w_ref, b_ref, o_ref):
    k1 = a1_ref.shape[1]
    w = w_ref[...].astype(BF16)
    acc = jnp.dot(a1_ref[...], w[:k1], preferred_element_type=F32)
    acc = acc + jnp.dot(a2_ref[...], w[k1:], preferred_element_type=F32)
    o_ref[...] = acc + b_ref[...]


def _outproj(o_a, o_f, w, b):
    m, k1 = o_a.shape
    k2 = o_f.shape[1]
    n = w.shape[1]
    tm = min(1024, m)
    tn = min(512, n)
    return pl.pallas_call(
        _outproj_kernel,
        out_shape=jax.ShapeDtypeStruct((m, n), F32),
        grid=(m // tm, n // tn),
        in_specs=[pl.BlockSpec((tm, k1), lambda i, j: (i, 0)),
                  pl.BlockSpec((tm, k2), lambda i, j: (i, 0)),
                  pl.BlockSpec((k1 + k2, tn), lambda i, j: (0, j)),
                  pl.BlockSpec((1, tn), lambda i, j: (0, j))],
        out_specs=pl.BlockSpec((tm, tn), lambda i, j: (i, j)),
        compiler_params=_params(("arbitrary", "arbitrary")),
        name="out_projection",
    )(o_a, o_f, w, b.reshape(1, n))


def _swap_halves(x, first_half):
    return jnp.where(first_half, pltpu.roll(x, LANES - HEAD_DIM // 2, 1),
                     pltpu.roll(x, HEAD_DIM // 2, 1))


def _swa_kernel(sinks_ref, posc_ref, posp_ref, invf_ref, q_ref, kp_ref, kc_ref, vp_ref, vc_ref, o_ref):
    n = pl.program_id(0)
    w = WINDOW
    lane = lax.broadcasted_iota(jnp.int32, (w, LANES), 1)
    first_half = (lane % HEAD_DIM) < (HEAD_DIM // 2)
    lo64 = lane < HEAD_DIM
    lane2 = lax.broadcasted_iota(jnp.int32, (2 * w, LANES), 1)
    lo64_2 = lane2 < HEAD_DIM

    def tables(pos_ref):
        ang = pos_ref[...].astype(F32) * invf_ref[...]
        sin = jnp.sin(ang)
        return jnp.cos(ang), jnp.where(first_half, -sin, sin)

    cos_c, sin_c = tables(posc_ref)
    cos_p, sin_p = tables(posp_ref)

    def rope(t, cos, sin):
        return t * cos + _swap_halves(t, first_half) * sin

    qi = lax.broadcasted_iota(jnp.int32, (w, 2 * w), 0)
    kj = lax.broadcasted_iota(jnp.int32, (w, 2 * w), 1)
    diff = qi + w - kj
    valid = (diff >= 0) & (diff < w) & ((n * w - w + kj) >= 0)

    scale = HEAD_DIM ** -0.5
    nt_dims = (((1,), (1,)), ((), ()))
    for kvp in range(SWA_KV_HEADS // 2):
        cols = slice(kvp * LANES, (kvp + 1) * LANES)
        k2 = jnp.concatenate([rope(kp_ref[:, cols], cos_p, sin_p),
                              rope(kc_ref[:, cols], cos_c, sin_c)], axis=0)
        v2 = jnp.concatenate([vp_ref[:, cols], vc_ref[:, cols]], axis=0)
        k2r = pltpu.roll(k2, HEAD_DIM, 1)
        v2r = pltpu.roll(v2, HEAD_DIM, 1)
        for sub in range(2):
            hkv = 2 * kvp + sub
            src_k, alt_k = (k2, k2r) if sub == 0 else (k2r, k2)
            src_v, alt_v = (v2, v2r) if sub == 0 else (v2r, v2)
            ka = jnp.where(lo64_2, src_k, 0.0).astype(BF16)
            kb = jnp.where(lo64_2, 0.0, alt_k).astype(BF16)
            va = jnp.where(lo64_2, src_v, 0.0).astype(BF16)
            vb = jnp.where(lo64_2, 0.0, alt_v).astype(BF16)
            for gp in range(SWA_GROUP // 2):
                pair = hkv * (SWA_GROUP // 2) + gp
                qcols = slice(pair * LANES, (pair + 1) * LANES)
                q = (rope(q_ref[:, qcols], cos_c, sin_c) * scale).astype(BF16)
                o = jnp.zeros((w, LANES), F32)
                for which, (kk, vv) in enumerate(((ka, va), (kb, vb))):
                    sink = sinks_ref[2 * pair + which]
                    s = lax.dot_general(q, kk, nt_dims, preferred_element_type=F32)
                    s = jnp.where(valid, s, NEG)
                    m = jnp.maximum(jnp.max(s, axis=-1, keepdims=True), sink)
                    p = jnp.exp(s - m)
                    den = jnp.sum(p, axis=-1, keepdims=True) + jnp.exp(sink - m)
                    pv = jnp.dot(p.astype(BF16), vv, preferred_element_type=F32)
                    o = o + pv * (1.0 / den)
                o_ref[:, qcols] = o.astype(o_ref.dtype)


def _swa_attention(proj, positions, sinks):
    s = proj.shape[0]
    w = WINDOW
    nb = s // w
    half = HEAD_DIM // 2
    inv_freq = jnp.power(ROPE_THETA, -jnp.arange(half, dtype=F32) * (2.0 / HEAD_DIM))
    invf = jnp.tile(inv_freq, LANES // half).reshape(1, LANES)
    pos_col = positions.reshape(s, 1)
    prev = lambda n: jnp.maximum(n - 1, 0)
    ka_blk = OFF_KA // SWA_KV_W
    va_blk = OFF_VA // SWA_KV_W
    return pl.pallas_call(
        _swa_kernel,
        out_shape=jax.ShapeDtypeStruct((s, SWA_Q_W), BF16),
        grid=(nb,),
        in_specs=[pl.BlockSpec(memory_space=pltpu.SMEM),
                  pl.BlockSpec((w, 1), lambda n: (n, 0)),
                  pl.BlockSpec((w, 1), lambda n: (prev(n), 0)),
                  pl.BlockSpec((1, LANES), lambda n: (0, 0)),
                  pl.BlockSpec((w, SWA_Q_W), lambda n: (n, 0)),
                  pl.BlockSpec((w, SWA_KV_W), lambda n: (prev(n), ka_blk)),
                  pl.BlockSpec((w, SWA_KV_W), lambda n: (n, ka_blk)),
                  pl.BlockSpec((w, SWA_KV_W), lambda n: (prev(n), va_blk)),
                  pl.BlockSpec((w, SWA_KV_W), lambda n: (n, va_blk))],
        out_specs=pl.BlockSpec((w, SWA_Q_W), lambda n: (n, 0)),
        compiler_params=_params(("arbitrary",)),
        name="swa_sink_attention",
    )(sinks, pos_col, pos_col, invf, proj, proj, proj, proj, proj)


CUM_ROWS = 256


def _cum_kernel(f_ref, o_ref, carry_ref):
    @pl.when(pl.program_id(0) == 0)
    def _():
        carry_ref[...] = jnp.zeros_like(carry_ref)

    f = f_ref[:, :FOX_HEADS]
    ls = jnp.minimum(f, 0.0) - jnp.log1p(jnp.exp(-jnp.abs(f)))
    r = lax.broadcasted_iota(jnp.int32, (CUM_ROWS, CUM_ROWS), 0)
    c = lax.broadcasted_iota(jnp.int32, (CUM_ROWS, CUM_ROWS), 1)
    tri = (c <= r).astype(F32)
    cum = jnp.dot(tri, ls, preferred_element_type=F32, precision=lax.Precision.HIGHEST) + carry_ref[...]
    o_ref[...] = cum
    carry_ref[...] = cum[CUM_ROWS - 1:CUM_ROWS, :]


def _fox_cum(proj):
    s = proj.shape[0]
    return pl.pallas_call(
        _cum_kernel,
        out_shape=jax.ShapeDtypeStruct((s, FOX_HEADS), F32),
        grid=(s // CUM_ROWS,),
        in_specs=[pl.BlockSpec((CUM_ROWS, LANES), lambda i: (i, OFF_FL // LANES))],
        out_specs=pl.BlockSpec((CUM_ROWS, FOX_HEADS), lambda i: (i, 0)),
        scratch_shapes=[pltpu.VMEM((1, FOX_HEADS), F32)],
        compiler_params=_params(("arbitrary",)),
        name="fox_decay_cumsum",
    )(proj)


FOX_BLK = 512
AUG_CQ = HEAD_DIM
AUG_ONE = HEAD_DIM + 3
AUG_END = HEAD_DIM + 6


def _split3(c):
    hi = c.astype(BF16).astype(F32)
    r = c - hi
    mid = r.astype(BF16).astype(F32)
    lo = (r - mid).astype(BF16).astype(F32)
    return hi, mid, lo


def _fox_prep_kernel(q_ref, k_ref, v_ref, cum_ref, qo_ref, ko_ref, vo_ref):
    j = pl.program_id(0)
    tb = q_ref.shape[0]
    lane = lax.broadcasted_iota(jnp.int32, (tb, LANES), 1)
    lo64 = lane < HEAD_DIM
    lane_h = lax.broadcasted_iota(jnp.int32, (tb, FOX_HEADS), 1)
    cum = cum_ref[...]
    scale = HEAD_DIM ** -0.5 * LOG2E
    for hh in range(2):
        h = 2 * j + hh
        col = jnp.sum(jnp.where(lane_h == h, cum, 0.0), axis=-1, keepdims=True)
        hi, mid, lo = _split3((col - col[0:1, :]) * LOG2E)
        q = q_ref[...] * scale
        k = k_ref[...]
        v = v_ref[...]
        if hh == 1:
            q = pltpu.roll(q, HEAD_DIM, 1)
            k = pltpu.roll(k, HEAD_DIM, 1)
            v = pltpu.roll(v, HEAD_DIM, 1)
        q_aug = jnp.where(lane == AUG_CQ, hi, jnp.where(lane == AUG_CQ + 1, mid,
                jnp.where(lane == AUG_CQ + 2, lo, jnp.where(lane < AUG_END, 1.0, 0.0))))
        k_aug = jnp.where(lane < AUG_ONE, 1.0, jnp.where(lane == AUG_ONE, -hi,
                jnp.where(lane == AUG_ONE + 1, -mid, jnp.where(lane == AUG_ONE + 2, -lo, 0.0))))
        v_aug = jnp.where(lane == HEAD_DIM, 1.0, 0.0)
        qo_ref[hh] = jnp.where(lo64, q, q_aug).astype(BF16)
        ko_ref[hh] = jnp.where(lo64, k, k_aug).astype(BF16)
        vo_ref[hh, 0] = jnp.where(lo64, v, v_aug).T.astype(BF16)


def _fox_prep(proj, cum, tb):
    s = proj.shape[0]
    npairs = FOX_HEADS // 2
    shp = jax.ShapeDtypeStruct((FOX_HEADS, s, LANES), BF16)
    ospec = pl.BlockSpec((2, tb, LANES), lambda j, i: (j, i, 0))
    return pl.pallas_call(
        _fox_prep_kernel,
        out_shape=(shp, shp, jax.ShapeDtypeStruct((FOX_HEADS, s // tb, LANES, tb), BF16)),
        grid=(npairs, s // tb),
        in_specs=[pl.BlockSpec((tb, LANES), lambda j, i: (i, OFF_QF // LANES + j)),
                  pl.BlockSpec((tb, LANES), lambda j, i: (i, OFF_KF // LANES + j)),
                  pl.BlockSpec((tb, LANES), lambda j, i: (i, OFF_VF // LANES + j)),
                  pl.BlockSpec((tb, FOX_HEADS), lambda j, i: (i, 0))],
        out_specs=(ospec, ospec, pl.BlockSpec((2, 1, LANES, tb), lambda j, i: (j, i, 0, 0))),
        compiler_params=_params(("arbitrary", "arbitrary")),
        name="fox_prepare",
    )(proj, proj, proj, cum)


def _fox_kernel(a_ref, q_ref, k_ref, vt_ref, o_ref, acc_ref, m_ref, s0_ref):
    j = pl.program_id(0)
    i = pl.program_id(1)
    tb = q_ref.shape[1]
    nt_dims = (((1,), (1,)), ((), ()))
    key = lax.broadcasted_iota(jnp.int32, (tb, tb), 0)
    qry = lax.broadcasted_iota(jnp.int32, (tb, tb), 1)
    causal = key <= qry

    def scores(hh, kt):
        rows = pl.ds(pl.multiple_of(kt * tb, tb), tb)
        s = lax.dot_general(k_ref[hh, rows, :], q_ref[hh], nt_dims, preferred_element_type=F32)
        return jnp.where(jnp.logical_or(causal, kt < i), s, NEG)

    def update(hh, kt, s):
        h = 2 * j + hh
        delta = (a_ref[h, i] - a_ref[h, kt]) * LOG2E
        m_old = m_ref[hh]
        m_new = jnp.maximum(m_old, jnp.max(s, axis=0, keepdims=True) + delta)
        p = jnp.exp2(s - (m_new - delta))
        alpha = jnp.exp2(m_old - m_new)
        acc_ref[hh] = alpha * acc_ref[hh] + jnp.dot(vt_ref[hh, kt], p.astype(BF16), preferred_element_type=F32)
        m_ref[hh] = m_new

    def step(kt, carry):
        s1 = scores(1, kt)
        update(0, kt, s0_ref[...])
        s0_ref[...] = scores(0, jnp.minimum(kt + 1, i))
        update(1, kt, s1)
        return carry

    m_ref[...] = jnp.full(m_ref.shape, NEG, F32)
    acc_ref[...] = jnp.zeros(acc_ref.shape, F32)
    s0_ref[...] = scores(0, 0)
    lax.fori_loop(0, i + 1, step, 0)
    acc0 = acc_ref[0]
    acc1 = acc_ref[1]
    o0 = (acc0 * (1.0 / acc0[HEAD_DIM:HEAD_DIM + 1, :])).T
    o1 = (acc1 * (1.0 / acc1[HEAD_DIM:HEAD_DIM + 1, :])).T
    lo64 = lax.broadcasted_iota(jnp.int32, (tb, LANES), 1) < HEAD_DIM
    o_ref[...] = jnp.where(lo64, o0, pltpu.roll(o1, HEAD_DIM, 1)).astype(o_ref.dtype)


def _fox_attention(proj):
    s = proj.shape[0]
    tb = min(FOX_BLK, s)
    cum = _fox_cum(proj)
    q_aug, k_aug, vt_aug = _fox_prep(proj, cum, tb)
    block_ref = cum[::tb].T
    npairs = FOX_HEADS // 2
    return pl.pallas_call(
        _fox_kernel,
        out_shape=jax.ShapeDtypeStruct((s, FOX_W), BF16),
        grid=(npairs, s // tb),
        in_specs=[pl.BlockSpec(memory_space=pltpu.SMEM),
                  pl.BlockSpec((2, tb, LANES), lambda j, i: (j, i, 0)),
                  pl.BlockSpec((2, s, LANES), lambda j, i: (j, 0, 0)),
                  pl.BlockSpec((2, s // tb, LANES, tb), lambda j, i: (j, 0, 0, 0))],
        out_specs=pl.BlockSpec((tb, LANES), lambda j, i: (i, j)),
        scratch_shapes=[pltpu.VMEM((2, LANES, tb), F32), pltpu.VMEM((2, 1, tb), F32), pltpu.VMEM((tb, tb), F32)],
        compiler_params=_params(("arbitrary", "arbitrary")),
        name="fox_attention",
    )(block_ref, q_aug, k_aug, vt_aug)


def _lane_pack(cols, width, dtype):
    rows = cols[0].shape[0]
    lane = lax.broadcasted_iota(jnp.int32, (rows, width), 1)
    out = jnp.zeros((rows, width), dtype)
    for k, cval in enumerate(cols):
        out = jnp.where(lane == k, cval.astype(dtype), out)
    return out


def _postmix_kernel(x_ref, y_ref, gpost_ref, gate_ref, gpre_ref, scale_ref, shift_ref, wr_ref, br_ref,
                    x1_ref, h2_ref, idx_ref, prob_ref, rank_ref, cnt_ref, carry_ref):
    step = pl.program_id(0)

    @pl.when(step == 0)
    def _():
        carry_ref[...] = jnp.zeros_like(carry_ref)

    x1 = x_ref[...] + gate_ref[...] * _rms(y_ref[...], gpost_ref[...])
    x1_ref[...] = x1
    h2 = _rms(x1, gpre_ref[...]) * (1.0 + scale_ref[...]) + shift_ref[...]
    h2_ref[...] = h2
    logits = jnp.dot(h2, wr_ref[...], preferred_element_type=F32,
                     precision=lax.Precision.HIGHEST) + br_ref[...]
    tm = logits.shape[0]
    lane_e = lax.broadcasted_iota(jnp.int32, (tm, N_EXPERTS), 1).astype(F32)
    vals, idxs, sels = [], [], []
    cur = logits
    for _ in range(TOP_K):
        mk = jnp.max(cur, axis=-1, keepdims=True)
        ik = jnp.min(jnp.where(cur == mk, lane_e, float(N_EXPERTS)), axis=-1, keepdims=True)
        sel = lane_e == ik
        vals.append(mk)
        idxs.append(ik)
        sels.append(sel)
        cur = jnp.where(sel, -jnp.inf, cur)
    exps = [jnp.exp(v - vals[0]) for v in vals]
    inv = 1.0 / functools.reduce(lambda a, b: a + b, exps)
    probs = [e * inv for e in exps]

    onehot = functools.reduce(lambda a, b: a | b, sels).astype(F32)
    r = lax.broadcasted_iota(jnp.int32, (tm, tm), 0)
    c = lax.broadcasted_iota(jnp.int32, (tm, tm), 1)
    strict = (c < r).astype(BF16)
    before = jnp.dot(strict, onehot.astype(BF16), preferred_element_type=F32) + carry_ref[...]
    ranks = [jnp.sum(jnp.where(sel, before, 0.0), axis=-1, keepdims=True) for sel in sels]
    carry_ref[...] = carry_ref[...] + jnp.sum(onehot, axis=0, keepdims=True)

    idx_ref[...] = _lane_pack(idxs, LANES, jnp.int32)
    prob_ref[...] = _lane_pack(probs, LANES, F32)
    rank_ref[...] = _lane_pack([rk.astype(jnp.int32) for rk in ranks], LANES, jnp.int32)
    cnt_ref[...] = carry_ref[...].astype(jnp.int32)


def _postmix_router(x, y, mod, g_post, g_pre, w_router, b_router):
    s, d = x.shape
    tm = min(256, s)
    row = pl.BlockSpec((tm, d), lambda i: (i, 0))
    vec = lambda idx: pl.BlockSpec((1, d), lambda i: (0, idx))
    lanes = pl.BlockSpec((tm, LANES), lambda i: (i, 0))
    return pl.pallas_call(
        _postmix_kernel,
        out_shape=(jax.ShapeDtypeStruct((s, d), F32), jax.ShapeDtypeStruct((s, d), F32),
                   jax.ShapeDtypeStruct((s, LANES), jnp.int32), jax.ShapeDtypeStruct((s, LANES), F32),
                   jax.ShapeDtypeStruct((s, LANES), jnp.int32),
                   jax.ShapeDtypeStruct((1, N_EXPERTS), jnp.int32)),
        grid=(s // tm,),
        in_specs=[row, row, vec(0), vec(2), vec(0), vec(4), vec(3),
                  pl.BlockSpec((d, N_EXPERTS), lambda i: (0, 0)),
                  pl.BlockSpec((1, N_EXPERTS), lambda i: (0, 0))],
        out_specs=(row, row, lanes, lanes, lanes, pl.BlockSpec((1, N_EXPERTS), lambda i: (0, 0))),
        scratch_shapes=[pltpu.VMEM((1, N_EXPERTS), F32)],
        compiler_params=_params(("arbitrary",)),
        name="postmix_router",
    )(x, y, g_post.reshape(1, d), mod, g_pre.reshape(1, d), mod, mod, w_router, b_router.reshape(1, N_EXPERTS))


MOE_CAP = 1280
MOE_RB = 256
GU_TN = 512
DEINT = 256
DOWN_TN = 1024


def _item_maps(n_j):
    def clamp(i, j, ni):
        used = i < ni[0]
        return jnp.where(used, i, ni[0] - 1), jnp.where(used, j, n_j - 1)

    def item_rows(i, j, ie, ir, ni):
        return clamp(i, j, ni)[0], 0, 0

    def act_rows(i, j, ie, ir, ni):
        return clamp(i, j, ni)[0], 0

    def weights(i, j, ie, ir, ni):
        ii, jj = clamp(i, j, ni)
        return ie[ii], 0, jj

    return item_rows, act_rows, weights


def _dot_mixed(a_bf16, w_f32):
    return lax.dot_general(a_bf16, w_f32, (((1,), (0,)), ((), ())), preferred_element_type=F32)


def _for_rows(n, body, unroll=4):
    def trip(t, carry):
        for u in range(unroll):
            body(t * unroll + u)
        return carry

    lax.fori_loop(0, n // unroll, trip, 0)
    lax.fori_loop((n // unroll) * unroll, n, lambda r, carry: (body(r), carry)[1], 0)


def _for_blocks(nblk, compute, finish):
    def run(blks):
        zs = [compute(b) for b in blks]
        for b, z in zip(blks, zs):
            finish(b, z)

    def pair(pb, carry):
        run([2 * pb, 2 * pb + 1])
        return carry

    lax.fori_loop(0, nblk // 2, pair, 0)

    @pl.when(nblk % 2 == 1)
    def _():
        run([nblk - 1])


def _gateup_kernel(ie_ref, ir_ref, ni_ref, dst_ref, h_ref, w_ref, b_ref, o_ref, stg_ref, xb_ref, sem):
    i = pl.program_id(0)
    j = pl.program_id(1)
    cap = o_ref.shape[0]

    @pl.when(i >= ni_ref[0])
    def _():
        o_ref[...] = jnp.zeros_like(o_ref)

    @pl.when(i < ni_ref[0])
    def _():
        rows = ir_ref[i]
        nblk = (rows + MOE_RB - 1) // MOE_RB

        def row_copy(c, r, slot):
            token = lax.rem(dst_ref[0, 0, c * MOE_RB + r], h_ref.shape[0])
            return pltpu.make_async_copy(h_ref.at[pl.ds(token, 1)], stg_ref.at[slot, pl.ds(r, 1)], sem.at[slot])

        def start_chunk(c, slot):
            _for_rows(jnp.minimum(rows - c * MOE_RB, MOE_RB), lambda r: row_copy(c, r, slot).start())

        def wait_chunk(c, slot):
            _for_rows(jnp.minimum(rows - c * MOE_RB, MOE_RB), lambda r: row_copy(c, 0, slot).wait())

        @pl.when(j == 0)
        def _():
            @pl.when(i == 0)
            def _():
                stg_ref[...] = jnp.zeros_like(stg_ref)

            start_chunk(0, 0)

            def chunk(c, carry):
                slot = c % 2

                @pl.when(c + 1 < nblk)
                def _():
                    start_chunk(c + 1, 1 - slot)

                wait_chunk(c, slot)
                xb_ref[pl.ds(pl.multiple_of(c * MOE_RB, MOE_RB), MOE_RB), :] = stg_ref[slot].astype(BF16)
                return carry

            lax.fori_loop(0, nblk, chunk, 0)

        r = lax.broadcasted_iota(jnp.int32, (DEINT, DEINT), 0)
        c = lax.broadcasted_iota(jnp.int32, (DEINT, DEINT), 1)
        half = DEINT // 2
        perm = (r == jnp.where(c < half, 2 * c, 2 * (c - half) + 1)).astype(BF16)

        def block_rows(blk):
            return pl.ds(pl.multiple_of(blk * MOE_RB, MOE_RB), MOE_RB)

        def gate_up(blk):
            return _dot_mixed(xb_ref[block_rows(blk), :], w_ref[...]) + b_ref[...]

        def activate(blk, z):
            for t in range(GU_TN // DEINT):
                zz = jnp.dot(z[:, t * DEINT:(t + 1) * DEINT].astype(BF16), perm, preferred_element_type=F32)
                glu = jnp.minimum(zz[:, :half], SWIGLU_LIMIT)
                lin = jnp.clip(zz[:, half:], -SWIGLU_LIMIT, SWIGLU_LIMIT)
                act = glu * jax.nn.sigmoid(SWIGLU_ALPHA * glu) * (lin + 1.0)
                o_ref[block_rows(blk), t * half:(t + 1) * half] = act.astype(o_ref.dtype)

        _for_blocks(nblk, gate_up, activate)

        def zero_block(blk, carry):
            rs = pl.ds(pl.multiple_of(blk * MOE_RB, MOE_RB), MOE_RB)
            o_ref[rs, :] = jnp.zeros((MOE_RB, o_ref.shape[1]), o_ref.dtype)
            return carry

        lax.fori_loop(nblk, cap // MOE_RB, zero_block, 0)


def _moe_gateup(h2, dst, item_expert, item_rows, num_items, w_gate_up, b_gate_up, cap):
    s, d = h2.shape
    n_items = dst.shape[0]
    n_j = (2 * D_FF) // GU_TN
    item_map, _, weights = _item_maps(n_j)
    grid_spec = pltpu.PrefetchScalarGridSpec(
        num_scalar_prefetch=3,
        grid=(n_items, n_j),
        in_specs=[pl.BlockSpec((1, 1, cap), item_map, memory_space=pltpu.SMEM),
                  pl.BlockSpec(memory_space=pl.ANY),
                  pl.BlockSpec((None, d, GU_TN), weights),
                  pl.BlockSpec((None, 1, GU_TN), weights)],
        out_specs=pl.BlockSpec((cap, GU_TN // 2), lambda i, j, ie, ir, ni: (i, j)),
        scratch_shapes=[pltpu.VMEM((2, MOE_RB, d), F32), pltpu.VMEM((cap, d), BF16),
                        pltpu.SemaphoreType.DMA((2,))],
    )
    return pl.pallas_call(
        _gateup_kernel,
        out_shape=jax.ShapeDtypeStruct((n_items * cap, D_FF), BF16),
        grid_spec=grid_spec,
        compiler_params=_params(("arbitrary", "arbitrary")),
        name="moe_gate_up",
    )(item_expert, item_rows, num_items, dst, h2, w_gate_up, b_gate_up.reshape(N_EXPERTS, 1, 2 * D_FF))


def _down_kernel(ie_ref, ir_ref, ni_ref, dst_ref, a_ref, w_ref, b_ref, o_ref, y_ref, sem):
    i = pl.program_id(0)
    j = pl.program_id(1)
    tn = w_ref.shape[1]
    n_j = y_ref.shape[1] // tn

    @pl.when(i < ni_ref[0])
    def _():
        rows = ir_ref[i]
        nblk = (rows + MOE_RB - 1) // MOE_RB

        def block_rows(blk):
            return pl.ds(pl.multiple_of(blk * MOE_RB, MOE_RB), MOE_RB)

        def down(blk):
            return _dot_mixed(a_ref[block_rows(blk), :], w_ref[...]) + b_ref[...]

        for jj in range(n_j):
            @pl.when(j == jj)
            def _(jj=jj):
                def keep(blk, y):
                    y_ref[block_rows(blk), jj * tn:(jj + 1) * tn] = y

                _for_blocks(nblk, down, keep)

        @pl.when(j == n_j - 1)
        def _():
            def row_copy(r, dst_row):
                return pltpu.make_async_copy(y_ref.at[pl.ds(r, 1)], o_ref.at[pl.ds(dst_row, 1)], sem)

            _for_rows(rows, lambda r: row_copy(r, dst_ref[0, 0, r]).start())
            _for_rows(rows, lambda r: row_copy(0, 0).wait())


def _moe_down(act, dst, item_expert, item_rows, num_items, w_down, b_down, n_out_rows, cap):
    d = w_down.shape[2]
    tn = min(DOWN_TN, d)
    n_items = dst.shape[0]
    n_j = d // tn
    item_map, act_rows, weights = _item_maps(n_j)
    grid_spec = pltpu.PrefetchScalarGridSpec(
        num_scalar_prefetch=3,
        grid=(n_items, n_j),
        in_specs=[pl.BlockSpec((1, 1, cap), item_map, memory_space=pltpu.SMEM),
                  pl.BlockSpec((cap, D_FF), act_rows),
                  pl.BlockSpec((None, D_FF, tn), weights),
                  pl.BlockSpec((None, 1, tn), weights)],
        out_specs=pl.BlockSpec(memory_space=pl.ANY),
        scratch_shapes=[pltpu.VMEM((cap, d), F32), pltpu.SemaphoreType.DMA(())],
    )
    return pl.pallas_call(
        _down_kernel,
        out_shape=jax.ShapeDtypeStruct((n_out_rows, d), F32),
        grid_spec=grid_spec,
        compiler_params=_params(("arbitrary", "arbitrary")),
        name="moe_down",
    )(item_expert, item_rows, num_items, dst, act, w_down, b_down.reshape(N_EXPERTS, 1, d))


COMBINE_TOKENS = 128


def _combine_kernel(prob_ref, x1_ref, g_ref, gate_ref, ys_ref, o_ref):
    prob = prob_ref[...]
    y = prob[:, 0:1] * ys_ref[0]
    for k in range(1, TOP_K):
        y = y + prob[:, k:k + 1] * ys_ref[k]
    o_ref[...] = x1_ref[...] + gate_ref[...] * _rms(y, g_ref[...])


def _moe_combine(ys, probs, x1, g_post, mod, gate_idx):
    s, d = x1.shape
    tc = min(COMBINE_TOKENS, s)
    return pl.pallas_call(
        _combine_kernel,
        out_shape=jax.ShapeDtypeStruct((s, d), F32),
        grid=(s // tc,),
        in_specs=[pl.BlockSpec((tc, LANES), lambda i: (i, 0)),
                  pl.BlockSpec((tc, d), lambda i: (i, 0)),
                  pl.BlockSpec((1, d), lambda i: (0, 0)),
                  pl.BlockSpec((1, d), lambda i: (0, gate_idx)),
                  pl.BlockSpec((TOP_K, tc, d), lambda i: (0, i, 0))],
        out_specs=pl.BlockSpec((tc, d), lambda i: (i, 0)),
        compiler_params=_params(("arbitrary",)),
        name="moe_combine",
    )(probs, x1, g_post.reshape(1, d), mod, ys.reshape(TOP_K, s, d))


def _moe(h2, idx, probs, rank, counts, x1, mod, g_post, w_gate_up, b_gate_up, w_down, b_down):
    s, d = h2.shape
    cap = MOE_CAP
    i32 = jnp.int32
    counts = counts.reshape(N_EXPERTS)
    items_per_expert = (counts + cap - 1) // cap
    item_end = jnp.cumsum(items_per_expert)
    first_item = (item_end - items_per_expert).astype(i32)
    n_items_max = (s * TOP_K + N_EXPERTS * (cap - 1)) // cap
    num_items = item_end[-1].astype(i32).reshape(1)
    item_ids = jnp.arange(n_items_max, dtype=i32)
    item_expert = jnp.minimum(jnp.sum(item_end[None, :] <= item_ids[:, None], axis=1), N_EXPERTS - 1).astype(i32)
    item_rows = jnp.clip(counts[item_expert] - cap * (item_ids - first_item[item_expert]), 0, cap)
    item_rows = jnp.where(item_ids < num_items[0], item_rows, 0).astype(i32)

    pos = (first_item[idx[:, :TOP_K]] * cap + rank[:, :TOP_K]).reshape(-1)
    tok = jnp.repeat(jnp.arange(s, dtype=i32), TOP_K)
    slot = jnp.tile(jnp.arange(TOP_K, dtype=i32), s)
    dst = jnp.zeros((n_items_max * cap,), i32).at[pos].set(slot * s + tok, unique_indices=True)
    dst = dst.reshape(n_items_max, 1, cap)

    act = _moe_gateup(h2, dst, item_expert, item_rows, num_items, w_gate_up, b_gate_up, cap)
    ys = _moe_down(act, dst, item_expert, item_rows, num_items, w_down, b_down, TOP_K * s, cap)
    return _moe_combine(ys, probs, x1, g_post, mod, 5)


def _layer(x, c, positions, w_mod, b_mod, g_pre_mix, g_post_mix, g_pre_ffn, g_post_ffn,
           w_in, b_in, sinks, w_out, b_out, w_router, b_router, w_gate_up, b_gate_up, w_down, b_down):
    mod = _modulation(c, w_mod, b_mod)
    h = _prenorm(x, g_pre_mix, mod, 1, 0)
    proj = _inproj(h, w_in, b_in)
    o_a = _swa_attention(proj, positions, sinks)
    o_f = _fox_attention(proj)
    y = _outproj(o_a, o_f, w_out, b_out)
    x1, h2, idx, probs, rank, counts = _postmix_router(x, y, mod, g_post_mix, g_pre_ffn, w_router, b_router)
    return _moe(h2, idx, probs, rank, counts, x1, mod, g_post_ffn, w_gate_up, b_gate_up, w_down, b_down)


def kernel(x, c, positions, w_mod, b_mod, g_pre_mix, g_post_mix, g_pre_ffn, g_post_ffn, w_in, b_in, sinks,
           w_out, b_out, w_router, b_router, w_gate_up, b_gate_up, w_down, b_down):
    batch, seq, d = x.shape
    assert batch == 1 and w_mod.shape[0] == 1, "one sequence, one layer"
    assert seq % FOX_BLK == 0 or seq < FOX_BLK
    out = _layer(x[0], c[0], positions[0], w_mod[0], b_mod[0], g_pre_mix[0], g_post_mix[0], g_pre_ffn[0],
                 g_post_ffn[0], w_in[0], b_in[0], sinks[0], w_out[0], b_out[0], w_router[0], b_router[0],
                 w_gate_up[0], b_gate_up[0], w_down[0], b_down[0])
    return out[None]
```

```python
import functools

import jax
import jax.numpy as jnp
import numpy as np
from jax import lax
from jax.experimental import pallas as pl
from jax.experimental.pallas import tpu as pltpu

HEAD_DIM = 64
SWA_Q_HEADS = 32
SWA_KV_HEADS = 4
SWA_GROUP = SWA_Q_HEADS // SWA_KV_HEADS
WINDOW = 128
FOX_HEADS = 32
ROPE_THETA = 10000.0
N_EXPERTS = 32
TOP_K = 4
D_FF = 1536
SWIGLU_LIMIT = 7.0
SWIGLU_ALPHA = 1.702
RMS_EPS = 1e-6
N_MOD = 6

SWA_Q_W = SWA_Q_HEADS * HEAD_DIM
SWA_KV_W = SWA_KV_HEADS * HEAD_DIM
FOX_W = FOX_HEADS * HEAD_DIM
MIX_W = SWA_Q_W + FOX_W
IN_W = SWA_Q_W + 2 * SWA_KV_W + 3 * FOX_W + FOX_HEADS
OFF_QA = 0
OFF_KA = SWA_Q_W
OFF_VA = OFF_KA + SWA_KV_W
OFF_QF = OFF_VA + SWA_KV_W
OFF_KF = OFF_QF + FOX_W
OFF_VF = OFF_KF + FOX_W
OFF_FL = OFF_VF + FOX_W

LANES = 128
SUBLANES = 8
V7X_VMEM_BYTES = 64 * 1024 * 1024
VMEM_LIMIT = V7X_VMEM_BYTES - 8 * 1024 * 1024

NEG = -1e30
LOG2E = 1.4426950408889634

F32 = jnp.float32
BF16 = jnp.bfloat16


def _params(sem, vmem=VMEM_LIMIT):
    return pltpu.CompilerParams(dimension_semantics=sem, vmem_limit_bytes=vmem)


MOD_ROWS = 256


def _mod_kernel(c_ref, w_ref, b_ref, o_ref):
    d, tn = w_ref.shape

    def body(r, acc):
        rows = pl.ds(pl.multiple_of(r * MOD_ROWS, MOD_ROWS), MOD_ROWS)
        c = c_ref[rows, :]
        sc = c * jax.nn.sigmoid(c)
        prod = w_ref[rows, :] * sc
        return acc + jnp.sum(prod.reshape(MOD_ROWS // SUBLANES, SUBLANES, tn), axis=0)

    acc = lax.fori_loop(0, d // MOD_ROWS, body, jnp.zeros((SUBLANES, tn), F32))
    o_ref[...] = jnp.sum(acc, axis=0, keepdims=True) + b_ref[...]


def _modulation(c, w_mod, b_mod):
    d, n = w_mod.shape
    tn = min(1024, n)
    assert n % tn == 0 and d % MOD_ROWS == 0
    return pl.pallas_call(
        _mod_kernel,
        out_shape=jax.ShapeDtypeStruct((1, n), F32),
        grid=(n // tn,),
        in_specs=[pl.BlockSpec((d, 1), lambda j: (0, 0)),
                  pl.BlockSpec((d, tn), lambda j: (0, j)),
                  pl.BlockSpec((1, tn), lambda j: (0, j))],
        out_specs=pl.BlockSpec((1, tn), lambda j: (0, j)),
        compiler_params=_params(("arbitrary",)),
        name="modulation",
    )(c.reshape(d, 1), w_mod, b_mod.reshape(1, n))


def _rms(x, g):
    inv = lax.rsqrt(jnp.mean(x * x, axis=-1, keepdims=True) + RMS_EPS)
    return x * inv * g


def _prenorm_kernel(x_ref, g_ref, scale_ref, shift_ref, o_ref):
    h = _rms(x_ref[...], g_ref[...]) * (1.0 + scale_ref[...]) + shift_ref[...]
    o_ref[...] = h.astype(o_ref.dtype)


def _prenorm(x, g, mod, scale_idx, shift_idx):
    s, d = x.shape
    tm = min(256, s)
    return pl.pallas_call(
        _prenorm_kernel,
        out_shape=jax.ShapeDtypeStruct((s, d), BF16),
        grid=(s // tm,),
        in_specs=[pl.BlockSpec((tm, d), lambda i: (i, 0)),
                  pl.BlockSpec((1, d), lambda i: (0, 0)),
                  pl.BlockSpec((1, d), lambda i: (0, scale_idx)),
                  pl.BlockSpec((1, d), lambda i: (0, shift_idx))],
        out_specs=pl.BlockSpec((tm, d), lambda i: (i, 0)),
        compiler_params=_params(("arbitrary",)),
        name="prenorm",
    )(x, g.reshape(1, d), mod, mod)


def _inproj_kernel(a_ref, wt_ref, b_ref, o_ref):
    acc = lax.dot_general(a_ref[...], wt_ref[...].astype(BF16), (((1,), (1,)), ((), ())),
                          preferred_element_type=F32)
    o_ref[...] = acc + b_ref[...]


def _inproj(h, w, b):
    m, k = h.shape
    n = w.shape[1]
    tm = min(1024, m)
    tn = 512
    return pl.pallas_call(
        _inproj_kernel,
        out_shape=jax.ShapeDtypeStruct((m, n), F32),
        grid=(m // tm, pl.cdiv(n, tn)),
        in_specs=[pl.BlockSpec((tm, k), lambda i, j: (i, 0)),
                  pl.BlockSpec((tn, k), lambda i, j: (j, 0)),
                  pl.BlockSpec((1, tn), lambda i, j: (0, j))],
        out_specs=pl.BlockSpec((tm, tn), lambda i, j: (i, j)),
        compiler_params=_params(("arbitrary", "arbitrary")),
        name="in_projection",
    )(h, w.T, b.reshape(1, n))


def _outproj_kernel(a1_ref, a2_ref, w_ref, b_ref, o_ref):
    k1 = a1_ref.shape[1]
    w = w_ref[...].astype(BF16)
    acc = jnp.dot(a1_ref[...], w[:k1], preferred_element_type=F32)
    acc = acc + jnp.dot(a2_ref[...], w[k1:], preferred_element_type=F32)
    o_ref[...] = acc + b_ref[...]


def _outproj(o_a, o_f, w, b):
    m, k1 = o_a.shape
    k2 = o_f.shape[1]
    n = w.shape[1]
    tm = min(1024, m)
    tn = min(512, n)
    return pl.pallas_call(
        _outproj_kernel,
        out_shape=jax.ShapeDtypeStruct((m, n), F32),
        grid=(m // tm, n // tn),
        in_specs=[pl.BlockSpec((tm, k1), lambda i, j: (i, 0)),
                  pl.BlockSpec((tm, k2), lambda i, j: (i, 0)),
                  pl.BlockSpec((k1 + k2, tn), lambda i, j: (0, j)),
                  pl.BlockSpec((1, tn), lambda i, j: (0, j))],
        out_specs=pl.BlockSpec((tm, tn), lambda i, j: (i, j)),
        compiler_params=_params(("arbitrary", "arbitrary")),
        name="out_projection",
    )(o_a, o_f, w, b.reshape(1, n))


def _swap_halves(x, first_half):
    return jnp.where(first_half, pltpu.roll(x, LANES - HEAD_DIM // 2, 1),
                     pltpu.roll(x, HEAD_DIM // 2, 1))


ROPE_ROWS = 512


def _rope_table_kernel(pos_ref, invf_ref, cos_ref, sin_ref):
    lane = lax.broadcasted_iota(jnp.int32, cos_ref.shape, 1)
    first_half = (lane % HEAD_DIM) < (HEAD_DIM // 2)
    ang = pos_ref[...].astype(F32) * invf_ref[...]
    sin = jnp.sin(ang)
    cos_ref[...] = jnp.cos(ang)
    sin_ref[...] = jnp.where(first_half, -sin, sin)


def _rope_tables(positions):
    s = positions.shape[0]
    tr = min(ROPE_ROWS, s)
    half = HEAD_DIM // 2
    inv_freq = jnp.power(ROPE_THETA, -jnp.arange(half, dtype=F32) * (2.0 / HEAD_DIM))
    invf = jnp.tile(inv_freq, LANES // half).reshape(1, LANES)
    table = jax.ShapeDtypeStruct((s, LANES), F32)
    return pl.pallas_call(
        _rope_table_kernel,
        out_shape=(table, table),
        grid=(s // tr,),
        in_specs=[pl.BlockSpec((tr, 1), lambda i: (i, 0)), pl.BlockSpec((1, LANES), lambda i: (0, 0))],
        out_specs=(pl.BlockSpec((tr, LANES), lambda i: (i, 0)), pl.BlockSpec((tr, LANES), lambda i: (i, 0))),
        compiler_params=_params(("arbitrary",)),
        name="rope_tables",
    )(positions.reshape(s, 1), invf)


def _swa_kernel(sinks_ref, cosc_ref, sinc_ref, cosp_ref, sinp_ref, q_ref, kp_ref, kc_ref, vp_ref, vc_ref, o_ref):
    n = pl.program_id(0)
    w = WINDOW
    lane = lax.broadcasted_iota(jnp.int32, (w, LANES), 1)
    first_half = (lane % HEAD_DIM) < (HEAD_DIM // 2)
    lo64 = lane < HEAD_DIM
    lane2 = lax.broadcasted_iota(jnp.int32, (2 * w, LANES), 1)
    lo64_2 = lane2 < HEAD_DIM

    cos_c, sin_c = cosc_ref[...], sinc_ref[...]
    cos_p, sin_p = cosp_ref[...], sinp_ref[...]

    def rope(t, cos, sin):
        return t * cos + _swap_halves(t, first_half) * sin

    stacked = (SWA_GROUP // 2) * w
    qi = lax.rem(lax.broadcasted_iota(jnp.int32, (stacked, 2 * w), 0), w)
    kj = lax.broadcasted_iota(jnp.int32, (stacked, 2 * w), 1)
    diff = qi + w - kj
    valid4 = (diff >= 0) & (diff < w) & ((n * w - w + kj) >= 0)

    scale = HEAD_DIM ** -0.5
    nt_dims = (((1,), (1,)), ((), ()))
    for kvp in range(SWA_KV_HEADS // 2):
        cols = slice(kvp * LANES, (kvp + 1) * LANES)
        k2 = jnp.concatenate([rope(kp_ref[:, cols], cos_p, sin_p),
                              rope(kc_ref[:, cols], cos_c, sin_c)], axis=0)
        v2 = jnp.concatenate([vp_ref[:, cols], vc_ref[:, cols]], axis=0)
        k2r = pltpu.roll(k2, HEAD_DIM, 1)
        v2r = pltpu.roll(v2, HEAD_DIM, 1)
        for sub in range(2):
            hkv = 2 * kvp + sub
            src_k, alt_k = (k2, k2r) if sub == 0 else (k2r, k2)
            src_v, alt_v = (v2, v2r) if sub == 0 else (v2r, v2)
            ka = jnp.where(lo64_2, src_k, 0.0).astype(BF16)
            kb = jnp.where(lo64_2, 0.0, alt_k).astype(BF16)
            va = jnp.where(lo64_2, src_v, 0.0).astype(BF16)
            vb = jnp.where(lo64_2, 0.0, alt_v).astype(BF16)
            pairs = [hkv * (SWA_GROUP // 2) + gp for gp in range(SWA_GROUP // 2)]
            q = jnp.concatenate([(rope(q_ref[:, pr * LANES:(pr + 1) * LANES], cos_c, sin_c) * scale).astype(BF16)
                                 for pr in pairs], axis=0)
            scores = [lax.dot_general(q, kk, nt_dims, preferred_element_type=F32) for kk in (ka, kb)]
            o = jnp.zeros((len(pairs) * w, LANES), F32)
            for which, (s, vv) in enumerate(zip(scores, (va, vb))):
                sink = jnp.concatenate([jnp.full((w, 1), sinks_ref[2 * pr + which], F32) for pr in pairs], axis=0)
                s = jnp.where(valid4, s, NEG)
                m = jnp.maximum(jnp.max(s, axis=-1, keepdims=True), sink)
                p = jnp.exp(s - m)
                den = jnp.sum(p, axis=-1, keepdims=True) + jnp.exp(sink - m)
                pv = jnp.dot(p.astype(BF16), vv, preferred_element_type=F32)
                o = o + pv * (1.0 / den)
            for r, pr in enumerate(pairs):
                o_ref[:, pr * LANES:(pr + 1) * LANES] = o[r * w:(r + 1) * w].astype(o_ref.dtype)


def _swa_attention(proj, positions, sinks):
    s = proj.shape[0]
    w = WINDOW
    nb = s // w
    cos, sin = _rope_tables(positions)
    prev = lambda n: jnp.maximum(n - 1, 0)
    ka_blk = OFF_KA // SWA_KV_W
    va_blk = OFF_VA // SWA_KV_W
    cur_tab = pl.BlockSpec((w, LANES), lambda n: (n, 0))
    prev_tab = pl.BlockSpec((w, LANES), lambda n: (prev(n), 0))
    return pl.pallas_call(
        _swa_kernel,
        out_shape=jax.ShapeDtypeStruct((s, SWA_Q_W), BF16),
        grid=(nb,),
        in_specs=[pl.BlockSpec(memory_space=pltpu.SMEM),
                  cur_tab, cur_tab, prev_tab, prev_tab,
                  pl.BlockSpec((w, SWA_Q_W), lambda n: (n, 0)),
                  pl.BlockSpec((w, SWA_KV_W), lambda n: (prev(n), ka_blk)),
                  pl.BlockSpec((w, SWA_KV_W), lambda n: (n, ka_blk)),
                  pl.BlockSpec((w, SWA_KV_W), lambda n: (prev(n), va_blk)),
                  pl.BlockSpec((w, SWA_KV_W), lambda n: (n, va_blk))],
        out_specs=pl.BlockSpec((w, SWA_Q_W), lambda n: (n, 0)),
        compiler_params=_params(("arbitrary",)),
        name="swa_sink_attention",
    )(sinks, cos, sin, cos, sin, proj, proj, proj, proj, proj)


CUM_ROWS = 256


def _cum_kernel(f_ref, o_ref, carry_ref):
    @pl.when(pl.program_id(0) == 0)
    def _():
        carry_ref[...] = jnp.zeros_like(carry_ref)

    f = f_ref[:, :FOX_HEADS]
    ls = jnp.minimum(f, 0.0) - jnp.log1p(jnp.exp(-jnp.abs(f)))
    r = lax.broadcasted_iota(jnp.int32, (CUM_ROWS, CUM_ROWS), 0)
    c = lax.broadcasted_iota(jnp.int32, (CUM_ROWS, CUM_ROWS), 1)
    tri = (c <= r).astype(F32)
    cum = jnp.dot(tri, ls, preferred_element_type=F32, precision=lax.Precision.HIGHEST) + carry_ref[...]
    o_ref[...] = cum
    carry_ref[...] = cum[CUM_ROWS - 1:CUM_ROWS, :]


def _fox_cum(proj):
    s = proj.shape[0]
    return pl.pallas_call(
        _cum_kernel,
        out_shape=jax.ShapeDtypeStruct((s, FOX_HEADS), F32),
        grid=(s // CUM_ROWS,),
        in_specs=[pl.BlockSpec((CUM_ROWS, LANES), lambda i: (i, OFF_FL // LANES))],
        out_specs=pl.BlockSpec((CUM_ROWS, FOX_HEADS), lambda i: (i, 0)),
        scratch_shapes=[pltpu.VMEM((1, FOX_HEADS), F32)],
        compiler_params=_params(("arbitrary",)),
        name="fox_decay_cumsum",
    )(proj)


FOX_BLK = 512
AUG_CQ = HEAD_DIM
AUG_ONE = HEAD_DIM + 3
AUG_END = HEAD_DIM + 6
PREP_HEADS = 8
assert OFF_QF % (PREP_HEADS * HEAD_DIM) == 0 and OFF_KF % (PREP_HEADS * HEAD_DIM) == 0
assert OFF_VF % (PREP_HEADS * HEAD_DIM) == 0


def _split3(c):
    hi = c.astype(BF16).astype(F32)
    r = c - hi
    mid = r.astype(BF16).astype(F32)
    lo = (r - mid).astype(BF16).astype(F32)
    return hi, mid, lo


def _fox_prep_kernel(q_ref, k_ref, v_ref, cum_ref, qo_ref, ko_ref, vo_ref):
    j = pl.program_id(0)
    tb = q_ref.shape[0]
    lane = lax.broadcasted_iota(jnp.int32, (tb, LANES), 1)
    lo64 = lane < HEAD_DIM
    lane_h = lax.broadcasted_iota(jnp.int32, (tb, FOX_HEADS), 1)
    cum = cum_ref[...]
    scale = HEAD_DIM ** -0.5 * LOG2E
    for local in range(PREP_HEADS):
        pair, hh = divmod(local, 2)
        h = PREP_HEADS * j + local
        col = jnp.sum(jnp.where(lane_h == h, cum, 0.0), axis=-1, keepdims=True)
        hi, mid, lo = _split3((col - col[0:1, :]) * LOG2E)
        cols = slice(pair * LANES, (pair + 1) * LANES)
        q = q_ref[:, cols] * scale
        k = k_ref[:, cols]
        v = v_ref[:, cols]
        if hh == 1:
            q = pltpu.roll(q, HEAD_DIM, 1)
            k = pltpu.roll(k, HEAD_DIM, 1)
            v = pltpu.roll(v, HEAD_DIM, 1)
        q_aug = jnp.where(lane == AUG_CQ, hi, jnp.where(lane == AUG_CQ + 1, mid,
                jnp.where(lane == AUG_CQ + 2, lo, jnp.where(lane < AUG_END, 1.0, 0.0))))
        k_aug = jnp.where(lane < AUG_ONE, 1.0, jnp.where(lane == AUG_ONE, -hi,
                jnp.where(lane == AUG_ONE + 1, -mid, jnp.where(lane == AUG_ONE + 2, -lo, 0.0))))
        v_aug = jnp.where(lane == HEAD_DIM, 1.0, 0.0)
        qo_ref[local] = jnp.where(lo64, q, q_aug).astype(BF16)
        ko_ref[local] = jnp.where(lo64, k, k_aug).astype(BF16)
        vo_ref[local, 0] = jnp.where(lo64, v, v_aug).T.astype(BF16)


def _fox_prep(proj, cum, tb):
    s = proj.shape[0]
    width = PREP_HEADS * HEAD_DIM
    shp = jax.ShapeDtypeStruct((FOX_HEADS, s, LANES), BF16)
    ospec = pl.BlockSpec((PREP_HEADS, tb, LANES), lambda j, i: (j, i, 0))
    return pl.pallas_call(
        _fox_prep_kernel,
        out_shape=(shp, shp, jax.ShapeDtypeStruct((FOX_HEADS, s // tb, LANES, tb), BF16)),
        grid=(FOX_HEADS // PREP_HEADS, s // tb),
        in_specs=[pl.BlockSpec((tb, width), lambda j, i: (i, OFF_QF // width + j)),
                  pl.BlockSpec((tb, width), lambda j, i: (i, OFF_KF // width + j)),
                  pl.BlockSpec((tb, width), lambda j, i: (i, OFF_VF // width + j)),
                  pl.BlockSpec((tb, FOX_HEADS), lambda j, i: (i, 0))],
        out_specs=(ospec, ospec, pl.BlockSpec((PREP_HEADS, 1, LANES, tb), lambda j, i: (j, i, 0, 0))),
        compiler_params=_params(("arbitrary", "arbitrary")),
        name="fox_prepare",
    )(proj, proj, proj, cum)


def _fox_kernel(a_ref, q_ref, k_ref, vt_ref, o_ref, acc_ref, m_ref, s0_ref):
    j = pl.program_id(0)
    i = pl.program_id(1)
    tb = q_ref.shape[1]
    nt_dims = (((1,), (1,)), ((), ()))
    key = lax.broadcasted_iota(jnp.int32, (tb, tb), 0)
    qry = lax.broadcasted_iota(jnp.int32, (tb, tb), 1)
    causal = key <= qry

    def scores(hh, kt):
        rows = pl.ds(pl.multiple_of(kt * tb, tb), tb)
        s = lax.dot_general(k_ref[hh, rows, :], q_ref[hh], nt_dims, preferred_element_type=F32)
        return jnp.where(jnp.logical_or(causal, kt < i), s, NEG)

    def update(hh, kt, s):
        h = 2 * j + hh
        delta = (a_ref[h, i] - a_ref[h, kt]) * LOG2E
        m_old = m_ref[hh]
        m_new = jnp.maximum(m_old, jnp.max(s, axis=0, keepdims=True) + delta)
        p = jnp.exp2(s - (m_new - delta))
        alpha = jnp.exp2(m_old - m_new)
        acc_ref[hh] = alpha * acc_ref[hh] + jnp.dot(vt_ref[hh, kt], p.astype(BF16), preferred_element_type=F32)
        m_ref[hh] = m_new

    def step(kt, carry):
        s1 = scores(1, kt)
        update(0, kt, s0_ref[...])
        s0_ref[...] = scores(0, jnp.minimum(kt + 1, i))
        update(1, kt, s1)
        return carry

    m_ref[...] = jnp.full(m_ref.shape, NEG, F32)
    acc_ref[...] = jnp.zeros(acc_ref.shape, F32)
    s0_ref[...] = scores(0, 0)
    lax.fori_loop(0, i + 1, step, 0)
    acc0 = acc_ref[0]
    acc1 = acc_ref[1]
    o0 = (acc0 * (1.0 / acc0[HEAD_DIM:HEAD_DIM + 1, :])).T
    o1 = (acc1 * (1.0 / acc1[HEAD_DIM:HEAD_DIM + 1, :])).T
    lo64 = lax.broadcasted_iota(jnp.int32, (tb, LANES), 1) < HEAD_DIM
    o_ref[...] = jnp.where(lo64, o0, pltpu.roll(o1, HEAD_DIM, 1)).astype(o_ref.dtype)


def _fox_attention(proj):
    s = proj.shape[0]
    tb = min(FOX_BLK, s)
    cum = _fox_cum(proj)
    q_aug, k_aug, vt_aug = _fox_prep(proj, cum, tb)
    block_ref = cum[::tb].T
    npairs = FOX_HEADS // 2
    return pl.pallas_call(
        _fox_kernel,
        out_shape=jax.ShapeDtypeStruct((s, FOX_W), BF16),
        grid=(npairs, s // tb),
        in_specs=[pl.BlockSpec(memory_space=pltpu.SMEM),
                  pl.BlockSpec((2, tb, LANES), lambda j, i: (j, i, 0)),
                  pl.BlockSpec((2, s, LANES), lambda j, i: (j, 0, 0)),
                  pl.BlockSpec((2, s // tb, LANES, tb), lambda j, i: (j, 0, 0, 0))],
        out_specs=pl.BlockSpec((tb, LANES), lambda j, i: (i, j)),
        scratch_shapes=[pltpu.VMEM((2, LANES, tb), F32), pltpu.VMEM((2, 1, tb), F32), pltpu.VMEM((tb, tb), F32)],
        compiler_params=_params(("arbitrary", "arbitrary")),
        name="fox_attention",
    )(block_ref, q_aug, k_aug, vt_aug)


def _lane_pack(cols, width, dtype):
    rows = cols[0].shape[0]
    lane = lax.broadcasted_iota(jnp.int32, (rows, width), 1)
    out = jnp.zeros((rows, width), dtype)
    for k, cval in enumerate(cols):
        out = jnp.where(lane == k, cval.astype(dtype), out)
    return out


def _postmix_kernel(x_ref, y_ref, gpost_ref, gate_ref, gpre_ref, scale_ref, shift_ref, wr_ref, br_ref,
                    x1_ref, h2_ref, idx_ref, prob_ref, rank_ref, cnt_ref, carry_ref):
    step = pl.program_id(0)

    @pl.when(step == 0)
    def _():
        carry_ref[...] = jnp.zeros_like(carry_ref)

    x1 = x_ref[...] + gate_ref[...] * _rms(y_ref[...], gpost_ref[...])
    x1_ref[...] = x1
    h2 = _rms(x1, gpre_ref[...]) * (1.0 + scale_ref[...]) + shift_ref[...]
    h2_ref[...] = h2
    logits = jnp.dot(h2, wr_ref[...], preferred_element_type=F32,
                     precision=lax.Precision.HIGHEST) + br_ref[...]
    tm = logits.shape[0]
    lane_e = lax.broadcasted_iota(jnp.int32, (tm, N_EXPERTS), 1).astype(F32)
    vals, idxs, sels = [], [], []
    cur = logits
    for _ in range(TOP_K):
        mk = jnp.max(cur, axis=-1, keepdims=True)
        ik = jnp.min(jnp.where(cur == mk, lane_e, float(N_EXPERTS)), axis=-1, keepdims=True)
        sel = lane_e == ik
        vals.append(mk)
        idxs.append(ik)
        sels.append(sel)
        cur = jnp.where(sel, -jnp.inf, cur)
    exps = [jnp.exp(v - vals[0]) for v in vals]
    inv = 1.0 / functools.reduce(lambda a, b: a + b, exps)
    probs = [e * inv for e in exps]

    onehot = functools.reduce(lambda a, b: a | b, sels).astype(F32)
    r = lax.broadcasted_iota(jnp.int32, (tm, tm), 0)
    c = lax.broadcasted_iota(jnp.int32, (tm, tm), 1)
    strict = (c < r).astype(BF16)
    before = jnp.dot(strict, onehot.astype(BF16), preferred_element_type=F32) + carry_ref[...]
    ranks = [jnp.sum(jnp.where(sel, before, 0.0), axis=-1, keepdims=True) for sel in sels]
    carry_ref[...] = carry_ref[...] + jnp.sum(onehot, axis=0, keepdims=True)

    idx_ref[...] = _lane_pack(idxs, LANES, jnp.int32)
    prob_ref[...] = _lane_pack(probs, LANES, F32)
    rank_ref[...] = _lane_pack([rk.astype(jnp.int32) for rk in ranks], LANES, jnp.int32)
    cnt_ref[...] = carry_ref[...].astype(jnp.int32)


def _postmix_router(x, y, mod, g_post, g_pre, w_router, b_router):
    s, d = x.shape
    tm = min(256, s)
    row = pl.BlockSpec((tm, d), lambda i: (i, 0))
    vec = lambda idx: pl.BlockSpec((1, d), lambda i: (0, idx))
    lanes = pl.BlockSpec((tm, LANES), lambda i: (i, 0))
    return pl.pallas_call(
        _postmix_kernel,
        out_shape=(jax.ShapeDtypeStruct((s, d), F32), jax.ShapeDtypeStruct((s, d), F32),
                   jax.ShapeDtypeStruct((s, LANES), jnp.int32), jax.ShapeDtypeStruct((s, LANES), F32),
                   jax.ShapeDtypeStruct((s, LANES), jnp.int32),
                   jax.ShapeDtypeStruct((1, N_EXPERTS), jnp.int32)),
        grid=(s // tm,),
        in_specs=[row, row, vec(0), vec(2), vec(0), vec(4), vec(3),
                  pl.BlockSpec((d, N_EXPERTS), lambda i: (0, 0)),
                  pl.BlockSpec((1, N_EXPERTS), lambda i: (0, 0))],
        out_specs=(row, row, lanes, lanes, lanes, pl.BlockSpec((1, N_EXPERTS), lambda i: (0, 0))),
        scratch_shapes=[pltpu.VMEM((1, N_EXPERTS), F32)],
        compiler_params=_params(("arbitrary",)),
        name="postmix_router",
    )(x, y, g_post.reshape(1, d), mod, g_pre.reshape(1, d), mod, mod, w_router, b_router.reshape(1, N_EXPERTS))


MOE_CAP = 1280
MOE_RB = 256
GU_TN = 512
DEINT = 256
DOWN_TN = 1024


def _item_maps(n_j):
    def clamp(i, j, ni):
        used = i < ni[0]
        return jnp.where(used, i, ni[0] - 1), jnp.where(used, j, n_j - 1)

    def item_rows(i, j, ie, ir, ni):
        return clamp(i, j, ni)[0], 0, 0

    def act_rows(i, j, ie, ir, ni):
        return clamp(i, j, ni)[0], 0

    def weights(i, j, ie, ir, ni):
        ii, jj = clamp(i, j, ni)
        return ie[ii], 0, jj

    return item_rows, act_rows, weights


def _dot_mixed(a_bf16, w_f32):
    return lax.dot_general(a_bf16, w_f32, (((1,), (0,)), ((), ())), preferred_element_type=F32)


def _for_rows(n, body, unroll=4):
    def trip(t, carry):
        for u in range(unroll):
            body(t * unroll + u)
        return carry

    lax.fori_loop(0, n // unroll, trip, 0)
    lax.fori_loop((n // unroll) * unroll, n, lambda r, carry: (body(r), carry)[1], 0)


def _wait_rows(src, dst, sem, n):
    n8 = pl.multiple_of((n // SUBLANES) * SUBLANES, SUBLANES)

    @pl.when(n8 > 0)
    def _():
        pltpu.make_async_copy(src.at[pl.ds(0, n8)], dst.at[pl.ds(0, n8)], sem).wait()

    one = pltpu.make_async_copy(src.at[pl.ds(0, 1)], dst.at[pl.ds(0, 1)], sem)
    lax.fori_loop(n8, n, lambda r, carry: (one.wait(), carry)[1], 0)


def _for_blocks(nblk, compute, finish):
    def run(blks):
        zs = [compute(b) for b in blks]
        for b, z in zip(blks, zs):
            finish(b, z)

    def pair(pb, carry):
        run([2 * pb, 2 * pb + 1])
        return carry

    lax.fori_loop(0, nblk // 2, pair, 0)

    @pl.when(nblk % 2 == 1)
    def _():
        run([nblk - 1])


def _gateup_kernel(ie_ref, ir_ref, ni_ref, dst_ref, h_ref, w_ref, b_ref, o_ref, stg_ref, xb_ref, sem):
    i = pl.program_id(0)
    j = pl.program_id(1)
    cap = o_ref.shape[0]

    @pl.when(i >= ni_ref[0])
    def _():
        o_ref[...] = jnp.zeros_like(o_ref)

    @pl.when(i < ni_ref[0])
    def _():
        rows = ir_ref[i]
        nblk = (rows + MOE_RB - 1) // MOE_RB

        def block_rows(blk):
            return pl.ds(pl.multiple_of(blk * MOE_RB, MOE_RB), MOE_RB)

        def row_copy(r):
            token = lax.rem(dst_ref[0, 0, r], h_ref.shape[0])
            return pltpu.make_async_copy(h_ref.at[pl.ds(token, 1)], stg_ref.at[pl.ds(r, 1)], sem)

        @pl.when(j == 0)
        def _():
            @pl.when(i == 0)
            def _():
                stg_ref[...] = jnp.zeros_like(stg_ref)

            _for_rows(rows, lambda r: row_copy(r).start())
            _wait_rows(h_ref, stg_ref, sem, rows)

            def to_bf16(blk, carry):
                xb_ref[block_rows(blk), :] = stg_ref[block_rows(blk), :].astype(BF16)
                return carry

            lax.fori_loop(0, nblk, to_bf16, 0)

        r = lax.broadcasted_iota(jnp.int32, (DEINT, DEINT), 0)
        c = lax.broadcasted_iota(jnp.int32, (DEINT, DEINT), 1)
        half = DEINT // 2
        perm = (r == jnp.where(c < half, 2 * c, 2 * (c - half) + 1)).astype(BF16)

        def gate_up(blk):
            return _dot_mixed(xb_ref[block_rows(blk), :], w_ref[...]) + b_ref[...]

        def activate(blk, z):
            for t in range(GU_TN // DEINT):
                zz = jnp.dot(z[:, t * DEINT:(t + 1) * DEINT].astype(BF16), perm, preferred_element_type=F32)
                glu = jnp.minimum(zz[:, :half], SWIGLU_LIMIT)
                lin = jnp.clip(zz[:, half:], -SWIGLU_LIMIT, SWIGLU_LIMIT)
                act = glu * jax.nn.sigmoid(SWIGLU_ALPHA * glu) * (lin + 1.0)
                o_ref[block_rows(blk), t * half:(t + 1) * half] = act.astype(o_ref.dtype)

        _for_blocks(nblk, gate_up, activate)

        def zero_block(blk, carry):
            rs = pl.ds(pl.multiple_of(blk * MOE_RB, MOE_RB), MOE_RB)
            o_ref[rs, :] = jnp.zeros((MOE_RB, o_ref.shape[1]), o_ref.dtype)
            return carry

        lax.fori_loop(nblk, cap // MOE_RB, zero_block, 0)


def _moe_gateup(h2, dst, item_expert, item_rows, num_items, w_gate_up, b_gate_up, cap):
    s, d = h2.shape
    n_items = dst.shape[0]
    n_j = (2 * D_FF) // GU_TN
    item_map, _, weights = _item_maps(n_j)
    grid_spec = pltpu.PrefetchScalarGridSpec(
        num_scalar_prefetch=3,
        grid=(n_items, n_j),
        in_specs=[pl.BlockSpec((1, 1, cap), item_map, memory_space=pltpu.SMEM),
                  pl.BlockSpec(memory_space=pl.ANY),
                  pl.BlockSpec((None, d, GU_TN), weights),
                  pl.BlockSpec((None, 1, GU_TN), weights)],
        out_specs=pl.BlockSpec((cap, GU_TN // 2), lambda i, j, ie, ir, ni: (i, j)),
        scratch_shapes=[pltpu.VMEM((cap, d), F32), pltpu.VMEM((cap, d), BF16), pltpu.SemaphoreType.DMA(())],
    )
    return pl.pallas_call(
        _gateup_kernel,
        out_shape=jax.ShapeDtypeStruct((n_items * cap, D_FF), BF16),
        grid_spec=grid_spec,
        compiler_params=_params(("arbitrary", "arbitrary")),
        name="moe_gate_up",
    )(item_expert, item_rows, num_items, dst, h2, w_gate_up, b_gate_up.reshape(N_EXPERTS, 1, 2 * D_FF))


def _down_kernel(ie_ref, ir_ref, ni_ref, dst_ref, a_ref, w_ref, b_ref, o_ref, y_ref, sem):
    i = pl.program_id(0)
    j = pl.program_id(1)
    tn = w_ref.shape[1]
    n_j = y_ref.shape[1] // tn

    @pl.when(i < ni_ref[0])
    def _():
        rows = ir_ref[i]
        nblk = (rows + MOE_RB - 1) // MOE_RB

        def block_rows(blk):
            return pl.ds(pl.multiple_of(blk * MOE_RB, MOE_RB), MOE_RB)

        def down(blk):
            return _dot_mixed(a_ref[block_rows(blk), :], w_ref[...]) + b_ref[...]

        for jj in range(n_j):
            @pl.when(j == jj)
            def _(jj=jj):
                def keep(blk, y):
                    y_ref[block_rows(blk), jj * tn:(jj + 1) * tn] = y

                _for_blocks(nblk, down, keep)

        @pl.when(j == n_j - 1)
        def _():
            def row_copy(r, dst_row):
                return pltpu.make_async_copy(y_ref.at[pl.ds(r, 1)], o_ref.at[pl.ds(dst_row, 1)], sem)

            _for_rows(rows, lambda r: row_copy(r, dst_ref[0, 0, r]).start())
            _wait_rows(y_ref, o_ref, sem, rows)


def _moe_down(act, dst, item_expert, item_rows, num_items, w_down, b_down, n_out_rows, cap):
    d = w_down.shape[2]
    tn = min(DOWN_TN, d)
    n_items = dst.shape[0]
    n_j = d // tn
    item_map, act_rows, weights = _item_maps(n_j)
    grid_spec = pltpu.PrefetchScalarGridSpec(
        num_scalar_prefetch=3,
        grid=(n_items, n_j),
        in_specs=[pl.BlockSpec((1, 1, cap), item_map, memory_space=pltpu.SMEM),
                  pl.BlockSpec((cap, D_FF), act_rows),
                  pl.BlockSpec((None, D_FF, tn), weights),
                  pl.BlockSpec((None, 1, tn), weights)],
        out_specs=pl.BlockSpec(memory_space=pl.ANY),
        scratch_shapes=[pltpu.VMEM((cap, d), F32), pltpu.SemaphoreType.DMA(())],
    )
    return pl.pallas_call(
        _down_kernel,
        out_shape=jax.ShapeDtypeStruct((n_out_rows, d), F32),
        grid_spec=grid_spec,
        compiler_params=_params(("arbitrary", "arbitrary")),
        name="moe_down",
    )(item_expert, item_rows, num_items, dst, act, w_down, b_down.reshape(N_EXPERTS, 1, d))


COMBINE_TOKENS = 128


def _combine_kernel(prob_ref, x1_ref, g_ref, gate_ref, ys_ref, o_ref):
    prob = prob_ref[...]
    y = prob[:, 0:1] * ys_ref[0]
    for k in range(1, TOP_K):
        y = y + prob[:, k:k + 1] * ys_ref[k]
    o_ref[...] = x1_ref[...] + gate_ref[...] * _rms(y, g_ref[...])


def _moe_combine(ys, probs, x1, g_post, mod, gate_idx):
    s, d = x1.shape
    tc = min(COMBINE_TOKENS, s)
    return pl.pallas_call(
        _combine_kernel,
        out_shape=jax.ShapeDtypeStruct((s, d), F32),
        grid=(s // tc,),
        in_specs=[pl.BlockSpec((tc, LANES), lambda i: (i, 0)),
                  pl.BlockSpec((tc, d), lambda i: (i, 0)),
                  pl.BlockSpec((1, d), lambda i: (0, 0)),
                  pl.BlockSpec((1, d), lambda i: (0, gate_idx)),
                  pl.BlockSpec((TOP_K, tc, d), lambda i: (0, i, 0))],
        out_specs=pl.BlockSpec((tc, d), lambda i: (i, 0)),
        compiler_params=_params(("arbitrary",)),
        name="moe_combine",
    )(probs, x1, g_post.reshape(1, d), mod, ys.reshape(TOP_K, s, d))


def _moe(h2, idx, probs, rank, counts, x1, mod, g_post, w_gate_up, b_gate_up, w_down, b_down):
    s, d = h2.shape
    cap = MOE_CAP
    i32 = jnp.int32
    counts = counts.reshape(N_EXPERTS)
    items_per_expert = (counts + cap - 1) // cap
    item_end = jnp.cumsum(items_per_expert)
    first_item = (item_end - items_per_expert).astype(i32)
    n_items_max = (s * TOP_K + N_EXPERTS * (cap - 1)) // cap
    num_items = item_end[-1].astype(i32).reshape(1)
    item_ids = jnp.arange(n_items_max, dtype=i32)
    item_expert = jnp.minimum(jnp.sum(item_end[None, :] <= item_ids[:, None], axis=1), N_EXPERTS - 1).astype(i32)
    item_rows = jnp.clip(counts[item_expert] - cap * (item_ids - first_item[item_expert]), 0, cap)
    item_rows = jnp.where(item_ids < num_items[0], item_rows, 0).astype(i32)

    pos = (first_item[idx[:, :TOP_K]] * cap + rank[:, :TOP_K]).reshape(-1)
    tok = jnp.repeat(jnp.arange(s, dtype=i32), TOP_K)
    slot = jnp.tile(jnp.arange(TOP_K, dtype=i32), s)
    dst = jnp.zeros((n_items_max * cap,), i32).at[pos].set(slot * s + tok, unique_indices=True)
    dst = dst.reshape(n_items_max, 1, cap)

    act = _moe_gateup(h2, dst, item_expert, item_rows, num_items, w_gate_up, b_gate_up, cap)
    ys = _moe_down(act, dst, item_expert, item_rows, num_items, w_down, b_down, TOP_K * s, cap)
    return _moe_combine(ys, probs, x1, g_post, mod, 5)


def _layer(x, c, positions, w_mod, b_mod, g_pre_mix, g_post_mix, g_pre_ffn, g_post_ffn,
           w_in, b_in, sinks, w_out, b_out, w_router, b_router, w_gate_up, b_gate_up, w_down, b_down):
    mod = _modulation(c, w_mod, b_mod)
    h = _prenorm(x, g_pre_mix, mod, 1, 0)
    proj = _inproj(h, w_in, b_in)
    o_a = _swa_attention(proj, positions, sinks)
    o_f = _fox_attention(proj)
    y = _outproj(o_a, o_f, w_out, b_out)
    x1, h2, idx, probs, rank, counts = _postmix_router(x, y, mod, g_post_mix, g_pre_ffn, w_router, b_router)
    return _moe(h2, idx, probs, rank, counts, x1, mod, g_post_ffn, w_gate_up, b_gate_up, w_down, b_down)


def kernel(x, c, positions, w_mod, b_mod, g_pre_mix, g_post_mix, g_pre_ffn, g_post_ffn, w_in, b_in, sinks,
           w_out, b_out, w_router, b_router, w_gate_up, b_gate_up, w_down, b_down):
    batch, seq, d = x.shape
    assert batch == 1 and w_mod.shape[0] == 1, "one sequence, one layer"
    assert seq % FOX_BLK == 0 or seq < FOX_BLK
    out = _layer(x[0], c[0], positions[0], w_mod[0], b_mod[0], g_pre_mix[0], g_post_mix[0], g_pre_ffn[0],
                 g_post_ffn[0], w_in[0], b_in[0], sinks[0], w_out[0], b_out[0], w_router[0], b_router[0],
                 w_gate_up[0], b_gate_up[0], w_down[0], b_down[0])
    return out[None]
```

```python
import functools

import jax
import jax.numpy as jnp
import numpy as np
from jax import lax
from jax.experimental import pallas as pl
from jax.experimental.pallas import tpu as pltpu

HEAD_DIM = 64
SWA_Q_HEADS = 32
SWA_KV_HEADS = 4
SWA_GROUP = SWA_Q_HEADS // SWA_KV_HEADS
WINDOW = 128
FOX_HEADS = 32
ROPE_THETA = 10000.0
N_EXPERTS = 32
TOP_K = 4
D_FF = 1536
SWIGLU_LIMIT = 7.0
SWIGLU_ALPHA = 1.702
RMS_EPS = 1e-6
N_MOD = 6

SWA_Q_W = SWA_Q_HEADS * HEAD_DIM
SWA_KV_W = SWA_KV_HEADS * HEAD_DIM
FOX_W = FOX_HEADS * HEAD_DIM
MIX_W = SWA_Q_W + FOX_W
IN_W = SWA_Q_W + 2 * SWA_KV_W + 3 * FOX_W + FOX_HEADS
OFF_QA = 0
OFF_KA = SWA_Q_W
OFF_VA = OFF_KA + SWA_KV_W
OFF_QF = OFF_VA + SWA_KV_W
OFF_KF = OFF_QF + FOX_W
OFF_VF = OFF_KF + FOX_W
OFF_FL = OFF_VF + FOX_W

LANES = 128
SUBLANES = 8
V7X_VMEM_BYTES = 64 * 1024 * 1024
VMEM_LIMIT = V7X_VMEM_BYTES - 8 * 1024 * 1024

NEG = -1e30
LOG2E = 1.4426950408889634

F32 = jnp.float32
BF16 = jnp.bfloat16


def _params(sem, vmem=VMEM_LIMIT):
    return pltpu.CompilerParams(dimension_semantics=sem, vmem_limit_bytes=vmem)


MOD_ROWS = 256


def _mod_kernel(c_ref, w_ref, b_ref, o_ref):
    d, tn = w_ref.shape

    def body(r, acc):
        rows = pl.ds(pl.multiple_of(r * MOD_ROWS, MOD_ROWS), MOD_ROWS)
        c = c_ref[rows, :]
        sc = c * jax.nn.sigmoid(c)
        prod = w_ref[rows, :] * sc
        return acc + jnp.sum(prod.reshape(MOD_ROWS // SUBLANES, SUBLANES, tn), axis=0)

    acc = lax.fori_loop(0, d // MOD_ROWS, body, jnp.zeros((SUBLANES, tn), F32))
    o_ref[...] = jnp.sum(acc, axis=0, keepdims=True) + b_ref[...]


def _modulation(c, w_mod, b_mod):
    d, n = w_mod.shape
    tn = min(1024, n)
    assert n % tn == 0 and d % MOD_ROWS == 0
    return pl.pallas_call(
        _mod_kernel,
        out_shape=jax.ShapeDtypeStruct((1, n), F32),
        grid=(n // tn,),
        in_specs=[pl.BlockSpec((d, 1), lambda j: (0, 0)),
                  pl.BlockSpec((d, tn), lambda j: (0, j)),
                  pl.BlockSpec((1, tn), lambda j: (0, j))],
        out_specs=pl.BlockSpec((1, tn), lambda j: (0, j)),
        compiler_params=_params(("arbitrary",)),
        name="modulation",
    )(c.reshape(d, 1), w_mod, b_mod.reshape(1, n))


def _rms(x, g):
    inv = lax.rsqrt(jnp.mean(x * x, axis=-1, keepdims=True) + RMS_EPS)
    return x * inv * g


def _prenorm_kernel(x_ref, g_ref, scale_ref, shift_ref, o_ref):
    h = _rms(x_ref[...], g_ref[...]) * (1.0 + scale_ref[...]) + shift_ref[...]
    o_ref[...] = h.astype(o_ref.dtype)


def _prenorm(x, g, mod, scale_idx, shift_idx):
    s, d = x.shape
    tm = min(256, s)
    return pl.pallas_call(
        _prenorm_kernel,
        out_shape=jax.ShapeDtypeStruct((s, d), BF16),
        grid=(s // tm,),
        in_specs=[pl.BlockSpec((tm, d), lambda i: (i, 0)),
                  pl.BlockSpec((1, d), lambda i: (0, 0)),
                  pl.BlockSpec((1, d), lambda i: (0, scale_idx)),
                  pl.BlockSpec((1, d), lambda i: (0, shift_idx))],
        out_specs=pl.BlockSpec((tm, d), lambda i: (i, 0)),
        compiler_params=_params(("arbitrary",)),
        name="prenorm",
    )(x, g.reshape(1, d), mod, mod)


def _inproj_kernel(a_ref, wt_ref, b_ref, o_ref):
    acc = lax.dot_general(a_ref[...], wt_ref[...].astype(BF16), (((1,), (1,)), ((), ())),
                          preferred_element_type=F32)
    o_ref[...] = acc + b_ref[...]


def _inproj(h, w, b):
    m, k = h.shape
    n = w.shape[1]
    tm = min(1024, m)
    tn = 512
    return pl.pallas_call(
        _inproj_kernel,
        out_shape=jax.ShapeDtypeStruct((m, n), F32),
        grid=(m // tm, pl.cdiv(n, tn)),
        in_specs=[pl.BlockSpec((tm, k), lambda i, j: (i, 0)),
                  pl.BlockSpec((tn, k), lambda i, j: (j, 0)),
                  pl.BlockSpec((1, tn), lambda i, j: (0, j))],
        out_specs=pl.BlockSpec((tm, tn), lambda i, j: (i, j)),
        compiler_params=_params(("arbitrary", "arbitrary")),
        name="in_projection",
    )(h, w.T, b.reshape(1, n))


def _outproj_kernel(a1_ref, a2_ref, w_ref, b_ref, o_ref):
    k1 = a1_ref.shape[1]
    w = w_ref[...].astype(BF16)
    acc = jnp.dot(a1_ref[...], w[:k1], preferred_element_type=F32)
    acc = acc + jnp.dot(a2_ref[...], w[k1:], preferred_element_type=F32)
    o_ref[...] = acc + b_ref[...]


def _outproj(o_a, o_f, w, b):
    m, k1 = o_a.shape
    k2 = o_f.shape[1]
    n = w.shape[1]
    tm = min(1024, m)
    tn = min(512, n)
    return pl.pallas_call(
        _outproj_kernel,
        out_shape=jax.ShapeDtypeStruct((m, n), F32),
        grid=(m // tm, n // tn),
        in_specs=[pl.BlockSpec((tm, k1), lambda i, j: (i, 0)),
                  pl.BlockSpec((tm, k2), lambda i, j: (i, 0)),
                  pl.BlockSpec((k1 + k2, tn), lambda i, j: (0, j)),
                  pl.BlockSpec((1, tn), lambda i, j: (0, j))],
        out_specs=pl.BlockSpec((tm, tn), lambda i, j: (i, j)),
        compiler_params=_params(("arbitrary", "arbitrary")),
        name="out_projection",
    )(o_a, o_f, w, b.reshape(1, n))


def _swap_halves(x, first_half):
    return jnp.where(first_half, pltpu.roll(x, LANES - HEAD_DIM // 2, 1),
                     pltpu.roll(x, HEAD_DIM // 2, 1))


ROPE_ROWS = 512


def _rope_table_kernel(pos_ref, invf_ref, cos_ref, sin_ref):
    lane = lax.broadcasted_iota(jnp.int32, cos_ref.shape, 1)
    first_half = (lane % HEAD_DIM) < (HEAD_DIM // 2)
    ang = pos_ref[...].astype(F32) * invf_ref[...]
    sin = jnp.sin(ang)
    cos_ref[...] = jnp.cos(ang)
    sin_ref[...] = jnp.where(first_half, -sin, sin)


def _rope_tables(positions):
    s = positions.shape[0]
    tr = min(ROPE_ROWS, s)
    half = HEAD_DIM // 2
    inv_freq = jnp.power(ROPE_THETA, -jnp.arange(half, dtype=F32) * (2.0 / HEAD_DIM))
    invf = jnp.tile(inv_freq, LANES // half).reshape(1, LANES)
    table = jax.ShapeDtypeStruct((s, LANES), F32)
    return pl.pallas_call(
        _rope_table_kernel,
        out_shape=(table, table),
        grid=(s // tr,),
        in_specs=[pl.BlockSpec((tr, 1), lambda i: (i, 0)), pl.BlockSpec((1, LANES), lambda i: (0, 0))],
        out_specs=(pl.BlockSpec((tr, LANES), lambda i: (i, 0)), pl.BlockSpec((tr, LANES), lambda i: (i, 0))),
        compiler_params=_params(("arbitrary",)),
        name="rope_tables",
    )(positions.reshape(s, 1), invf)


def _swa_kernel(sinks_ref, cosc_ref, sinc_ref, cosp_ref, sinp_ref, q_ref, kp_ref, kc_ref, vp_ref, vc_ref, o_ref):
    n = pl.program_id(0)
    w = WINDOW
    lane = lax.broadcasted_iota(jnp.int32, (w, LANES), 1)
    first_half = (lane % HEAD_DIM) < (HEAD_DIM // 2)
    lo64 = lane < HEAD_DIM
    lane2 = lax.broadcasted_iota(jnp.int32, (2 * w, LANES), 1)
    lo64_2 = lane2 < HEAD_DIM

    cos_c, sin_c = cosc_ref[...], sinc_ref[...]
    cos_p, sin_p = cosp_ref[...], sinp_ref[...]

    def rope(t, cos, sin):
        return t * cos + _swap_halves(t, first_half) * sin

    stacked = (SWA_GROUP // 2) * w
    qi = lax.rem(lax.broadcasted_iota(jnp.int32, (stacked, 2 * w), 0), w)
    kj = lax.broadcasted_iota(jnp.int32, (stacked, 2 * w), 1)
    diff = qi + w - kj
    valid4 = (diff >= 0) & (diff < w) & ((n * w - w + kj) >= 0)

    scale = HEAD_DIM ** -0.5
    nt_dims = (((1,), (1,)), ((), ()))
    for kvp in range(SWA_KV_HEADS // 2):
        cols = slice(kvp * LANES, (kvp + 1) * LANES)
        k2 = jnp.concatenate([rope(kp_ref[:, cols], cos_p, sin_p),
                              rope(kc_ref[:, cols], cos_c, sin_c)], axis=0)
        v2 = jnp.concatenate([vp_ref[:, cols], vc_ref[:, cols]], axis=0)
        k2r = pltpu.roll(k2, HEAD_DIM, 1)
        v2r = pltpu.roll(v2, HEAD_DIM, 1)
        for sub in range(2):
            hkv = 2 * kvp + sub
            src_k, alt_k = (k2, k2r) if sub == 0 else (k2r, k2)
            src_v, alt_v = (v2, v2r) if sub == 0 else (v2r, v2)
            ka = jnp.where(lo64_2, src_k, 0.0).astype(BF16)
            kb = jnp.where(lo64_2, 0.0, alt_k).astype(BF16)
            va = jnp.where(lo64_2, src_v, 0.0).astype(BF16)
            vb = jnp.where(lo64_2, 0.0, alt_v).astype(BF16)
            pairs = [hkv * (SWA_GROUP // 2) + gp for gp in range(SWA_GROUP // 2)]
            q = jnp.concatenate([(rope(q_ref[:, pr * LANES:(pr + 1) * LANES], cos_c, sin_c) * scale).astype(BF16)
                                 for pr in pairs], axis=0)
            scores = [lax.dot_general(q, kk, nt_dims, preferred_element_type=F32) for kk in (ka, kb)]
            o = jnp.zeros((len(pairs) * w, LANES), F32)
            for which, (s, vv) in enumerate(zip(scores, (va, vb))):
                sink = jnp.concatenate([jnp.full((w, 1), sinks_ref[2 * pr + which], F32) for pr in pairs], axis=0)
                s = jnp.where(valid4, s, NEG)
                m = jnp.maximum(jnp.max(s, axis=-1, keepdims=True), sink)
                p = jnp.exp(s - m)
                den = jnp.sum(p, axis=-1, keepdims=True) + jnp.exp(sink - m)
                pv = jnp.dot(p.astype(BF16), vv, preferred_element_type=F32)
                o = o + pv * (1.0 / den)
            for r, pr in enumerate(pairs):
                o_ref[:, pr * LANES:(pr + 1) * LANES] = o[r * w:(r + 1) * w].astype(o_ref.dtype)


def _swa_attention(proj, positions, sinks):
    s = proj.shape[0]
    w = WINDOW
    nb = s // w
    cos, sin = _rope_tables(positions)
    prev = lambda n: jnp.maximum(n - 1, 0)
    ka_blk = OFF_KA // SWA_KV_W
    va_blk = OFF_VA // SWA_KV_W
    cur_tab = pl.BlockSpec((w, LANES), lambda n: (n, 0))
    prev_tab = pl.BlockSpec((w, LANES), lambda n: (prev(n), 0))
    return pl.pallas_call(
        _swa_kernel,
        out_shape=jax.ShapeDtypeStruct((s, SWA_Q_W), BF16),
        grid=(nb,),
        in_specs=[pl.BlockSpec(memory_space=pltpu.SMEM),
                  cur_tab, cur_tab, prev_tab, prev_tab,
                  pl.BlockSpec((w, SWA_Q_W), lambda n: (n, 0)),
                  pl.BlockSpec((w, SWA_KV_W), lambda n: (prev(n), ka_blk)),
                  pl.BlockSpec((w, SWA_KV_W), lambda n: (n, ka_blk)),
                  pl.BlockSpec((w, SWA_KV_W), lambda n: (prev(n), va_blk)),
                  pl.BlockSpec((w, SWA_KV_W), lambda n: (n, va_blk))],
        out_specs=pl.BlockSpec((w, SWA_Q_W), lambda n: (n, 0)),
        compiler_params=_params(("arbitrary",)),
        name="swa_sink_attention",
    )(sinks, cos, sin, cos, sin, proj, proj, proj, proj, proj)


CUM_ROWS = 256


def _cum_kernel(f_ref, o_ref, carry_ref):
    @pl.when(pl.program_id(0) == 0)
    def _():
        carry_ref[...] = jnp.zeros_like(carry_ref)

    f = f_ref[:, :FOX_HEADS]
    ls = jnp.minimum(f, 0.0) - jnp.log1p(jnp.exp(-jnp.abs(f)))
    r = lax.broadcasted_iota(jnp.int32, (CUM_ROWS, CUM_ROWS), 0)
    c = lax.broadcasted_iota(jnp.int32, (CUM_ROWS, CUM_ROWS), 1)
    tri = (c <= r).astype(F32)
    cum = jnp.dot(tri, ls, preferred_element_type=F32, precision=lax.Precision.HIGHEST) + carry_ref[...]
    o_ref[...] = cum
    carry_ref[...] = cum[CUM_ROWS - 1:CUM_ROWS, :]


def _fox_cum(proj):
    s = proj.shape[0]
    return pl.pallas_call(
        _cum_kernel,
        out_shape=jax.ShapeDtypeStruct((s, FOX_HEADS), F32),
        grid=(s // CUM_ROWS,),
        in_specs=[pl.BlockSpec((CUM_ROWS, LANES), lambda i: (i, OFF_FL // LANES))],
        out_specs=pl.BlockSpec((CUM_ROWS, FOX_HEADS), lambda i: (i, 0)),
        scratch_shapes=[pltpu.VMEM((1, FOX_HEADS), F32)],
        compiler_params=_params(("arbitrary",)),
        name="fox_decay_cumsum",
    )(proj)


FOX_BLK = 512
AUG_CQ = HEAD_DIM
AUG_ONE = HEAD_DIM + 3
AUG_END = HEAD_DIM + 6
PREP_HEADS = 8
SKIP_MARGIN = 170.0
NORM_SLACK = 1.001
BF16_ROUND_UP = 1.0 + 2.0 ** -7
assert OFF_QF % (PREP_HEADS * HEAD_DIM) == 0 and OFF_KF % (PREP_HEADS * HEAD_DIM) == 0
assert OFF_VF % (PREP_HEADS * HEAD_DIM) == 0


def _split3(c):
    hi = c.astype(BF16).astype(F32)
    r = c - hi
    mid = r.astype(BF16).astype(F32)
    lo = (r - mid).astype(BF16).astype(F32)
    return hi, mid, lo


def _fox_prep_kernel(q_ref, k_ref, v_ref, cum_ref, qo_ref, ko_ref, vo_ref, qn_ref, kn_ref):
    j = pl.program_id(0)
    tb = q_ref.shape[0]
    lane = lax.broadcasted_iota(jnp.int32, (tb, LANES), 1)
    lo64 = lane < HEAD_DIM
    lane_h = lax.broadcasted_iota(jnp.int32, (tb, FOX_HEADS), 1)
    cum = cum_ref[...]
    scale = HEAD_DIM ** -0.5 * LOG2E
    for local in range(PREP_HEADS):
        pair, hh = divmod(local, 2)
        h = PREP_HEADS * j + local
        col = jnp.sum(jnp.where(lane_h == h, cum, 0.0), axis=-1, keepdims=True)
        hi, mid, lo = _split3((col - col[0:1, :]) * LOG2E)
        cols = slice(pair * LANES, (pair + 1) * LANES)
        q = q_ref[:, cols] * scale
        k = k_ref[:, cols]
        v = v_ref[:, cols]
        if hh == 1:
            q = pltpu.roll(q, HEAD_DIM, 1)
            k = pltpu.roll(k, HEAD_DIM, 1)
            v = pltpu.roll(v, HEAD_DIM, 1)
        q_aug = jnp.where(lane == AUG_CQ, hi, jnp.where(lane == AUG_CQ + 1, mid,
                jnp.where(lane == AUG_CQ + 2, lo, jnp.where(lane < AUG_END, 1.0, 0.0))))
        k_aug = jnp.where(lane < AUG_ONE, 1.0, jnp.where(lane == AUG_ONE, -hi,
                jnp.where(lane == AUG_ONE + 1, -mid, jnp.where(lane == AUG_ONE + 2, -lo, 0.0))))
        v_aug = jnp.where(lane == HEAD_DIM, 1.0, 0.0)
        qb = jnp.where(lo64, q, q_aug).astype(BF16)
        kb = jnp.where(lo64, k, k_aug).astype(BF16)
        qo_ref[local] = qb
        ko_ref[local] = kb
        vo_ref[local, 0] = jnp.where(lo64, v, v_aug).T.astype(BF16)
        ones = jnp.ones((LANES, LANES), BF16)
        for src, norm_ref in ((qb, qn_ref), (kb, kn_ref)):
            x = jnp.where(lo64, src.astype(F32), 0.0)
            sq = jnp.dot((x * x * BF16_ROUND_UP).astype(BF16), ones, preferred_element_type=F32)
            norm_ref[local, 0] = jnp.broadcast_to(jnp.sqrt(jnp.max(sq, axis=0, keepdims=True)),
                                                  norm_ref.shape[2:])


def _fox_prep(proj, cum, tb):
    s = proj.shape[0]
    width = PREP_HEADS * HEAD_DIM
    shp = jax.ShapeDtypeStruct((FOX_HEADS, s, LANES), BF16)
    ospec = pl.BlockSpec((PREP_HEADS, tb, LANES), lambda j, i: (j, i, 0))
    norms = jax.ShapeDtypeStruct((FOX_HEADS, s // tb, SUBLANES, LANES), F32)
    nspec = pl.BlockSpec((PREP_HEADS, 1, SUBLANES, LANES), lambda j, i: (j, i, 0, 0))
    return pl.pallas_call(
        _fox_prep_kernel,
        out_shape=(shp, shp, jax.ShapeDtypeStruct((FOX_HEADS, s // tb, LANES, tb), BF16), norms, norms),
        grid=(FOX_HEADS // PREP_HEADS, s // tb),
        in_specs=[pl.BlockSpec((tb, width), lambda j, i: (i, OFF_QF // width + j)),
                  pl.BlockSpec((tb, width), lambda j, i: (i, OFF_KF // width + j)),
                  pl.BlockSpec((tb, width), lambda j, i: (i, OFF_VF // width + j)),
                  pl.BlockSpec((tb, FOX_HEADS), lambda j, i: (i, 0))],
        out_specs=(ospec, ospec, pl.BlockSpec((PREP_HEADS, 1, LANES, tb), lambda j, i: (j, i, 0, 0)),
                   nspec, nspec),
        compiler_params=_params(("arbitrary", "arbitrary")),
        name="fox_prepare",
    )(proj, proj, proj, cum)


def _fox_kernel(a_ref, e_ref, qk_ref, q_ref, k_ref, vt_ref, o_ref, acc_ref, m_ref, s0_ref):
    j = pl.program_id(0)
    i = pl.program_id(1)
    tb = q_ref.shape[1]
    nt_dims = (((1,), (1,)), ((), ()))
    key = lax.broadcasted_iota(jnp.int32, (tb, tb), 0)
    qry = lax.broadcasted_iota(jnp.int32, (tb, tb), 1)
    causal = key <= qry

    def scores(hh, kt, diagonal=False):
        rows = pl.ds(pl.multiple_of(kt * tb, tb), tb)
        s = lax.dot_general(k_ref[hh, rows, :], q_ref[hh], nt_dims, preferred_element_type=F32)
        return jnp.where(causal, s, NEG) if diagonal else s

    def update(hh, kt, s):
        h = 2 * j + hh
        delta = (a_ref[h, i] - a_ref[h, kt]) * LOG2E
        m_old = m_ref[hh]
        m_new = jnp.maximum(m_old, jnp.max(s, axis=0, keepdims=True) + delta)
        p = jnp.exp2(s - (m_new - delta))
        alpha = jnp.exp2(m_old - m_new)
        acc_ref[hh] = alpha * acc_ref[hh] + jnp.dot(vt_ref[hh, kt], p.astype(BF16), preferred_element_type=F32)
        m_ref[hh] = m_new

    def step(kt, diagonal=False):
        s1 = scores(1, kt, diagonal)
        update(0, kt, s0_ref[...])
        s0_ref[...] = scores(0, jnp.maximum(kt - 1, 0))
        update(1, kt, s1)

    def tiles_needed(hh):
        h = 2 * j + hh
        floor = jnp.min(m_ref[hh]) - SKIP_MARGIN
        top = qk_ref[h, i] + a_ref[h, i] * LOG2E

        def first_kept(kt, lo):
            kept = top - e_ref[h, kt] * LOG2E >= floor
            return jnp.where(kept, jnp.minimum(lo, kt), lo)

        return i - lax.fori_loop(0, i, first_kept, i)

    m_ref[...] = jnp.full(m_ref.shape, NEG, F32)
    acc_ref[...] = jnp.zeros(acc_ref.shape, F32)
    s0_ref[...] = scores(0, i, diagonal=True)
    step(i, diagonal=True)
    n_back = jnp.maximum(tiles_needed(0), tiles_needed(1))
    lax.fori_loop(0, n_back, lambda t, carry: (step(i - 1 - t), carry)[1], 0)
    acc0 = acc_ref[0]
    acc1 = acc_ref[1]
    o0 = (acc0 * (1.0 / acc0[HEAD_DIM:HEAD_DIM + 1, :])).T
    o1 = (acc1 * (1.0 / acc1[HEAD_DIM:HEAD_DIM + 1, :])).T
    lo64 = lax.broadcasted_iota(jnp.int32, (tb, LANES), 1) < HEAD_DIM
    o_ref[...] = jnp.where(lo64, o0, pltpu.roll(o1, HEAD_DIM, 1)).astype(o_ref.dtype)


def _fox_attention(proj):
    s = proj.shape[0]
    tb = min(FOX_BLK, s)
    cum = _fox_cum(proj)
    q_aug, k_aug, vt_aug, q_norm, k_norm = _fox_prep(proj, cum, tb)
    block_first = cum[::tb].T
    block_last = cum[tb - 1::tb].T
    qk_bound = q_norm[:, :, 0, 0] * jnp.max(k_norm[:, :, 0, 0], axis=1, keepdims=True) * NORM_SLACK
    npairs = FOX_HEADS // 2
    smem = pl.BlockSpec(memory_space=pltpu.SMEM)
    return pl.pallas_call(
        _fox_kernel,
        out_shape=jax.ShapeDtypeStruct((s, FOX_W), BF16),
        grid=(npairs, s // tb),
        in_specs=[smem, smem, smem,
                  pl.BlockSpec((2, tb, LANES), lambda j, i: (j, i, 0)),
                  pl.BlockSpec((2, s, LANES), lambda j, i: (j, 0, 0)),
                  pl.BlockSpec((2, s // tb, LANES, tb), lambda j, i: (j, 0, 0, 0))],
        out_specs=pl.BlockSpec((tb, LANES), lambda j, i: (i, j)),
        scratch_shapes=[pltpu.VMEM((2, LANES, tb), F32), pltpu.VMEM((2, 1, tb), F32), pltpu.VMEM((tb, tb), F32)],
        compiler_params=_params(("arbitrary", "arbitrary")),
        name="fox_attention",
    )(block_first, block_last, qk_bound, q_aug, k_aug, vt_aug)


def _lane_pack(cols, width, dtype):
    rows = cols[0].shape[0]
    lane = lax.broadcasted_iota(jnp.int32, (rows, width), 1)
    out = jnp.zeros((rows, width), dtype)
    for k, cval in enumerate(cols):
        out = jnp.where(lane == k, cval.astype(dtype), out)
    return out


def _postmix_kernel(x_ref, y_ref, gpost_ref, gate_ref, gpre_ref, scale_ref, shift_ref, wr_ref, br_ref,
                    x1_ref, h2_ref, idx_ref, prob_ref, rank_ref, cnt_ref, carry_ref):
    step = pl.program_id(0)

    @pl.when(step == 0)
    def _():
        carry_ref[...] = jnp.zeros_like(carry_ref)

    x1 = x_ref[...] + gate_ref[...] * _rms(y_ref[...], gpost_ref[...])
    x1_ref[...] = x1
    h2 = _rms(x1, gpre_ref[...]) * (1.0 + scale_ref[...]) + shift_ref[...]
    h2_ref[...] = h2
    logits = jnp.dot(h2, wr_ref[...], preferred_element_type=F32,
                     precision=lax.Precision.HIGHEST) + br_ref[...]
    tm = logits.shape[0]
    lane_e = lax.broadcasted_iota(jnp.int32, (tm, N_EXPERTS), 1).astype(F32)
    vals, idxs, sels = [], [], []
    cur = logits
    for _ in range(TOP_K):
        mk = jnp.max(cur, axis=-1, keepdims=True)
        ik = jnp.min(jnp.where(cur == mk, lane_e, float(N_EXPERTS)), axis=-1, keepdims=True)
        sel = lane_e == ik
        vals.append(mk)
        idxs.append(ik)
        sels.append(sel)
        cur = jnp.where(sel, -jnp.inf, cur)
    exps = [jnp.exp(v - vals[0]) for v in vals]
    inv = 1.0 / functools.reduce(lambda a, b: a + b, exps)
    probs = [e * inv for e in exps]

    onehot = functools.reduce(lambda a, b: a | b, sels).astype(F32)
    r = lax.broadcasted_iota(jnp.int32, (tm, tm), 0)
    c = lax.broadcasted_iota(jnp.int32, (tm, tm), 1)
    strict = (c < r).astype(BF16)
    before = jnp.dot(strict, onehot.astype(BF16), preferred_element_type=F32) + carry_ref[...]
    ranks = [jnp.sum(jnp.where(sel, before, 0.0), axis=-1, keepdims=True) for sel in sels]
    carry_ref[...] = carry_ref[...] + jnp.sum(onehot, axis=0, keepdims=True)

    idx_ref[...] = _lane_pack(idxs, LANES, jnp.int32)
    prob_ref[...] = _lane_pack(probs, LANES, F32)
    rank_ref[...] = _lane_pack([rk.astype(jnp.int32) for rk in ranks], LANES, jnp.int32)
    cnt_ref[...] = carry_ref[...].astype(jnp.int32)


def _postmix_router(x, y, mod, g_post, g_pre, w_router, b_router):
    s, d = x.shape
    tm = min(256, s)
    row = pl.BlockSpec((tm, d), lambda i: (i, 0))
    vec = lambda idx: pl.BlockSpec((1, d), lambda i: (0, idx))
    lanes = pl.BlockSpec((tm, LANES), lambda i: (i, 0))
    return pl.pallas_call(
        _postmix_kernel,
        out_shape=(jax.ShapeDtypeStruct((s, d), F32), jax.ShapeDtypeStruct((s, d), F32),
                   jax.ShapeDtypeStruct((s, LANES), jnp.int32), jax.ShapeDtypeStruct((s, LANES), F32),
                   jax.ShapeDtypeStruct((s, LANES), jnp.int32),
                   jax.ShapeDtypeStruct((1, N_EXPERTS), jnp.int32)),
        grid=(s // tm,),
        in_specs=[row, row, vec(0), vec(2), vec(0), vec(4), vec(3),
                  pl.BlockSpec((d, N_EXPERTS), lambda i: (0, 0)),
                  pl.BlockSpec((1, N_EXPERTS), lambda i: (0, 0))],
        out_specs=(row, row, lanes, lanes, lanes, pl.BlockSpec((1, N_EXPERTS), lambda i: (0, 0))),
        scratch_shapes=[pltpu.VMEM((1, N_EXPERTS), F32)],
        compiler_params=_params(("arbitrary",)),
        name="postmix_router",
    )(x, y, g_post.reshape(1, d), mod, g_pre.reshape(1, d), mod, mod, w_router, b_router.reshape(1, N_EXPERTS))


MOE_CAP = 1280
MOE_RB = 256
GU_TN = 512
DEINT = 256
DOWN_TN = 1024


def _item_maps(n_j):
    def clamp(i, j, ni):
        used = i < ni[0]
        return jnp.where(used, i, ni[0] - 1), jnp.where(used, j, n_j - 1)

    def item_rows(i, j, ie, ir, ni):
        return clamp(i, j, ni)[0], 0, 0

    def act_rows(i, j, ie, ir, ni):
        return clamp(i, j, ni)[0], 0

    def weights(i, j, ie, ir, ni):
        ii, jj = clamp(i, j, ni)
        return ie[ii], 0, jj

    return item_rows, act_rows, weights


def _dot_mixed(a_bf16, w_f32):
    return lax.dot_general(a_bf16, w_f32, (((1,), (0,)), ((), ())), preferred_element_type=F32)


def _for_rows(n, body, unroll=4):
    def trip(t, carry):
        for u in range(unroll):
            body(t * unroll + u)
        return carry

    lax.fori_loop(0, n // unroll, trip, 0)
    lax.fori_loop((n // unroll) * unroll, n, lambda r, carry: (body(r), carry)[1], 0)


def _wait_rows(src, dst, sem, n):
    n8 = pl.multiple_of((n // SUBLANES) * SUBLANES, SUBLANES)

    @pl.when(n8 > 0)
    def _():
        pltpu.make_async_copy(src.at[pl.ds(0, n8)], dst.at[pl.ds(0, n8)], sem).wait()

    one = pltpu.make_async_copy(src.at[pl.ds(0, 1)], dst.at[pl.ds(0, 1)], sem)
    lax.fori_loop(n8, n, lambda r, carry: (one.wait(), carry)[1], 0)


def _for_blocks(nblk, compute, finish):
    def run(blks):
        zs = [compute(b) for b in blks]
        for b, z in zip(blks, zs):
            finish(b, z)

    def pair(pb, carry):
        run([2 * pb, 2 * pb + 1])
        return carry

    lax.fori_loop(0, nblk // 2, pair, 0)

    @pl.when(nblk % 2 == 1)
    def _():
        run([nblk - 1])


def _gateup_kernel(ie_ref, ir_ref, ni_ref, dst_ref, h_ref, w_ref, b_ref, o_ref, stg_ref, xb_ref, sem):
    i = pl.program_id(0)
    j = pl.program_id(1)
    cap = o_ref.shape[0]

    @pl.when(i >= ni_ref[0])
    def _():
        o_ref[...] = jnp.zeros_like(o_ref)

    @pl.when(i < ni_ref[0])
    def _():
        rows = ir_ref[i]
        nblk = (rows + MOE_RB - 1) // MOE_RB

        def block_rows(blk):
            return pl.ds(pl.multiple_of(blk * MOE_RB, MOE_RB), MOE_RB)

        def row_copy(r):
            token = lax.rem(dst_ref[0, 0, r], h_ref.shape[0])
            return pltpu.make_async_copy(h_ref.at[pl.ds(token, 1)], stg_ref.at[pl.ds(r, 1)], sem)

        @pl.when(j == 0)
        def _():
            @pl.when(i == 0)
            def _():
                stg_ref[...] = jnp.zeros_like(stg_ref)

            _for_rows(rows, lambda r: row_copy(r).start())
            _wait_rows(h_ref, stg_ref, sem, rows)

            def to_bf16(blk, carry):
                xb_ref[block_rows(blk), :] = stg_ref[block_rows(blk), :].astype(BF16)
                return carry

            lax.fori_loop(0, nblk, to_bf16, 0)

        r = lax.broadcasted_iota(jnp.int32, (DEINT, DEINT), 0)
        c = lax.broadcasted_iota(jnp.int32, (DEINT, DEINT), 1)
        half = DEINT // 2
        perm = (r == jnp.where(c < half, 2 * c, 2 * (c - half) + 1)).astype(BF16)

        def gate_up(blk):
            return _dot_mixed(xb_ref[block_rows(blk), :], w_ref[...]) + b_ref[...]

        def activate(blk, z):
            for t in range(GU_TN // DEINT):
                zz = jnp.dot(z[:, t * DEINT:(t + 1) * DEINT].astype(BF16), perm, preferred_element_type=F32)
                glu = jnp.minimum(zz[:, :half], SWIGLU_LIMIT)
                lin = jnp.clip(zz[:, half:], -SWIGLU_LIMIT, SWIGLU_LIMIT)
                act = glu * jax.nn.sigmoid(SWIGLU_ALPHA * glu) * (lin + 1.0)
                o_ref[block_rows(blk), t * half:(t + 1) * half] = act.astype(o_ref.dtype)

        _for_blocks(nblk, gate_up, activate)

        def zero_block(blk, carry):
            rs = pl.ds(pl.multiple_of(blk * MOE_RB, MOE_RB), MOE_RB)
            o_ref[rs, :] = jnp.zeros((MOE_RB, o_ref.shape[1]), o_ref.dtype)
            return carry

        lax.fori_loop(nblk, cap // MOE_RB, zero_block, 0)


def _moe_gateup(h2, dst, item_expert, item_rows, num_items, w_gate_up, b_gate_up, cap):
    s, d = h2.shape
    n_items = dst.shape[0]
    n_j = (2 * D_FF) // GU_TN
    item_map, _, weights = _item_maps(n_j)
    grid_spec = pltpu.PrefetchScalarGridSpec(
        num_scalar_prefetch=3,
        grid=(n_items, n_j),
        in_specs=[pl.BlockSpec((1, 1, cap), item_map, memory_space=pltpu.SMEM),
                  pl.BlockSpec(memory_space=pl.ANY),
                  pl.BlockSpec((None, d, GU_TN), weights),
                  pl.BlockSpec((None, 1, GU_TN), weights)],
        out_specs=pl.BlockSpec((cap, GU_TN // 2), lambda i, j, ie, ir, ni: (i, j)),
        scratch_shapes=[pltpu.VMEM((cap, d), F32), pltpu.VMEM((cap, d), BF16), pltpu.SemaphoreType.DMA(())],
    )
    return pl.pallas_call(
        _gateup_kernel,
        out_shape=jax.ShapeDtypeStruct((n_items * cap, D_FF), BF16),
        grid_spec=grid_spec,
        compiler_params=_params(("arbitrary", "arbitrary")),
        name="moe_gate_up",
    )(item_expert, item_rows, num_items, dst, h2, w_gate_up, b_gate_up.reshape(N_EXPERTS, 1, 2 * D_FF))


def _down_kernel(ie_ref, ir_ref, ni_ref, dst_ref, a_ref, w_ref, b_ref, o_ref, y_ref, sem):
    i = pl.program_id(0)
    j = pl.program_id(1)
    tn = w_ref.shape[1]
    n_j = y_ref.shape[1] // tn

    @pl.when(i < ni_ref[0])
    def _():
        rows = ir_ref[i]
        nblk = (rows + MOE_RB - 1) // MOE_RB

        def block_rows(blk):
            return pl.ds(pl.multiple_of(blk * MOE_RB, MOE_RB), MOE_RB)

        def down(blk):
            return _dot_mixed(a_ref[block_rows(blk), :], w_ref[...]) + b_ref[...]

        for jj in range(n_j):
            @pl.when(j == jj)
            def _(jj=jj):
                def keep(blk, y):
                    y_ref[block_rows(blk), jj * tn:(jj + 1) * tn] = y

                _for_blocks(nblk, down, keep)

        @pl.when(j == n_j - 1)
        def _():
            def row_copy(r, dst_row):
                return pltpu.make_async_copy(y_ref.at[pl.ds(r, 1)], o_ref.at[pl.ds(dst_row, 1)], sem)

            _for_rows(rows, lambda r: row_copy(r, dst_ref[0, 0, r]).start())
            _wait_rows(y_ref, o_ref, sem, rows)


def _moe_down(act, dst, item_expert, item_rows, num_items, w_down, b_down, n_out_rows, cap):
    d = w_down.shape[2]
    tn = min(DOWN_TN, d)
    n_items = dst.shape[0]
    n_j = d // tn
    item_map, act_rows, weights = _item_maps(n_j)
    grid_spec = pltpu.PrefetchScalarGridSpec(
        num_scalar_prefetch=3,
        grid=(n_items, n_j),
        in_specs=[pl.BlockSpec((1, 1, cap), item_map, memory_space=pltpu.SMEM),
                  pl.BlockSpec((cap, D_FF), act_rows),
                  pl.BlockSpec((None, D_FF, tn), weights),
                  pl.BlockSpec((None, 1, tn), weights)],
        out_specs=pl.BlockSpec(memory_space=pl.ANY),
        scratch_shapes=[pltpu.VMEM((cap, d), F32), pltpu.SemaphoreType.DMA(())],
    )
    return pl.pallas_call(
        _down_kernel,
        out_shape=jax.ShapeDtypeStruct((n_out_rows, d), F32),
        grid_spec=grid_spec,
        compiler_params=_params(("arbitrary", "arbitrary")),
        name="moe_down",
    )(item_expert, item_rows, num_items, dst, act, w_down, b_down.reshape(N_EXPERTS, 1, d))


COMBINE_TOKENS = 128


def _combine_kernel(prob_ref, x1_ref, g_ref, gate_ref, ys_ref, o_ref):
    prob = prob_ref[...]
    y = prob[:, 0:1] * ys_ref[0]
    for k in range(1, TOP_K):
        y = y + prob[:, k:k + 1] * ys_ref[k]
    o_ref[...] = x1_ref[...] + gate_ref[...] * _rms(y, g_ref[...])


def _moe_combine(ys, probs, x1, g_post, mod, gate_idx):
    s, d = x1.shape
    tc = min(COMBINE_TOKENS, s)
    return pl.pallas_call(
        _combine_kernel,
        out_shape=jax.ShapeDtypeStruct((s, d), F32),
        grid=(s // tc,),
        in_specs=[pl.BlockSpec((tc, LANES), lambda i: (i, 0)),
                  pl.BlockSpec((tc, d), lambda i: (i, 0)),
                  pl.BlockSpec((1, d), lambda i: (0, 0)),
                  pl.BlockSpec((1, d), lambda i: (0, gate_idx)),
                  pl.BlockSpec((TOP_K, tc, d), lambda i: (0, i, 0))],
        out_specs=pl.BlockSpec((tc, d), lambda i: (i, 0)),
        compiler_params=_params(("arbitrary",)),
        name="moe_combine",
    )(probs, x1, g_post.reshape(1, d), mod, ys.reshape(TOP_K, s, d))


def _moe(h2, idx, probs, rank, counts, x1, mod, g_post, w_gate_up, b_gate_up, w_down, b_down):
    s, d = h2.shape
    cap = MOE_CAP
    i32 = jnp.int32
    counts = counts.reshape(N_EXPERTS)
    items_per_expert = (counts + cap - 1) // cap
    item_end = jnp.cumsum(items_per_expert)
    first_item = (item_end - items_per_expert).astype(i32)
    n_items_max = (s * TOP_K + N_EXPERTS * (cap - 1)) // cap
    num_items = item_end[-1].astype(i32).reshape(1)
    item_ids = jnp.arange(n_items_max, dtype=i32)
    item_expert = jnp.minimum(jnp.sum(item_end[None, :] <= item_ids[:, None], axis=1), N_EXPERTS - 1).astype(i32)
    item_rows = jnp.clip(counts[item_expert] - cap * (item_ids - first_item[item_expert]), 0, cap)
    item_rows = jnp.where(item_ids < num_items[0], item_rows, 0).astype(i32)

    pos = (first_item[idx[:, :TOP_K]] * cap + rank[:, :TOP_K]).reshape(-1)
    tok = jnp.repeat(jnp.arange(s, dtype=i32), TOP_K)
    slot = jnp.tile(jnp.arange(TOP_K, dtype=i32), s)
    dst = jnp.zeros((n_items_max * cap,), i32).at[pos].set(slot * s + tok, unique_indices=True)
    dst = dst.reshape(n_items_max, 1, cap)

    act = _moe_gateup(h2, dst, item_expert, item_rows, num_items, w_gate_up, b_gate_up, cap)
    ys = _moe_down(act, dst, item_expert, item_rows, num_items, w_down, b_down, TOP_K * s, cap)
    return _moe_combine(ys, probs, x1, g_post, mod, 5)


def _layer(x, c, positions, w_mod, b_mod, g_pre_mix, g_post_mix, g_pre_ffn, g_post_ffn,
           w_in, b_in, sinks, w_out, b_out, w_router, b_router, w_gate_up, b_gate_up, w_down, b_down):
    mod = _modulation(c, w_mod, b_mod)
    h = _prenorm(x, g_pre_mix, mod, 1, 0)
    proj = _inproj(h, w_in, b_in)
    o_a = _swa_attention(proj, positions, sinks)
    o_f = _fox_attention(proj)
    y = _outproj(o_a, o_f, w_out, b_out)
    x1, h2, idx, probs, rank, counts = _postmix_router(x, y, mod, g_post_mix, g_pre_ffn, w_router, b_router)
    return _moe(h2, idx, probs, rank, counts, x1, mod, g_post_ffn, w_gate_up, b_gate_up, w_down, b_down)


def kernel(x, c, positions, w_mod, b_mod, g_pre_mix, g_post_mix, g_pre_ffn, g_post_ffn, w_in, b_in, sinks,
           w_out, b_out, w_router, b_router, w_gate_up, b_gate_up, w_down, b_down):
    batch, seq, d = x.shape
    assert batch == 1 and w_mod.shape[0] == 1, "one sequence, one layer"
    assert seq % FOX_BLK == 0 or seq < FOX_BLK
    out = _layer(x[0], c[0], positions[0], w_mod[0], b_mod[0], g_pre_mix[0], g_post_mix[0], g_pre_ffn[0],
                 g_post_ffn[0], w_in[0], b_in[0], sinks[0], w_out[0], b_out[0], w_router[0], b_router[0],
                 w_gate_up[0], b_gate_up[0], w_down[0], b_down[0])
    return out[None]
```

```python
import functools

import jax
import jax.numpy as jnp
import numpy as np
from jax import lax
from jax.experimental import pallas as pl
from jax.experimental.pallas import tpu as pltpu

HEAD_DIM = 64
SWA_Q_HEADS = 32
SWA_KV_HEADS = 4
SWA_GROUP = SWA_Q_HEADS // SWA_KV_HEADS
WINDOW = 128
FOX_HEADS = 32
ROPE_THETA = 10000.0
N_EXPERTS = 32
TOP_K = 4
D_FF = 1536
SWIGLU_LIMIT = 7.0
SWIGLU_ALPHA = 1.702
RMS_EPS = 1e-6
N_MOD = 6

SWA_Q_W = SWA_Q_HEADS * HEAD_DIM
SWA_KV_W = SWA_KV_HEADS * HEAD_DIM
FOX_W = FOX_HEADS * HEAD_DIM
MIX_W = SWA_Q_W + FOX_W
IN_W = SWA_Q_W + 2 * SWA_KV_W + 3 * FOX_W + FOX_HEADS
OFF_QA = 0
OFF_KA = SWA_Q_W
OFF_VA = OFF_KA + SWA_KV_W
OFF_QF = OFF_VA + SWA_KV_W
OFF_KF = OFF_QF + FOX_W
OFF_VF = OFF_KF + FOX_W
OFF_FL = OFF_VF + FOX_W

LANES = 128
SUBLANES = 8
V7X_VMEM_BYTES = 64 * 1024 * 1024
VMEM_LIMIT = V7X_VMEM_BYTES - 8 * 1024 * 1024

NEG = -1e30
LOG2E = 1.4426950408889634

F32 = jnp.float32
BF16 = jnp.bfloat16


def _params(sem, vmem=VMEM_LIMIT):
    return pltpu.CompilerParams(dimension_semantics=sem, vmem_limit_bytes=vmem)


MOD_ROWS = 256


def _mod_kernel(c_ref, w_ref, b_ref, o_ref):
    d, tn = w_ref.shape

    def body(r, acc):
        rows = pl.ds(pl.multiple_of(r * MOD_ROWS, MOD_ROWS), MOD_ROWS)
        c = c_ref[rows, :]
        sc = c * jax.nn.sigmoid(c)
        prod = w_ref[rows, :] * sc
        return acc + jnp.sum(prod.reshape(MOD_ROWS // SUBLANES, SUBLANES, tn), axis=0)

    acc = lax.fori_loop(0, d // MOD_ROWS, body, jnp.zeros((SUBLANES, tn), F32))
    o_ref[...] = jnp.sum(acc, axis=0, keepdims=True) + b_ref[...]


def _modulation(c, w_mod, b_mod):
    d, n = w_mod.shape
    tn = min(1024, n)
    assert n % tn == 0 and d % MOD_ROWS == 0
    return pl.pallas_call(
        _mod_kernel,
        out_shape=jax.ShapeDtypeStruct((1, n), F32),
        grid=(n // tn,),
        in_specs=[pl.BlockSpec((d, 1), lambda j: (0, 0)),
                  pl.BlockSpec((d, tn), lambda j: (0, j)),
                  pl.BlockSpec((1, tn), lambda j: (0, j))],
        out_specs=pl.BlockSpec((1, tn), lambda j: (0, j)),
        compiler_params=_params(("arbitrary",)),
        name="modulation",
    )(c.reshape(d, 1), w_mod, b_mod.reshape(1, n))


def _rms(x, g):
    inv = lax.rsqrt(jnp.mean(x * x, axis=-1, keepdims=True) + RMS_EPS)
    return x * inv * g


def _prenorm_kernel(x_ref, g_ref, scale_ref, shift_ref, o_ref):
    h = _rms(x_ref[...], g_ref[...]) * (1.0 + scale_ref[...]) + shift_ref[...]
    o_ref[...] = h.astype(o_ref.dtype)


def _prenorm(x, g, mod, scale_idx, shift_idx):
    s, d = x.shape
    tm = min(256, s)
    return pl.pallas_call(
        _prenorm_kernel,
        out_shape=jax.ShapeDtypeStruct((s, d), BF16),
        grid=(s // tm,),
        in_specs=[pl.BlockSpec((tm, d), lambda i: (i, 0)),
                  pl.BlockSpec((1, d), lambda i: (0, 0)),
                  pl.BlockSpec((1, d), lambda i: (0, scale_idx)),
                  pl.BlockSpec((1, d), lambda i: (0, shift_idx))],
        out_specs=pl.BlockSpec((tm, d), lambda i: (i, 0)),
        compiler_params=_params(("arbitrary",)),
        name="prenorm",
    )(x, g.reshape(1, d), mod, mod)


def _inproj_kernel(a_ref, wt_ref, b_ref, o_ref):
    acc = lax.dot_general(a_ref[...], wt_ref[...].astype(BF16), (((1,), (1,)), ((), ())),
                          preferred_element_type=F32)
    o_ref[...] = acc + b_ref[...]


def _inproj(h, w, b):
    m, k = h.shape
    n = w.shape[1]
    tm = min(1024, m)
    tn = 512
    return pl.pallas_call(
        _inproj_kernel,
        out_shape=jax.ShapeDtypeStruct((m, n), F32),
        grid=(m // tm, pl.cdiv(n, tn)),
        in_specs=[pl.BlockSpec((tm, k), lambda i, j: (i, 0)),
                  pl.BlockSpec((tn, k), lambda i, j: (j, 0)),
                  pl.BlockSpec((1, tn), lambda i, j: (0, j))],
        out_specs=pl.BlockSpec((tm, tn), lambda i, j: (i, j)),
        compiler_params=_params(("arbitrary", "arbitrary")),
        name="in_projection",
    )(h, w.T, b.reshape(1, n))


def _outproj_kernel(a1_ref, a2_ref, w_ref, b_ref, o_ref):
    k1 = a1_ref.shape[1]
    w = w_ref[...].astype(BF16)
    acc = jnp.dot(a1_ref[...], w[:k1], preferred_element_type=F32)
    acc = acc + jnp.dot(a2_ref[...], w[k1:], preferred_element_type=F32)
    o_ref[...] = acc + b_ref[...]


def _outproj(o_a, o_f, w, b):
    m, k1 = o_a.shape
    k2 = o_f.shape[1]
    n = w.shape[1]
    tm = min(1024, m)
    tn = min(512, n)
    return pl.pallas_call(
        _outproj_kernel,
        out_shape=jax.ShapeDtypeStruct((m, n), F32),
        grid=(m // tm, n // tn),
        in_specs=[pl.BlockSpec((tm, k1), lambda i, j: (i, 0)),
                  pl.BlockSpec((tm, k2), lambda i, j: (i, 0)),
                  pl.BlockSpec((k1 + k2, tn), lambda i, j: (0, j)),
                  pl.BlockSpec((1, tn), lambda i, j: (0, j))],
        out_specs=pl.BlockSpec((tm, tn), lambda i, j: (i, j)),
        compiler_params=_params(("arbitrary", "arbitrary")),
        name="out_projection",
    )(o_a, o_f, w, b.reshape(1, n))


def _swap_halves(x, first_half):
    return jnp.where(first_half, pltpu.roll(x, LANES - HEAD_DIM // 2, 1),
                     pltpu.roll(x, HEAD_DIM // 2, 1))


ROPE_ROWS = 512


def _rope_table_kernel(pos_ref, invf_ref, cos_ref, sin_ref):
    lane = lax.broadcasted_iota(jnp.int32, cos_ref.shape, 1)
    first_half = (lane % HEAD_DIM) < (HEAD_DIM // 2)
    ang = pos_ref[...].astype(F32) * invf_ref[...]
    sin = jnp.sin(ang)
    cos_ref[...] = jnp.cos(ang)
    sin_ref[...] = jnp.where(first_half, -sin, sin)


def _rope_tables(positions):
    s = positions.shape[0]
    tr = min(ROPE_ROWS, s)
    half = HEAD_DIM // 2
    inv_freq = jnp.power(ROPE_THETA, -jnp.arange(half, dtype=F32) * (2.0 / HEAD_DIM))
    invf = jnp.tile(inv_freq, LANES // half).reshape(1, LANES)
    table = jax.ShapeDtypeStruct((s, LANES), F32)
    return pl.pallas_call(
        _rope_table_kernel,
        out_shape=(table, table),
        grid=(s // tr,),
        in_specs=[pl.BlockSpec((tr, 1), lambda i: (i, 0)), pl.BlockSpec((1, LANES), lambda i: (0, 0))],
        out_specs=(pl.BlockSpec((tr, LANES), lambda i: (i, 0)), pl.BlockSpec((tr, LANES), lambda i: (i, 0))),
        compiler_params=_params(("arbitrary",)),
        name="rope_tables",
    )(positions.reshape(s, 1), invf)


def _swa_kernel(sinks_ref, cosc_ref, sinc_ref, cosp_ref, sinp_ref, q_ref, kp_ref, kc_ref, vp_ref, vc_ref, o_ref):
    n = pl.program_id(0)
    w = WINDOW
    lane = lax.broadcasted_iota(jnp.int32, (w, LANES), 1)
    first_half = (lane % HEAD_DIM) < (HEAD_DIM // 2)
    lo64 = lane < HEAD_DIM
    lane2 = lax.broadcasted_iota(jnp.int32, (2 * w, LANES), 1)
    lo64_2 = lane2 < HEAD_DIM

    cos_c, sin_c = cosc_ref[...], sinc_ref[...]
    cos_p, sin_p = cosp_ref[...], sinp_ref[...]

    def rope(t, cos, sin):
        return t * cos + _swap_halves(t, first_half) * sin

    stacked = (SWA_GROUP // 2) * w
    qi = lax.rem(lax.broadcasted_iota(jnp.int32, (stacked, 2 * w), 0), w)
    kj = lax.broadcasted_iota(jnp.int32, (stacked, 2 * w), 1)
    diff = qi + w - kj
    valid4 = (diff >= 0) & (diff < w) & ((n * w - w + kj) >= 0)

    scale = HEAD_DIM ** -0.5
    nt_dims = (((1,), (1,)), ((), ()))
    for kvp in range(SWA_KV_HEADS // 2):
        cols = slice(kvp * LANES, (kvp + 1) * LANES)
        k2 = jnp.concatenate([rope(kp_ref[:, cols], cos_p, sin_p),
                              rope(kc_ref[:, cols], cos_c, sin_c)], axis=0)
        v2 = jnp.concatenate([vp_ref[:, cols], vc_ref[:, cols]], axis=0)
        k2r = pltpu.roll(k2, HEAD_DIM, 1)
        v2r = pltpu.roll(v2, HEAD_DIM, 1)
        for sub in range(2):
            hkv = 2 * kvp + sub
            src_k, alt_k = (k2, k2r) if sub == 0 else (k2r, k2)
            src_v, alt_v = (v2, v2r) if sub == 0 else (v2r, v2)
            ka = jnp.where(lo64_2, src_k, 0.0).astype(BF16)
            kb = jnp.where(lo64_2, 0.0, alt_k).astype(BF16)
            va = jnp.where(lo64_2, src_v, 0.0).astype(BF16)
            vb = jnp.where(lo64_2, 0.0, alt_v).astype(BF16)
            pairs = [hkv * (SWA_GROUP // 2) + gp for gp in range(SWA_GROUP // 2)]
            q = jnp.concatenate([(rope(q_ref[:, pr * LANES:(pr + 1) * LANES], cos_c, sin_c) * scale).astype(BF16)
                                 for pr in pairs], axis=0)
            scores = [lax.dot_general(q, kk, nt_dims, preferred_element_type=F32) for kk in (ka, kb)]
            o = jnp.zeros((len(pairs) * w, LANES), F32)
            for which, (s, vv) in enumerate(zip(scores, (va, vb))):
                sink = jnp.concatenate([jnp.full((w, 1), sinks_ref[2 * pr + which], F32) for pr in pairs], axis=0)
                s = jnp.where(valid4, s, NEG)
                m = jnp.maximum(jnp.max(s, axis=-1, keepdims=True), sink)
                p = jnp.exp(s - m)
                den = jnp.sum(p, axis=-1, keepdims=True) + jnp.exp(sink - m)
                pv = jnp.dot(p.astype(BF16), vv, preferred_element_type=F32)
                o = o + pv * (1.0 / den)
            for r, pr in enumerate(pairs):
                o_ref[:, pr * LANES:(pr + 1) * LANES] = o[r * w:(r + 1) * w].astype(o_ref.dtype)


def _swa_attention(proj, positions, sinks):
    s = proj.shape[0]
    w = WINDOW
    nb = s // w
    cos, sin = _rope_tables(positions)
    prev = lambda n: jnp.maximum(n - 1, 0)
    ka_blk = OFF_KA // SWA_KV_W
    va_blk = OFF_VA // SWA_KV_W
    cur_tab = pl.BlockSpec((w, LANES), lambda n: (n, 0))
    prev_tab = pl.BlockSpec((w, LANES), lambda n: (prev(n), 0))
    return pl.pallas_call(
        _swa_kernel,
        out_shape=jax.ShapeDtypeStruct((s, SWA_Q_W), BF16),
        grid=(nb,),
        in_specs=[pl.BlockSpec(memory_space=pltpu.SMEM),
                  cur_tab, cur_tab, prev_tab, prev_tab,
                  pl.BlockSpec((w, SWA_Q_W), lambda n: (n, 0)),
                  pl.BlockSpec((w, SWA_KV_W), lambda n: (prev(n), ka_blk)),
                  pl.BlockSpec((w, SWA_KV_W), lambda n: (n, ka_blk)),
                  pl.BlockSpec((w, SWA_KV_W), lambda n: (prev(n), va_blk)),
                  pl.BlockSpec((w, SWA_KV_W), lambda n: (n, va_blk))],
        out_specs=pl.BlockSpec((w, SWA_Q_W), lambda n: (n, 0)),
        compiler_params=_params(("arbitrary",)),
        name="swa_sink_attention",
    )(sinks, cos, sin, cos, sin, proj, proj, proj, proj, proj)


CUM_ROWS = 256


def _cum_kernel(f_ref, o_ref, carry_ref):
    @pl.when(pl.program_id(0) == 0)
    def _():
        carry_ref[...] = jnp.zeros_like(carry_ref)

    f = f_ref[:, :FOX_HEADS]
    ls = jnp.minimum(f, 0.0) - jnp.log1p(jnp.exp(-jnp.abs(f)))
    r = lax.broadcasted_iota(jnp.int32, (CUM_ROWS, CUM_ROWS), 0)
    c = lax.broadcasted_iota(jnp.int32, (CUM_ROWS, CUM_ROWS), 1)
    tri = (c <= r).astype(F32)
    cum = jnp.dot(tri, ls, preferred_element_type=F32, precision=lax.Precision.HIGHEST) + carry_ref[...]
    o_ref[...] = cum
    carry_ref[...] = cum[CUM_ROWS - 1:CUM_ROWS, :]


def _fox_cum(proj):
    s = proj.shape[0]
    return pl.pallas_call(
        _cum_kernel,
        out_shape=jax.ShapeDtypeStruct((s, FOX_HEADS), F32),
        grid=(s // CUM_ROWS,),
        in_specs=[pl.BlockSpec((CUM_ROWS, LANES), lambda i: (i, OFF_FL // LANES))],
        out_specs=pl.BlockSpec((CUM_ROWS, FOX_HEADS), lambda i: (i, 0)),
        scratch_shapes=[pltpu.VMEM((1, FOX_HEADS), F32)],
        compiler_params=_params(("arbitrary",)),
        name="fox_decay_cumsum",
    )(proj)


FOX_BLK = 512
AUG_CQ = HEAD_DIM
AUG_ONE = HEAD_DIM + 3
AUG_END = HEAD_DIM + 6
PREP_HEADS = 8
SKIP_MARGIN = 170.0
NORM_SLACK = 1.001
BF16_ROUND_UP = 1.0 + 2.0 ** -7
assert OFF_QF % (PREP_HEADS * HEAD_DIM) == 0 and OFF_KF % (PREP_HEADS * HEAD_DIM) == 0
assert OFF_VF % (PREP_HEADS * HEAD_DIM) == 0


def _split3(c):
    hi = c.astype(BF16).astype(F32)
    r = c - hi
    mid = r.astype(BF16).astype(F32)
    lo = (r - mid).astype(BF16).astype(F32)
    return hi, mid, lo


def _fox_prep_kernel(q_ref, k_ref, v_ref, cum_ref, qo_ref, ko_ref, vo_ref, qn_ref, kn_ref):
    j = pl.program_id(0)
    tb = q_ref.shape[0]
    lane = lax.broadcasted_iota(jnp.int32, (tb, LANES), 1)
    lo64 = lane < HEAD_DIM
    lane_h = lax.broadcasted_iota(jnp.int32, (tb, FOX_HEADS), 1)
    cum = cum_ref[...]
    scale = HEAD_DIM ** -0.5 * LOG2E
    for local in range(PREP_HEADS):
        pair, hh = divmod(local, 2)
        h = PREP_HEADS * j + local
        col = jnp.sum(jnp.where(lane_h == h, cum, 0.0), axis=-1, keepdims=True)
        hi, mid, lo = _split3((col - col[0:1, :]) * LOG2E)
        cols = slice(pair * LANES, (pair + 1) * LANES)
        q = q_ref[:, cols] * scale
        k = k_ref[:, cols]
        v = v_ref[:, cols]
        if hh == 1:
            q = pltpu.roll(q, HEAD_DIM, 1)
            k = pltpu.roll(k, HEAD_DIM, 1)
            v = pltpu.roll(v, HEAD_DIM, 1)
        q_aug = jnp.where(lane == AUG_CQ, hi, jnp.where(lane == AUG_CQ + 1, mid,
                jnp.where(lane == AUG_CQ + 2, lo, jnp.where(lane < AUG_END, 1.0, 0.0))))
        k_aug = jnp.where(lane < AUG_ONE, 1.0, jnp.where(lane == AUG_ONE, -hi,
                jnp.where(lane == AUG_ONE + 1, -mid, jnp.where(lane == AUG_ONE + 2, -lo, 0.0))))
        v_aug = jnp.where(lane == HEAD_DIM, 1.0, 0.0)
        qb = jnp.where(lo64, q, q_aug).astype(BF16)
        kb = jnp.where(lo64, k, k_aug).astype(BF16)
        qo_ref[local] = qb
        ko_ref[local] = kb
        vo_ref[local, 0] = jnp.where(lo64, v, v_aug).T.astype(BF16)
        ones = jnp.ones((LANES, LANES), BF16)
        for src, norm_ref in ((qb, qn_ref), (kb, kn_ref)):
            x = jnp.where(lo64, src.astype(F32), 0.0)
            sq = jnp.dot((x * x * BF16_ROUND_UP).astype(BF16), ones, preferred_element_type=F32)
            norm_ref[local, 0] = jnp.broadcast_to(jnp.sqrt(jnp.max(sq, axis=0, keepdims=True)),
                                                  norm_ref.shape[2:])


def _fox_prep(proj, cum, tb):
    s = proj.shape[0]
    width = PREP_HEADS * HEAD_DIM
    shp = jax.ShapeDtypeStruct((FOX_HEADS, s, LANES), BF16)
    ospec = pl.BlockSpec((PREP_HEADS, tb, LANES), lambda j, i: (j, i, 0))
    norms = jax.ShapeDtypeStruct((FOX_HEADS, s // tb, SUBLANES, LANES), F32)
    nspec = pl.BlockSpec((PREP_HEADS, 1, SUBLANES, LANES), lambda j, i: (j, i, 0, 0))
    return pl.pallas_call(
        _fox_prep_kernel,
        out_shape=(shp, shp, jax.ShapeDtypeStruct((FOX_HEADS, s // tb, LANES, tb), BF16), norms, norms),
        grid=(FOX_HEADS // PREP_HEADS, s // tb),
        in_specs=[pl.BlockSpec((tb, width), lambda j, i: (i, OFF_QF // width + j)),
                  pl.BlockSpec((tb, width), lambda j, i: (i, OFF_KF // width + j)),
                  pl.BlockSpec((tb, width), lambda j, i: (i, OFF_VF // width + j)),
                  pl.BlockSpec((tb, FOX_HEADS), lambda j, i: (i, 0))],
        out_specs=(ospec, ospec, pl.BlockSpec((PREP_HEADS, 1, LANES, tb), lambda j, i: (j, i, 0, 0)),
                   nspec, nspec),
        compiler_params=_params(("arbitrary", "arbitrary")),
        name="fox_prepare",
    )(proj, proj, proj, cum)


def _fox_kernel(a_ref, e_ref, qk_ref, q_ref, k_ref, vt_ref, o_ref, acc_ref, m_ref, s0_ref):
    j = pl.program_id(0)
    i = pl.program_id(1)
    tb = q_ref.shape[1]
    nt_dims = (((1,), (1,)), ((), ()))
    key = lax.broadcasted_iota(jnp.int32, (tb, tb), 0)
    qry = lax.broadcasted_iota(jnp.int32, (tb, tb), 1)
    causal = key <= qry

    def scores(hh, kt, diagonal=False):
        rows = pl.ds(pl.multiple_of(kt * tb, tb), tb)
        s = lax.dot_general(k_ref[hh, rows, :], q_ref[hh], nt_dims, preferred_element_type=F32)
        return jnp.where(causal, s, NEG) if diagonal else s

    def update(hh, kt, s):
        h = 2 * j + hh
        delta = (a_ref[h, i] - a_ref[h, kt]) * LOG2E
        m_old = m_ref[hh]
        m_new = jnp.maximum(m_old, jnp.max(s, axis=0, keepdims=True) + delta)
        p = jnp.exp2(s - (m_new - delta))
        alpha = jnp.exp2(m_old - m_new)
        acc_ref[hh] = alpha * acc_ref[hh] + jnp.dot(vt_ref[hh, kt], p.astype(BF16), preferred_element_type=F32)
        m_ref[hh] = m_new

    def step(kt, diagonal=False):
        s1 = scores(1, kt, diagonal)
        update(0, kt, s0_ref[...])
        s0_ref[...] = scores(0, jnp.maximum(kt - 1, 0))
        update(1, kt, s1)

    def tiles_needed(hh):
        h = 2 * j + hh
        floor = jnp.min(m_ref[hh]) - SKIP_MARGIN
        top = qk_ref[h, i] + a_ref[h, i] * LOG2E

        def first_kept(kt, lo):
            kept = top - e_ref[h, kt] * LOG2E >= floor
            return jnp.where(kept, jnp.minimum(lo, kt), lo)

        return i - lax.fori_loop(0, i, first_kept, i)

    m_ref[...] = jnp.full(m_ref.shape, NEG, F32)
    acc_ref[...] = jnp.zeros(acc_ref.shape, F32)
    s0_ref[...] = scores(0, i, diagonal=True)
    step(i, diagonal=True)
    n_back = jnp.maximum(tiles_needed(0), tiles_needed(1))
    lax.fori_loop(0, n_back, lambda t, carry: (step(i - 1 - t), carry)[1], 0)
    acc0 = acc_ref[0]
    acc1 = acc_ref[1]
    o0 = (acc0 * (1.0 / acc0[HEAD_DIM:HEAD_DIM + 1, :])).T
    o1 = (acc1 * (1.0 / acc1[HEAD_DIM:HEAD_DIM + 1, :])).T
    lo64 = lax.broadcasted_iota(jnp.int32, (tb, LANES), 1) < HEAD_DIM
    o_ref[...] = jnp.where(lo64, o0, pltpu.roll(o1, HEAD_DIM, 1)).astype(o_ref.dtype)


def _fox_attention(proj):
    s = proj.shape[0]
    tb = min(FOX_BLK, s)
    cum = _fox_cum(proj)
    q_aug, k_aug, vt_aug, q_norm, k_norm = _fox_prep(proj, cum, tb)
    block_first = cum[::tb].T
    block_last = cum[tb - 1::tb].T
    qk_bound = q_norm[:, :, 0, 0] * jnp.max(k_norm[:, :, 0, 0], axis=1, keepdims=True) * NORM_SLACK
    npairs = FOX_HEADS // 2
    smem = pl.BlockSpec(memory_space=pltpu.SMEM)
    return pl.pallas_call(
        _fox_kernel,
        out_shape=jax.ShapeDtypeStruct((s, FOX_W), BF16),
        grid=(npairs, s // tb),
        in_specs=[smem, smem, smem,
                  pl.BlockSpec((2, tb, LANES), lambda j, i: (j, i, 0)),
                  pl.BlockSpec((2, s, LANES), lambda j, i: (j, 0, 0)),
                  pl.BlockSpec((2, s // tb, LANES, tb), lambda j, i: (j, 0, 0, 0))],
        out_specs=pl.BlockSpec((tb, LANES), lambda j, i: (i, j)),
        scratch_shapes=[pltpu.VMEM((2, LANES, tb), F32), pltpu.VMEM((2, 1, tb), F32), pltpu.VMEM((tb, tb), F32)],
        compiler_params=_params(("arbitrary", "arbitrary")),
        name="fox_attention",
    )(block_first, block_last, qk_bound, q_aug, k_aug, vt_aug)


def _lane_pack(cols, width, dtype):
    rows = cols[0].shape[0]
    lane = lax.broadcasted_iota(jnp.int32, (rows, width), 1)
    out = jnp.zeros((rows, width), dtype)
    for k, cval in enumerate(cols):
        out = jnp.where(lane == k, cval.astype(dtype), out)
    return out


def _postmix_kernel(x_ref, y_ref, gpost_ref, gate_ref, gpre_ref, scale_ref, shift_ref, wr_ref, br_ref,
                    x1_ref, h2_ref, idx_ref, prob_ref, rank_ref, cnt_ref, carry_ref):
    step = pl.program_id(0)

    @pl.when(step == 0)
    def _():
        carry_ref[...] = jnp.zeros_like(carry_ref)

    x1 = x_ref[...] + gate_ref[...] * _rms(y_ref[...], gpost_ref[...])
    x1_ref[...] = x1
    h2 = _rms(x1, gpre_ref[...]) * (1.0 + scale_ref[...]) + shift_ref[...]
    h2_ref[...] = h2
    logits = jnp.dot(h2, wr_ref[...], preferred_element_type=F32,
                     precision=lax.Precision.HIGHEST) + br_ref[...]
    tm = logits.shape[0]
    lane_e = lax.broadcasted_iota(jnp.int32, (tm, N_EXPERTS), 1).astype(F32)
    vals, idxs, sels = [], [], []
    cur = logits
    for _ in range(TOP_K):
        mk = jnp.max(cur, axis=-1, keepdims=True)
        ik = jnp.min(jnp.where(cur == mk, lane_e, float(N_EXPERTS)), axis=-1, keepdims=True)
        sel = lane_e == ik
        vals.append(mk)
        idxs.append(ik)
        sels.append(sel)
        cur = jnp.where(sel, -jnp.inf, cur)
    exps = [jnp.exp(v - vals[0]) for v in vals]
    inv = 1.0 / functools.reduce(lambda a, b: a + b, exps)
    probs = [e * inv for e in exps]

    onehot = functools.reduce(lambda a, b: a | b, sels).astype(F32)
    r = lax.broadcasted_iota(jnp.int32, (tm, tm), 0)
    c = lax.broadcasted_iota(jnp.int32, (tm, tm), 1)
    strict = (c < r).astype(BF16)
    before = jnp.dot(strict, onehot.astype(BF16), preferred_element_type=F32) + carry_ref[...]
    ranks = [jnp.sum(jnp.where(sel, before, 0.0), axis=-1, keepdims=True) for sel in sels]
    carry_ref[...] = carry_ref[...] + jnp.sum(onehot, axis=0, keepdims=True)

    idx_ref[...] = _lane_pack(idxs, LANES, jnp.int32)
    prob_ref[...] = _lane_pack(probs, LANES, F32)
    rank_ref[...] = _lane_pack([rk.astype(jnp.int32) for rk in ranks], LANES, jnp.int32)
    cnt_ref[...] = carry_ref[...].astype(jnp.int32)


def _postmix_router(x, y, mod, g_post, g_pre, w_router, b_router):
    s, d = x.shape
    tm = min(256, s)
    row = pl.BlockSpec((tm, d), lambda i: (i, 0))
    vec = lambda idx: pl.BlockSpec((1, d), lambda i: (0, idx))
    lanes = pl.BlockSpec((tm, LANES), lambda i: (i, 0))
    return pl.pallas_call(
        _postmix_kernel,
        out_shape=(jax.ShapeDtypeStruct((s, d), F32), jax.ShapeDtypeStruct((s, d), F32),
                   jax.ShapeDtypeStruct((s, LANES), jnp.int32), jax.ShapeDtypeStruct((s, LANES), F32),
                   jax.ShapeDtypeStruct((s, LANES), jnp.int32),
                   jax.ShapeDtypeStruct((1, N_EXPERTS), jnp.int32)),
        grid=(s // tm,),
        in_specs=[row, row, vec(0), vec(2), vec(0), vec(4), vec(3),
                  pl.BlockSpec((d, N_EXPERTS), lambda i: (0, 0)),
                  pl.BlockSpec((1, N_EXPERTS), lambda i: (0, 0))],
        out_specs=(row, row, lanes, lanes, lanes, pl.BlockSpec((1, N_EXPERTS), lambda i: (0, 0))),
        scratch_shapes=[pltpu.VMEM((1, N_EXPERTS), F32)],
        compiler_params=_params(("arbitrary",)),
        name="postmix_router",
    )(x, y, g_post.reshape(1, d), mod, g_pre.reshape(1, d), mod, mod, w_router, b_router.reshape(1, N_EXPERTS))


MOE_CAP = 1280
MOE_UNIT = 128
assert MOE_CAP % MOE_UNIT == 0
GU_TN = 512
DEINT = 256
DOWN_TN = 1024


def _item_maps(n_j):
    def clamp(i, j, ni):
        used = i < ni[0]
        return jnp.where(used, i, ni[0] - 1), jnp.where(used, j, n_j - 1)

    def item_rows(i, j, ie, ir, ni):
        return clamp(i, j, ni)[0], 0, 0

    def act_rows(i, j, ie, ir, ni):
        return clamp(i, j, ni)[0], 0

    def weights(i, j, ie, ir, ni):
        ii, jj = clamp(i, j, ni)
        return ie[ii], 0, jj

    return item_rows, act_rows, weights


def _dot_mixed(a_bf16, w_f32):
    return lax.dot_general(a_bf16, w_f32, (((1,), (0,)), ((), ())), preferred_element_type=F32)


def _for_rows(n, body, unroll=4):
    def trip(t, carry):
        for u in range(unroll):
            body(t * unroll + u)
        return carry

    lax.fori_loop(0, n // unroll, trip, 0)
    lax.fori_loop((n // unroll) * unroll, n, lambda r, carry: (body(r), carry)[1], 0)


def _wait_rows(src, dst, sem, n):
    n8 = pl.multiple_of((n // SUBLANES) * SUBLANES, SUBLANES)

    @pl.when(n8 > 0)
    def _():
        pltpu.make_async_copy(src.at[pl.ds(0, n8)], dst.at[pl.ds(0, n8)], sem).wait()

    one = pltpu.make_async_copy(src.at[pl.ds(0, 1)], dst.at[pl.ds(0, 1)], sem)
    lax.fori_loop(n8, n, lambda r, carry: (one.wait(), carry)[1], 0)


def _units(rows):
    return (rows + MOE_UNIT - 1) // MOE_UNIT


def _unit_rows(start_unit, n_units):
    return pl.ds(pl.multiple_of(start_unit * MOE_UNIT, MOE_UNIT), n_units * MOE_UNIT)


def _for_blocks(rows, compute, finish):
    units = _units(rows)
    triple = jnp.logical_and(units % 2 == 1, units >= 3)
    single = units == 1
    n2 = (units - jnp.where(triple, 3, 0) - jnp.where(single, 1, 0)) // 2

    def run(slices):
        zs = [compute(rs) for rs in slices]
        for rs, z in zip(slices, zs):
            finish(rs, z)

    def pair(pb, carry):
        run([_unit_rows(4 * pb, 2), _unit_rows(4 * pb + 2, 2)])
        return carry

    lax.fori_loop(0, n2 // 2, pair, 0)

    @pl.when(n2 % 2 == 1)
    def _():
        run([_unit_rows(2 * (n2 - 1), 2)])

    @pl.when(triple)
    def _():
        run([_unit_rows(2 * n2, 3)])

    @pl.when(single)
    def _():
        run([_unit_rows(0, 1)])


def _gateup_kernel(ie_ref, ir_ref, ni_ref, dst_ref, h_ref, w_ref, b_ref, o_ref, stg_ref, xb_ref, sem):
    i = pl.program_id(0)
    j = pl.program_id(1)
    cap = o_ref.shape[0]

    @pl.when(i >= ni_ref[0])
    def _():
        o_ref[...] = jnp.zeros_like(o_ref)

    @pl.when(i < ni_ref[0])
    def _():
        rows = ir_ref[i]

        def row_copy(r):
            token = lax.rem(dst_ref[0, 0, r], h_ref.shape[0])
            return pltpu.make_async_copy(h_ref.at[pl.ds(token, 1)], stg_ref.at[pl.ds(r, 1)], sem)

        @pl.when(j == 0)
        def _():
            @pl.when(i == 0)
            def _():
                stg_ref[...] = jnp.zeros_like(stg_ref)

            _for_rows(rows, lambda r: row_copy(r).start())
            _wait_rows(h_ref, stg_ref, sem, rows)

            def to_bf16(u, carry):
                xb_ref[_unit_rows(u, 1), :] = stg_ref[_unit_rows(u, 1), :].astype(BF16)
                return carry

            lax.fori_loop(0, _units(rows), to_bf16, 0)

        r = lax.broadcasted_iota(jnp.int32, (DEINT, DEINT), 0)
        c = lax.broadcasted_iota(jnp.int32, (DEINT, DEINT), 1)
        half = DEINT // 2
        perm = (r == jnp.where(c < half, 2 * c, 2 * (c - half) + 1)).astype(BF16)

        def gate_up(rs):
            return _dot_mixed(xb_ref[rs, :], w_ref[...]) + b_ref[...]

        def activate(rs, z):
            for t in range(GU_TN // DEINT):
                zz = jnp.dot(z[:, t * DEINT:(t + 1) * DEINT].astype(BF16), perm, preferred_element_type=F32)
                glu = jnp.minimum(zz[:, :half], SWIGLU_LIMIT)
                lin = jnp.clip(zz[:, half:], -SWIGLU_LIMIT, SWIGLU_LIMIT)
                act = glu * jax.nn.sigmoid(SWIGLU_ALPHA * glu) * (lin + 1.0)
                o_ref[rs, t * half:(t + 1) * half] = act.astype(o_ref.dtype)

        _for_blocks(rows, gate_up, activate)

        def zero_unit(u, carry):
            o_ref[_unit_rows(u, 1), :] = jnp.zeros((MOE_UNIT, o_ref.shape[1]), o_ref.dtype)
            return carry

        lax.fori_loop(_units(rows), cap // MOE_UNIT, zero_unit, 0)


def _moe_gateup(h2, dst, item_expert, item_rows, num_items, w_gate_up, b_gate_up, cap):
    s, d = h2.shape
    n_items = dst.shape[0]
    n_j = (2 * D_FF) // GU_TN
    item_map, _, weights = _item_maps(n_j)
    grid_spec = pltpu.PrefetchScalarGridSpec(
        num_scalar_prefetch=3,
        grid=(n_items, n_j),
        in_specs=[pl.BlockSpec((1, 1, cap), item_map, memory_space=pltpu.SMEM),
                  pl.BlockSpec(memory_space=pl.ANY),
                  pl.BlockSpec((None, d, GU_TN), weights),
                  pl.BlockSpec((None, 1, GU_TN), weights)],
        out_specs=pl.BlockSpec((cap, GU_TN // 2), lambda i, j, ie, ir, ni: (i, j)),
        scratch_shapes=[pltpu.VMEM((cap, d), F32), pltpu.VMEM((cap, d), BF16), pltpu.SemaphoreType.DMA(())],
    )
    return pl.pallas_call(
        _gateup_kernel,
        out_shape=jax.ShapeDtypeStruct((n_items * cap, D_FF), BF16),
        grid_spec=grid_spec,
        compiler_params=_params(("arbitrary", "arbitrary")),
        name="moe_gate_up",
    )(item_expert, item_rows, num_items, dst, h2, w_gate_up, b_gate_up.reshape(N_EXPERTS, 1, 2 * D_FF))


def _down_kernel(ie_ref, ir_ref, ni_ref, dst_ref, a_ref, w_ref, b_ref, o_ref, y_ref, sem):
    i = pl.program_id(0)
    j = pl.program_id(1)
    tn = w_ref.shape[1]
    n_j = y_ref.shape[1] // tn

    @pl.when(i < ni_ref[0])
    def _():
        rows = ir_ref[i]

        def down(rs):
            return _dot_mixed(a_ref[rs, :], w_ref[...]) + b_ref[...]

        for jj in range(n_j):
            @pl.when(j == jj)
            def _(jj=jj):
                def keep(rs, y):
                    y_ref[rs, jj * tn:(jj + 1) * tn] = y

                _for_blocks(rows, down, keep)

        @pl.when(j == n_j - 1)
        def _():
            def row_copy(r, dst_row):
                return pltpu.make_async_copy(y_ref.at[pl.ds(r, 1)], o_ref.at[pl.ds(dst_row, 1)], sem)

            _for_rows(rows, lambda r: row_copy(r, dst_ref[0, 0, r]).start())
            _wait_rows(y_ref, o_ref, sem, rows)


def _moe_down(act, dst, item_expert, item_rows, num_items, w_down, b_down, n_out_rows, cap):
    d = w_down.shape[2]
    tn = min(DOWN_TN, d)
    n_items = dst.shape[0]
    n_j = d // tn
    item_map, act_rows, weights = _item_maps(n_j)
    grid_spec = pltpu.PrefetchScalarGridSpec(
        num_scalar_prefetch=3,
        grid=(n_items, n_j),
        in_specs=[pl.BlockSpec((1, 1, cap), item_map, memory_space=pltpu.SMEM),
                  pl.BlockSpec((cap, D_FF), act_rows),
                  pl.BlockSpec((None, D_FF, tn), weights),
                  pl.BlockSpec((None, 1, tn), weights)],
        out_specs=pl.BlockSpec(memory_space=pl.ANY),
        scratch_shapes=[pltpu.VMEM((cap, d), F32), pltpu.SemaphoreType.DMA(())],
    )
    return pl.pallas_call(
        _down_kernel,
        out_shape=jax.ShapeDtypeStruct((n_out_rows, d), F32),
        grid_spec=grid_spec,
        compiler_params=_params(("arbitrary", "arbitrary")),
        name="moe_down",
    )(item_expert, item_rows, num_items, dst, act, w_down, b_down.reshape(N_EXPERTS, 1, d))


COMBINE_TOKENS = 128


def _combine_kernel(prob_ref, x1_ref, g_ref, gate_ref, ys_ref, o_ref):
    prob = prob_ref[...]
    y = prob[:, 0:1] * ys_ref[0]
    for k in range(1, TOP_K):
        y = y + prob[:, k:k + 1] * ys_ref[k]
    o_ref[...] = x1_ref[...] + gate_ref[...] * _rms(y, g_ref[...])


def _moe_combine(ys, probs, x1, g_post, mod, gate_idx):
    s, d = x1.shape
    tc = min(COMBINE_TOKENS, s)
    return pl.pallas_call(
        _combine_kernel,
        out_shape=jax.ShapeDtypeStruct((s, d), F32),
        grid=(s // tc,),
        in_specs=[pl.BlockSpec((tc, LANES), lambda i: (i, 0)),
                  pl.BlockSpec((tc, d), lambda i: (i, 0)),
                  pl.BlockSpec((1, d), lambda i: (0, 0)),
                  pl.BlockSpec((1, d), lambda i: (0, gate_idx)),
                  pl.BlockSpec((TOP_K, tc, d), lambda i: (0, i, 0))],
        out_specs=pl.BlockSpec((tc, d), lambda i: (i, 0)),
        compiler_params=_params(("arbitrary",)),
        name="moe_combine",
    )(probs, x1, g_post.reshape(1, d), mod, ys.reshape(TOP_K, s, d))


def _moe(h2, idx, probs, rank, counts, x1, mod, g_post, w_gate_up, b_gate_up, w_down, b_down):
    s, d = h2.shape
    cap = MOE_CAP
    i32 = jnp.int32
    counts = counts.reshape(N_EXPERTS)
    items_per_expert = (counts + cap - 1) // cap
    item_end = jnp.cumsum(items_per_expert)
    first_item = (item_end - items_per_expert).astype(i32)
    n_items_max = (s * TOP_K + N_EXPERTS * (cap - 1)) // cap
    num_items = item_end[-1].astype(i32).reshape(1)
    item_ids = jnp.arange(n_items_max, dtype=i32)
    item_expert = jnp.minimum(jnp.sum(item_end[None, :] <= item_ids[:, None], axis=1), N_EXPERTS - 1).astype(i32)
    item_rows = jnp.clip(counts[item_expert] - cap * (item_ids - first_item[item_expert]), 0, cap)
    item_rows = jnp.where(item_ids < num_items[0], item_rows, 0).astype(i32)

    pos = (first_item[idx[:, :TOP_K]] * cap + rank[:, :TOP_K]).reshape(-1)
    tok = jnp.repeat(jnp.arange(s, dtype=i32), TOP_K)
    slot = jnp.tile(jnp.arange(TOP_K, dtype=i32), s)
    dst = jnp.zeros((n_items_max * cap,), i32).at[pos].set(slot * s + tok, unique_indices=True)
    dst = dst.reshape(n_items_max, 1, cap)

    act = _moe_gateup(h2, dst, item_expert, item_rows, num_items, w_gate_up, b_gate_up, cap)
    ys = _moe_down(act, dst, item_expert, item_rows, num_items, w_down, b_down, TOP_K * s, cap)
    return _moe_combine(ys, probs, x1, g_post, mod, 5)


def _layer(x, c, positions, w_mod, b_mod, g_pre_mix, g_post_mix, g_pre_ffn, g_post_ffn,
           w_in, b_in, sinks, w_out, b_out, w_router, b_router, w_gate_up, b_gate_up, w_down, b_down):
    mod = _modulation(c, w_mod, b_mod)
    h = _prenorm(x, g_pre_mix, mod, 1, 0)
    proj = _inproj(h, w_in, b_in)
    o_a = _swa_attention(proj, positions, sinks)
    o_f = _fox_attention(proj)
    y = _outproj(o_a, o_f, w_out, b_out)
    x1, h2, idx, probs, rank, counts = _postmix_router(x, y, mod, g_post_mix, g_pre_ffn, w_router, b_router)
    return _moe(h2, idx, probs, rank, counts, x1, mod, g_post_ffn, w_gate_up, b_gate_up, w_down, b_down)


def kernel(x, c, positions, w_mod, b_mod, g_pre_mix, g_post_mix, g_pre_ffn, g_post_ffn, w_in, b_in, sinks,
           w_out, b_out, w_router, b_router, w_gate_up, b_gate_up, w_down, b_down):
    batch, seq, d = x.shape
    assert batch == 1 and w_mod.shape[0] == 1, "one sequence, one layer"
    assert seq % FOX_BLK == 0 or seq < FOX_BLK
    out = _layer(x[0], c[0], positions[0], w_mod[0], b_mod[0], g_pre_mix[0], g_post_mix[0], g_pre_ffn[0],
                 g_post_ffn[0], w_in[0], b_in[0], sinks[0], w_out[0], b_out[0], w_router[0], b_router[0],
                 w_gate_up[0], b_gate_up[0], w_down[0], b_down[0])
    return out[None]
```

```python
import functools

import jax
import jax.numpy as jnp
from jax import lax
from jax.experimental import pallas as pl
from jax.experimental.pallas import tpu as pltpu

HEAD_DIM = 64
SWA_Q_HEADS = 32
SWA_KV_HEADS = 4
SWA_GROUP = SWA_Q_HEADS // SWA_KV_HEADS
WINDOW = 128
FOX_HEADS = 32
ROPE_THETA = 10000.0
N_EXPERTS = 32
TOP_K = 4
D_FF = 1536
SWIGLU_LIMIT = 7.0
SWIGLU_ALPHA = 1.702
RMS_EPS = 1e-6
N_MOD = 6

SWA_Q_W = SWA_Q_HEADS * HEAD_DIM
SWA_KV_W = SWA_KV_HEADS * HEAD_DIM
FOX_W = FOX_HEADS * HEAD_DIM
MIX_W = SWA_Q_W + FOX_W
IN_W = SWA_Q_W + 2 * SWA_KV_W + 3 * FOX_W + FOX_HEADS
OFF_QA = 0
OFF_KA = SWA_Q_W
OFF_VA = OFF_KA + SWA_KV_W
OFF_QF = OFF_VA + SWA_KV_W
OFF_KF = OFF_QF + FOX_W
OFF_VF = OFF_KF + FOX_W
OFF_FL = OFF_VF + FOX_W

LANES = 128
SUBLANES = 8
V7X_VMEM_BYTES = 64 * 1024 * 1024
VMEM_LIMIT = V7X_VMEM_BYTES - 8 * 1024 * 1024

NEG = -1e30
LOG2E = 1.4426950408889634

F32 = jnp.float32
BF16 = jnp.bfloat16


def _params(sem, vmem=VMEM_LIMIT):
    return pltpu.CompilerParams(dimension_semantics=sem, vmem_limit_bytes=vmem)


MOD_ROWS = 256


def _mod_kernel(c_ref, w_ref, b_ref, o_ref):
    d, tn = w_ref.shape

    def body(r, acc):
        rows = pl.ds(pl.multiple_of(r * MOD_ROWS, MOD_ROWS), MOD_ROWS)
        c = c_ref[rows, :]
        sc = c * jax.nn.sigmoid(c)
        prod = w_ref[rows, :] * sc
        return acc + jnp.sum(prod.reshape(MOD_ROWS // SUBLANES, SUBLANES, tn), axis=0)

    acc = lax.fori_loop(0, d // MOD_ROWS, body, jnp.zeros((SUBLANES, tn), F32))
    o_ref[...] = jnp.sum(acc, axis=0, keepdims=True) + b_ref[...]


def _modulation(c, w_mod, b_mod):
    d, n = w_mod.shape
    tn = min(1024, n)
    assert n % tn == 0 and d % MOD_ROWS == 0
    return pl.pallas_call(
        _mod_kernel,
        out_shape=jax.ShapeDtypeStruct((1, n), F32),
        grid=(n // tn,),
        in_specs=[pl.BlockSpec((d, 1), lambda j: (0, 0)),
                  pl.BlockSpec((d, tn), lambda j: (0, j)),
                  pl.BlockSpec((1, tn), lambda j: (0, j))],
        out_specs=pl.BlockSpec((1, tn), lambda j: (0, j)),
        compiler_params=_params(("arbitrary",)),
        name="modulation",
    )(c.reshape(d, 1), w_mod, b_mod.reshape(1, n))


def _rms(x, g):
    inv = lax.rsqrt(jnp.mean(x * x, axis=-1, keepdims=True) + RMS_EPS)
    return x * inv * g


def _prenorm_kernel(x_ref, g_ref, scale_ref, shift_ref, o_ref):
    h = _rms(x_ref[...], g_ref[...]) * (1.0 + scale_ref[...]) + shift_ref[...]
    o_ref[...] = h.astype(o_ref.dtype)


def _prenorm(x, g, mod, scale_idx, shift_idx):
    s, d = x.shape
    tm = min(256, s)
    return pl.pallas_call(
        _prenorm_kernel,
        out_shape=jax.ShapeDtypeStruct((s, d), BF16),
        grid=(s // tm,),
        in_specs=[pl.BlockSpec((tm, d), lambda i: (i, 0)),
                  pl.BlockSpec((1, d), lambda i: (0, 0)),
                  pl.BlockSpec((1, d), lambda i: (0, scale_idx)),
                  pl.BlockSpec((1, d), lambda i: (0, shift_idx))],
        out_specs=pl.BlockSpec((tm, d), lambda i: (i, 0)),
        compiler_params=_params(("arbitrary",)),
        name="prenorm",
    )(x, g.reshape(1, d), mod, mod)


def _inproj_kernel(a_ref, wt_ref, b_ref, o_ref):
    acc = lax.dot_general(a_ref[...], wt_ref[...].astype(BF16), (((1,), (1,)), ((), ())),
                          preferred_element_type=F32)
    o_ref[...] = acc + b_ref[...]


def _inproj(h, w, b):
    m, k = h.shape
    n = w.shape[1]
    tm = min(1024, m)
    tn = 512
    return pl.pallas_call(
        _inproj_kernel,
        out_shape=jax.ShapeDtypeStruct((m, n), F32),
        grid=(m // tm, pl.cdiv(n, tn)),
        in_specs=[pl.BlockSpec((tm, k), lambda i, j: (i, 0)),
                  pl.BlockSpec((tn, k), lambda i, j: (j, 0)),
                  pl.BlockSpec((1, tn), lambda i, j: (0, j))],
        out_specs=pl.BlockSpec((tm, tn), lambda i, j: (i, j)),
        compiler_params=_params(("arbitrary", "arbitrary")),
        name="in_projection",
    )(h, w.T, b.reshape(1, n))


def _outproj_kernel(a1_ref, a2_ref, w_ref, b_ref, o_ref):
    k1 = a1_ref.shape[1]
    w = w_ref[...].astype(BF16)
    acc = jnp.dot(a1_ref[...], w[:k1], preferred_element_type=F32)
    acc = acc + jnp.dot(a2_ref[...], w[k1:], preferred_element_type=F32)
    o_ref[...] = acc + b_ref[...]


def _outproj(o_a, o_f, w, b):
    m, k1 = o_a.shape
    k2 = o_f.shape[1]
    n = w.shape[1]
    tm = min(1024, m)
    tn = min(512, n)
    return pl.pallas_call(
        _outproj_kernel,
        out_shape=jax.ShapeDtypeStruct((m, n), F32),
        grid=(m // tm, n // tn),
        in_specs=[pl.BlockSpec((tm, k1), lambda i, j: (i, 0)),
                  pl.BlockSpec((tm, k2), lambda i, j: (i, 0)),
                  pl.BlockSpec((k1 + k2, tn), lambda i, j: (0, j)),
                  pl.BlockSpec((1, tn), lambda i, j: (0, j))],
        out_specs=pl.BlockSpec((tm, tn), lambda i, j: (i, j)),
        compiler_params=_params(("arbitrary", "arbitrary")),
        name="out_projection",
    )(o_a, o_f, w, b.reshape(1, n))


def _swap_halves(x, first_half):
    return jnp.where(first_half, pltpu.roll(x, LANES - HEAD_DIM // 2, 1),
                     pltpu.roll(x, HEAD_DIM // 2, 1))


ROPE_ROWS = 512


def _rope_table_kernel(pos_ref, invf_ref, cos_ref, sin_ref):
    lane = lax.broadcasted_iota(jnp.int32, cos_ref.shape, 1)
    first_half = (lane % HEAD_DIM) < (HEAD_DIM // 2)
    ang = pos_ref[...].astype(F32) * invf_ref[...]
    sin = jnp.sin(ang)
    cos_ref[...] = jnp.cos(ang)
    sin_ref[...] = jnp.where(first_half, -sin, sin)


def _rope_tables(positions):
    s = positions.shape[0]
    tr = min(ROPE_ROWS, s)
    half = HEAD_DIM // 2
    inv_freq = jnp.power(ROPE_THETA, -jnp.arange(half, dtype=F32) * (2.0 / HEAD_DIM))
    invf = jnp.tile(inv_freq, LANES // half).reshape(1, LANES)
    table = jax.ShapeDtypeStruct((s, LANES), F32)
    return pl.pallas_call(
        _rope_table_kernel,
        out_shape=(table, table),
        grid=(s // tr,),
        in_specs=[pl.BlockSpec((tr, 1), lambda i: (i, 0)), pl.BlockSpec((1, LANES), lambda i: (0, 0))],
        out_specs=(pl.BlockSpec((tr, LANES), lambda i: (i, 0)), pl.BlockSpec((tr, LANES), lambda i: (i, 0))),
        compiler_params=_params(("arbitrary",)),
        name="rope_tables",
    )(positions.reshape(s, 1), invf)


def _swa_kernel(sinks_ref, cosc_ref, sinc_ref, cosp_ref, sinp_ref, q_ref, kp_ref, kc_ref, vp_ref, vc_ref, o_ref):
    n = pl.program_id(0)
    w = WINDOW
    lane = lax.broadcasted_iota(jnp.int32, (w, LANES), 1)
    first_half = (lane % HEAD_DIM) < (HEAD_DIM // 2)
    lane2 = lax.broadcasted_iota(jnp.int32, (2 * w, LANES), 1)
    lo64_2 = lane2 < HEAD_DIM

    cos_c, sin_c = cosc_ref[...], sinc_ref[...]
    cos_p, sin_p = cosp_ref[...], sinp_ref[...]

    def rope(t, cos, sin):
        return t * cos + _swap_halves(t, first_half) * sin

    stacked = (SWA_GROUP // 2) * w
    qi = lax.rem(lax.broadcasted_iota(jnp.int32, (stacked, 2 * w), 0), w)
    kj = lax.broadcasted_iota(jnp.int32, (stacked, 2 * w), 1)
    diff = qi + w - kj
    valid4 = (diff >= 0) & (diff < w) & ((n * w - w + kj) >= 0)

    scale = HEAD_DIM ** -0.5
    nt_dims = (((1,), (1,)), ((), ()))
    for kvp in range(SWA_KV_HEADS // 2):
        cols = slice(kvp * LANES, (kvp + 1) * LANES)
        k2 = jnp.concatenate([rope(kp_ref[:, cols], cos_p, sin_p),
                              rope(kc_ref[:, cols], cos_c, sin_c)], axis=0)
        v2 = jnp.concatenate([vp_ref[:, cols], vc_ref[:, cols]], axis=0)
        k2r = pltpu.roll(k2, HEAD_DIM, 1)
        v2r = pltpu.roll(v2, HEAD_DIM, 1)
        for sub in range(2):
            hkv = 2 * kvp + sub
            src_k, alt_k = (k2, k2r) if sub == 0 else (k2r, k2)
            src_v, alt_v = (v2, v2r) if sub == 0 else (v2r, v2)
            ka = jnp.where(lo64_2, src_k, 0.0).astype(BF16)
            kb = jnp.where(lo64_2, 0.0, alt_k).astype(BF16)
            va = jnp.where(lo64_2, src_v, 0.0).astype(BF16)
            vb = jnp.where(lo64_2, 0.0, alt_v).astype(BF16)
            pairs = [hkv * (SWA_GROUP // 2) + gp for gp in range(SWA_GROUP // 2)]
            q = jnp.concatenate([(rope(q_ref[:, pr * LANES:(pr + 1) * LANES], cos_c, sin_c) * scale).astype(BF16)
                                 for pr in pairs], axis=0)
            scores = [lax.dot_general(q, kk, nt_dims, preferred_element_type=F32) for kk in (ka, kb)]
            o = jnp.zeros((len(pairs) * w, LANES), F32)
            for which, (s, vv) in enumerate(zip(scores, (va, vb))):
                sink = jnp.concatenate([jnp.full((w, 1), sinks_ref[2 * pr + which], F32) for pr in pairs], axis=0)
                s = jnp.where(valid4, s, NEG)
                m = jnp.maximum(jnp.max(s, axis=-1, keepdims=True), sink)
                p = jnp.exp(s - m)
                den = jnp.sum(p, axis=-1, keepdims=True) + jnp.exp(sink - m)
                pv = jnp.dot(p.astype(BF16), vv, preferred_element_type=F32)
                o = o + pv * (1.0 / den)
            for r, pr in enumerate(pairs):
                o_ref[:, pr * LANES:(pr + 1) * LANES] = o[r * w:(r + 1) * w].astype(o_ref.dtype)


def _swa_attention(proj, positions, sinks):
    s = proj.shape[0]
    w = WINDOW
    nb = s // w
    cos, sin = _rope_tables(positions)
    prev = lambda n: jnp.maximum(n - 1, 0)
    ka_blk = OFF_KA // SWA_KV_W
    va_blk = OFF_VA // SWA_KV_W
    cur_tab = pl.BlockSpec((w, LANES), lambda n: (n, 0))
    prev_tab = pl.BlockSpec((w, LANES), lambda n: (prev(n), 0))
    return pl.pallas_call(
        _swa_kernel,
        out_shape=jax.ShapeDtypeStruct((s, SWA_Q_W), BF16),
        grid=(nb,),
        in_specs=[pl.BlockSpec(memory_space=pltpu.SMEM),
                  cur_tab, cur_tab, prev_tab, prev_tab,
                  pl.BlockSpec((w, SWA_Q_W), lambda n: (n, 0)),
                  pl.BlockSpec((w, SWA_KV_W), lambda n: (prev(n), ka_blk)),
                  pl.BlockSpec((w, SWA_KV_W), lambda n: (n, ka_blk)),
                  pl.BlockSpec((w, SWA_KV_W), lambda n: (prev(n), va_blk)),
                  pl.BlockSpec((w, SWA_KV_W), lambda n: (n, va_blk))],
        out_specs=pl.BlockSpec((w, SWA_Q_W), lambda n: (n, 0)),
        compiler_params=_params(("arbitrary",)),
        name="swa_sink_attention",
    )(sinks, cos, sin, cos, sin, proj, proj, proj, proj, proj)


CUM_ROWS = 256


def _cum_kernel(f_ref, o_ref, carry_ref):
    @pl.when(pl.program_id(0) == 0)
    def _():
        carry_ref[...] = jnp.zeros_like(carry_ref)

    f = f_ref[:, :FOX_HEADS]
    ls = jnp.minimum(f, 0.0) - jnp.log1p(jnp.exp(-jnp.abs(f)))
    r = lax.broadcasted_iota(jnp.int32, (CUM_ROWS, CUM_ROWS), 0)
    c = lax.broadcasted_iota(jnp.int32, (CUM_ROWS, CUM_ROWS), 1)
    tri = (c <= r).astype(F32)
    cum = jnp.dot(tri, ls, preferred_element_type=F32, precision=lax.Precision.HIGHEST) + carry_ref[...]
    o_ref[...] = cum
    carry_ref[...] = cum[CUM_ROWS - 1:CUM_ROWS, :]


def _fox_cum(proj):
    s = proj.shape[0]
    return pl.pallas_call(
        _cum_kernel,
        out_shape=jax.ShapeDtypeStruct((s, FOX_HEADS), F32),
        grid=(s // CUM_ROWS,),
        in_specs=[pl.BlockSpec((CUM_ROWS, LANES), lambda i: (i, OFF_FL // LANES))],
        out_specs=pl.BlockSpec((CUM_ROWS, FOX_HEADS), lambda i: (i, 0)),
        scratch_shapes=[pltpu.VMEM((1, FOX_HEADS), F32)],
        compiler_params=_params(("arbitrary",)),
        name="fox_decay_cumsum",
    )(proj)


FOX_BLK = 512
AUG_CQ = HEAD_DIM
AUG_ONE = HEAD_DIM + 3
AUG_END = HEAD_DIM + 6
PREP_HEADS = 8
SKIP_MARGIN = 170.0
NORM_SLACK = 1.001
BF16_ROUND_UP = 1.0 + 2.0 ** -7
assert OFF_QF % (PREP_HEADS * HEAD_DIM) == 0 and OFF_KF % (PREP_HEADS * HEAD_DIM) == 0
assert OFF_VF % (PREP_HEADS * HEAD_DIM) == 0


def _split3(c):
    hi = c.astype(BF16).astype(F32)
    r = c - hi
    mid = r.astype(BF16).astype(F32)
    lo = (r - mid).astype(BF16).astype(F32)
    return hi, mid, lo


def _fox_prep_kernel(q_ref, k_ref, v_ref, cum_ref, qo_ref, ko_ref, vo_ref, qn_ref, kn_ref):
    j = pl.program_id(0)
    tb = q_ref.shape[0]
    lane = lax.broadcasted_iota(jnp.int32, (tb, LANES), 1)
    lo64 = lane < HEAD_DIM
    lane_h = lax.broadcasted_iota(jnp.int32, (tb, FOX_HEADS), 1)
    cum = cum_ref[...]
    scale = HEAD_DIM ** -0.5 * LOG2E
    for local in range(PREP_HEADS):
        pair, hh = divmod(local, 2)
        h = PREP_HEADS * j + local
        col = jnp.sum(jnp.where(lane_h == h, cum, 0.0), axis=-1, keepdims=True)
        hi, mid, lo = _split3((col - col[0:1, :]) * LOG2E)
        cols = slice(pair * LANES, (pair + 1) * LANES)
        q = q_ref[:, cols] * scale
        k = k_ref[:, cols]
        v = v_ref[:, cols]
        if hh == 1:
            q = pltpu.roll(q, HEAD_DIM, 1)
            k = pltpu.roll(k, HEAD_DIM, 1)
            v = pltpu.roll(v, HEAD_DIM, 1)
        q_aug = jnp.where(lane == AUG_CQ, hi, jnp.where(lane == AUG_CQ + 1, mid,
                jnp.where(lane == AUG_CQ + 2, lo, jnp.where(lane < AUG_END, 1.0, 0.0))))
        k_aug = jnp.where(lane < AUG_ONE, 1.0, jnp.where(lane == AUG_ONE, -hi,
                jnp.where(lane == AUG_ONE + 1, -mid, jnp.where(lane == AUG_ONE + 2, -lo, 0.0))))
        v_aug = jnp.where(lane == HEAD_DIM, 1.0, 0.0)
        qb = jnp.where(lo64, q, q_aug).astype(BF16)
        kb = jnp.where(lo64, k, k_aug).astype(BF16)
        qo_ref[local] = qb
        ko_ref[local] = kb
        vo_ref[local, 0] = jnp.where(lo64, v, v_aug).T.astype(BF16)
        ones = jnp.ones((LANES, LANES), BF16)
        for src, norm_ref in ((qb, qn_ref), (kb, kn_ref)):
            x = jnp.where(lo64, src.astype(F32), 0.0)
            sq = jnp.dot((x * x * BF16_ROUND_UP).astype(BF16), ones, preferred_element_type=F32)
            norm_ref[local, 0] = jnp.broadcast_to(jnp.sqrt(jnp.max(sq, axis=0, keepdims=True)),
                                                  norm_ref.shape[2:])


def _fox_prep(proj, cum, tb):
    s = proj.shape[0]
    width = PREP_HEADS * HEAD_DIM
    shp = jax.ShapeDtypeStruct((FOX_HEADS, s, LANES), BF16)
    ospec = pl.BlockSpec((PREP_HEADS, tb, LANES), lambda j, i: (j, i, 0))
    norms = jax.ShapeDtypeStruct((FOX_HEADS, s // tb, SUBLANES, LANES), F32)
    nspec = pl.BlockSpec((PREP_HEADS, 1, SUBLANES, LANES), lambda j, i: (j, i, 0, 0))
    return pl.pallas_call(
        _fox_prep_kernel,
        out_shape=(shp, shp, jax.ShapeDtypeStruct((FOX_HEADS, s // tb, LANES, tb), BF16), norms, norms),
        grid=(FOX_HEADS // PREP_HEADS, s // tb),
        in_specs=[pl.BlockSpec((tb, width), lambda j, i: (i, OFF_QF // width + j)),
                  pl.BlockSpec((tb, width), lambda j, i: (i, OFF_KF // width + j)),
                  pl.BlockSpec((tb, width), lambda j, i: (i, OFF_VF // width + j)),
                  pl.BlockSpec((tb, FOX_HEADS), lambda j, i: (i, 0))],
        out_specs=(ospec, ospec, pl.BlockSpec((PREP_HEADS, 1, LANES, tb), lambda j, i: (j, i, 0, 0)),
                   nspec, nspec),
        compiler_params=_params(("arbitrary", "arbitrary")),
        name="fox_prepare",
    )(proj, proj, proj, cum)


def _fox_kernel(a_ref, e_ref, qk_ref, q_ref, k_ref, vt_ref, o_ref, acc_ref, m_ref, s0_ref):
    j = pl.program_id(0)
    i = pl.program_id(1)
    tb = q_ref.shape[1]
    nt_dims = (((1,), (1,)), ((), ()))
    key = lax.broadcasted_iota(jnp.int32, (tb, tb), 0)
    qry = lax.broadcasted_iota(jnp.int32, (tb, tb), 1)
    causal = key <= qry

    def scores(hh, kt, diagonal=False):
        rows = pl.ds(pl.multiple_of(kt * tb, tb), tb)
        s = lax.dot_general(k_ref[hh, rows, :], q_ref[hh], nt_dims, preferred_element_type=F32)
        return jnp.where(causal, s, NEG) if diagonal else s

    def update(hh, kt, s):
        h = 2 * j + hh
        delta = (a_ref[h, i] - a_ref[h, kt]) * LOG2E
        m_old = m_ref[hh]
        m_new = jnp.maximum(m_old, jnp.max(s, axis=0, keepdims=True) + delta)
        p = jnp.exp2(s - (m_new - delta))
        alpha = jnp.exp2(m_old - m_new)
        acc_ref[hh] = alpha * acc_ref[hh] + jnp.dot(vt_ref[hh, kt], p.astype(BF16), preferred_element_type=F32)
        m_ref[hh] = m_new

    def step(kt, diagonal=False):
        s1 = scores(1, kt, diagonal)
        update(0, kt, s0_ref[...])
        s0_ref[...] = scores(0, jnp.maximum(kt - 1, 0))
        update(1, kt, s1)

    def tiles_needed(hh):
        h = 2 * j + hh
        floor = jnp.min(m_ref[hh]) - SKIP_MARGIN
        top = qk_ref[h, i] + a_ref[h, i] * LOG2E

        def first_kept(kt, lo):
            kept = top - e_ref[h, kt] * LOG2E >= floor
            return jnp.where(kept, jnp.minimum(lo, kt), lo)

        return i - lax.fori_loop(0, i, first_kept, i)

    m_ref[...] = jnp.full(m_ref.shape, NEG, F32)
    acc_ref[...] = jnp.zeros(acc_ref.shape, F32)
    s0_ref[...] = scores(0, i, diagonal=True)
    step(i, diagonal=True)
    n_back = jnp.maximum(tiles_needed(0), tiles_needed(1))
    lax.fori_loop(0, n_back, lambda t, carry: (step(i - 1 - t), carry)[1], 0)
    acc0 = acc_ref[0]
    acc1 = acc_ref[1]
    o0 = (acc0 * (1.0 / acc0[HEAD_DIM:HEAD_DIM + 1, :])).T
    o1 = (acc1 * (1.0 / acc1[HEAD_DIM:HEAD_DIM + 1, :])).T
    lo64 = lax.broadcasted_iota(jnp.int32, (tb, LANES), 1) < HEAD_DIM
    o_ref[...] = jnp.where(lo64, o0, pltpu.roll(o1, HEAD_DIM, 1)).astype(o_ref.dtype)


def _fox_attention(proj):
    s = proj.shape[0]
    tb = min(FOX_BLK, s)
    cum = _fox_cum(proj)
    q_aug, k_aug, vt_aug, q_norm, k_norm = _fox_prep(proj, cum, tb)
    block_first = cum[::tb].T
    block_last = cum[tb - 1::tb].T
    qk_bound = q_norm[:, :, 0, 0] * jnp.max(k_norm[:, :, 0, 0], axis=1, keepdims=True) * NORM_SLACK
    npairs = FOX_HEADS // 2
    smem = pl.BlockSpec(memory_space=pltpu.SMEM)
    return pl.pallas_call(
        _fox_kernel,
        out_shape=jax.ShapeDtypeStruct((s, FOX_W), BF16),
        grid=(npairs, s // tb),
        in_specs=[smem, smem, smem,
                  pl.BlockSpec((2, tb, LANES), lambda j, i: (j, i, 0)),
                  pl.BlockSpec((2, s, LANES), lambda j, i: (j, 0, 0)),
                  pl.BlockSpec((2, s // tb, LANES, tb), lambda j, i: (j, 0, 0, 0))],
        out_specs=pl.BlockSpec((tb, LANES), lambda j, i: (i, j)),
        scratch_shapes=[pltpu.VMEM((2, LANES, tb), F32), pltpu.VMEM((2, 1, tb), F32), pltpu.VMEM((tb, tb), F32)],
        compiler_params=_params(("arbitrary", "arbitrary")),
        name="fox_attention",
    )(block_first, block_last, qk_bound, q_aug, k_aug, vt_aug)


def _lane_pack(cols, width, dtype):
    rows = cols[0].shape[0]
    lane = lax.broadcasted_iota(jnp.int32, (rows, width), 1)
    out = jnp.zeros((rows, width), dtype)
    for k, cval in enumerate(cols):
        out = jnp.where(lane == k, cval.astype(dtype), out)
    return out


def _postmix_kernel(x_ref, y_ref, gpost_ref, gate_ref, gpre_ref, scale_ref, shift_ref, wr_ref, br_ref,
                    x1_ref, h2_ref, idx_ref, prob_ref, rank_ref, cnt_ref, carry_ref):
    step = pl.program_id(0)

    @pl.when(step == 0)
    def _():
        carry_ref[...] = jnp.zeros_like(carry_ref)

    x1 = x_ref[...] + gate_ref[...] * _rms(y_ref[...], gpost_ref[...])
    x1_ref[...] = x1
    h2 = _rms(x1, gpre_ref[...]) * (1.0 + scale_ref[...]) + shift_ref[...]
    h2_ref[...] = h2
    logits = jnp.dot(h2, wr_ref[...], preferred_element_type=F32,
                     precision=lax.Precision.HIGHEST) + br_ref[...]
    tm = logits.shape[0]
    lane_e = lax.broadcasted_iota(jnp.int32, (tm, N_EXPERTS), 1).astype(F32)
    vals, idxs, sels = [], [], []
    cur = logits
    for _ in range(TOP_K):
        mk = jnp.max(cur, axis=-1, keepdims=True)
        ik = jnp.min(jnp.where(cur == mk, lane_e, float(N_EXPERTS)), axis=-1, keepdims=True)
        sel = lane_e == ik
        vals.append(mk)
        idxs.append(ik)
        sels.append(sel)
        cur = jnp.where(sel, -jnp.inf, cur)
    exps = [jnp.exp(v - vals[0]) for v in vals]
    inv = 1.0 / functools.reduce(lambda a, b: a + b, exps)
    probs = [e * inv for e in exps]

    onehot = functools.reduce(lambda a, b: a | b, sels).astype(F32)
    r = lax.broadcasted_iota(jnp.int32, (tm, tm), 0)
    c = lax.broadcasted_iota(jnp.int32, (tm, tm), 1)
    strict = (c < r).astype(BF16)
    before = jnp.dot(strict, onehot.astype(BF16), preferred_element_type=F32) + carry_ref[...]
    ranks = [jnp.sum(jnp.where(sel, before, 0.0), axis=-1, keepdims=True) for sel in sels]
    carry_ref[...] = carry_ref[...] + jnp.sum(onehot, axis=0, keepdims=True)

    idx_ref[...] = _lane_pack(idxs, LANES, jnp.int32)
    prob_ref[...] = _lane_pack(probs, LANES, F32)
    rank_ref[...] = _lane_pack([rk.astype(jnp.int32) for rk in ranks], LANES, jnp.int32)
    cnt_ref[...] = carry_ref[...].astype(jnp.int32)


def _postmix_router(x, y, mod, g_post, g_pre, w_router, b_router):
    s, d = x.shape
    tm = min(256, s)
    row = pl.BlockSpec((tm, d), lambda i: (i, 0))
    vec = lambda idx: pl.BlockSpec((1, d), lambda i: (0, idx))
    lanes = pl.BlockSpec((tm, LANES), lambda i: (i, 0))
    return pl.pallas_call(
        _postmix_kernel,
        out_shape=(jax.ShapeDtypeStruct((s, d), F32), jax.ShapeDtypeStruct((s, d), F32),
                   jax.ShapeDtypeStruct((s, LANES), jnp.int32), jax.ShapeDtypeStruct((s, LANES), F32),
                   jax.ShapeDtypeStruct((s, LANES), jnp.int32),
                   jax.ShapeDtypeStruct((1, N_EXPERTS), jnp.int32)),
        grid=(s // tm,),
        in_specs=[row, row, vec(0), vec(2), vec(0), vec(4), vec(3),
                  pl.BlockSpec((d, N_EXPERTS), lambda i: (0, 0)),
                  pl.BlockSpec((1, N_EXPERTS), lambda i: (0, 0))],
        out_specs=(row, row, lanes, lanes, lanes, pl.BlockSpec((1, N_EXPERTS), lambda i: (0, 0))),
        scratch_shapes=[pltpu.VMEM((1, N_EXPERTS), F32)],
        compiler_params=_params(("arbitrary",)),
        name="postmix_router",
    )(x, y, g_post.reshape(1, d), mod, g_pre.reshape(1, d), mod, mod, w_router, b_router.reshape(1, N_EXPERTS))


MOE_CAP = 1280
MOE_UNIT = 128
assert MOE_CAP % MOE_UNIT == 0
GU_TN = 512
DEINT = 256
DOWN_TN = 1024


def _item_maps(n_j):
    def clamp(i, j, ni):
        used = i < ni[0]
        return jnp.where(used, i, ni[0] - 1), jnp.where(used, j, n_j - 1)

    def item_rows(i, j, ie, ir, ni):
        return clamp(i, j, ni)[0], 0, 0

    def act_rows(i, j, ie, ir, ni):
        return clamp(i, j, ni)[0], 0

    def weights(i, j, ie, ir, ni):
        ii, jj = clamp(i, j, ni)
        return ie[ii], 0, jj

    return item_rows, act_rows, weights


def _dot_mixed(a_bf16, w_f32):
    return lax.dot_general(a_bf16, w_f32, (((1,), (0,)), ((), ())), preferred_element_type=F32)


def _for_rows(n, body, unroll=SUBLANES):
    def trip(t, carry):
        for u in range(unroll):
            body(t * unroll + u)
        return carry

    lax.fori_loop(0, n // unroll, trip, 0)
    lax.fori_loop((n // unroll) * unroll, n, lambda r, carry: (body(r), carry)[1], 0)


def _wait_rows(src, dst, sem, n):
    n8 = pl.multiple_of((n // SUBLANES) * SUBLANES, SUBLANES)

    @pl.when(n8 > 0)
    def _():
        pltpu.make_async_copy(src.at[pl.ds(0, n8)], dst.at[pl.ds(0, n8)], sem).wait()

    one = pltpu.make_async_copy(src.at[pl.ds(0, 1)], dst.at[pl.ds(0, 1)], sem)
    lax.fori_loop(n8, n, lambda r, carry: (one.wait(), carry)[1], 0)


def _units(rows):
    return (rows + MOE_UNIT - 1) // MOE_UNIT


def _unit_rows(start_unit, n_units):
    return pl.ds(pl.multiple_of(start_unit * MOE_UNIT, MOE_UNIT), n_units * MOE_UNIT)


def _for_blocks(rows, compute, finish):
    units = _units(rows)
    triple = jnp.logical_and(units % 2 == 1, units >= 3)
    single = units == 1
    n2 = (units - jnp.where(triple, 3, 0) - jnp.where(single, 1, 0)) // 2

    def run(slices):
        zs = [compute(rs) for rs in slices]
        for rs, z in zip(slices, zs):
            finish(rs, z)

    def pair(pb, carry):
        run([_unit_rows(4 * pb, 2), _unit_rows(4 * pb + 2, 2)])
        return carry

    lax.fori_loop(0, n2 // 2, pair, 0)

    @pl.when(n2 % 2 == 1)
    def _():
        run([_unit_rows(2 * (n2 - 1), 2)])

    @pl.when(triple)
    def _():
        run([_unit_rows(2 * n2, 3)])

    @pl.when(single)
    def _():
        run([_unit_rows(0, 1)])


def _gateup_kernel(ie_ref, ir_ref, ni_ref, dst_ref, h_ref, w_ref, b_ref, o_ref, stg_ref, xb_ref, sem):
    i = pl.program_id(0)
    j = pl.program_id(1)
    cap = o_ref.shape[0]

    @pl.when(i >= ni_ref[0])
    def _():
        o_ref[...] = jnp.zeros_like(o_ref)

    @pl.when(i < ni_ref[0])
    def _():
        rows = ir_ref[i]

        def row_copy(r):
            token = lax.rem(dst_ref[0, 0, r].astype(jnp.uint32), jnp.uint32(h_ref.shape[0])).astype(jnp.int32)
            return pltpu.make_async_copy(h_ref.at[pl.ds(token, 1)], stg_ref.at[pl.ds(r, 1)], sem)

        @pl.when(j == 0)
        def _():
            @pl.when(i == 0)
            def _():
                stg_ref[...] = jnp.zeros_like(stg_ref)

            _for_rows(rows, lambda r: row_copy(r).start())
            _wait_rows(h_ref, stg_ref, sem, rows)

            def to_bf16(u, carry):
                xb_ref[_unit_rows(u, 1), :] = stg_ref[_unit_rows(u, 1), :].astype(BF16)
                return carry

            lax.fori_loop(0, _units(rows), to_bf16, 0)

        r = lax.broadcasted_iota(jnp.int32, (DEINT, DEINT), 0)
        c = lax.broadcasted_iota(jnp.int32, (DEINT, DEINT), 1)
        half = DEINT // 2
        perm = (r == jnp.where(c < half, 2 * c, 2 * (c - half) + 1)).astype(BF16)

        def gate_up(rs):
            return _dot_mixed(xb_ref[rs, :], w_ref[...]) + b_ref[...]

        def activate(rs, z):
            for t in range(GU_TN // DEINT):
                zz = jnp.dot(z[:, t * DEINT:(t + 1) * DEINT].astype(BF16), perm, preferred_element_type=F32)
                glu = jnp.minimum(zz[:, :half], SWIGLU_LIMIT)
                lin = jnp.clip(zz[:, half:], -SWIGLU_LIMIT, SWIGLU_LIMIT)
                act = glu * jax.nn.sigmoid(SWIGLU_ALPHA * glu) * (lin + 1.0)
                o_ref[rs, t * half:(t + 1) * half] = act.astype(o_ref.dtype)

        _for_blocks(rows, gate_up, activate)

        def zero_unit(u, carry):
            o_ref[_unit_rows(u, 1), :] = jnp.zeros((MOE_UNIT, o_ref.shape[1]), o_ref.dtype)
            return carry

        lax.fori_loop(_units(rows), cap // MOE_UNIT, zero_unit, 0)


def _moe_gateup(h2, dst, item_expert, item_rows, num_items, w_gate_up, b_gate_up, cap):
    s, d = h2.shape
    n_items = dst.shape[0]
    n_j = (2 * D_FF) // GU_TN
    item_map, _, weights = _item_maps(n_j)
    grid_spec = pltpu.PrefetchScalarGridSpec(
        num_scalar_prefetch=3,
        grid=(n_items, n_j),
        in_specs=[pl.BlockSpec((1, 1, cap), item_map, memory_space=pltpu.SMEM),
                  pl.BlockSpec(memory_space=pl.ANY),
                  pl.BlockSpec((None, d, GU_TN), weights),
                  pl.BlockSpec((None, 1, GU_TN), weights)],
        out_specs=pl.BlockSpec((cap, GU_TN // 2), lambda i, j, ie, ir, ni: (i, j)),
        scratch_shapes=[pltpu.VMEM((cap, d), F32), pltpu.VMEM((cap, d), BF16), pltpu.SemaphoreType.DMA(())],
    )
    return pl.pallas_call(
        _gateup_kernel,
        out_shape=jax.ShapeDtypeStruct((n_items * cap, D_FF), BF16),
        grid_spec=grid_spec,
        compiler_params=_params(("arbitrary", "arbitrary")),
        name="moe_gate_up",
    )(item_expert, item_rows, num_items, dst, h2, w_gate_up, b_gate_up.reshape(N_EXPERTS, 1, 2 * D_FF))


def _down_kernel(ie_ref, ir_ref, ni_ref, dst_ref, a_ref, w_ref, b_ref, o_ref, y_ref, sem):
    i = pl.program_id(0)
    j = pl.program_id(1)
    tn = w_ref.shape[1]
    n_j = y_ref.shape[1] // tn

    @pl.when(i < ni_ref[0])
    def _():
        rows = ir_ref[i]

        def down(rs):
            return _dot_mixed(a_ref[rs, :], w_ref[...]) + b_ref[...]

        for jj in range(n_j):
            @pl.when(j == jj)
            def _(jj=jj):
                def keep(rs, y):
                    y_ref[rs, jj * tn:(jj + 1) * tn] = y

                _for_blocks(rows, down, keep)

        @pl.when(j == n_j - 1)
        def _():
            def row_copy(r, dst_row):
                return pltpu.make_async_copy(y_ref.at[pl.ds(r, 1)], o_ref.at[pl.ds(dst_row, 1)], sem)

            _for_rows(rows, lambda r: row_copy(r, dst_ref[0, 0, r]).start())
            _wait_rows(y_ref, o_ref, sem, rows)


def _moe_down(act, dst, item_expert, item_rows, num_items, w_down, b_down, n_out_rows, cap):
    d = w_down.shape[2]
    tn = min(DOWN_TN, d)
    n_items = dst.shape[0]
    n_j = d // tn
    item_map, act_rows, weights = _item_maps(n_j)
    grid_spec = pltpu.PrefetchScalarGridSpec(
        num_scalar_prefetch=3,
        grid=(n_items, n_j),
        in_specs=[pl.BlockSpec((1, 1, cap), item_map, memory_space=pltpu.SMEM),
                  pl.BlockSpec((cap, D_FF), act_rows),
                  pl.BlockSpec((None, D_FF, tn), weights),
                  pl.BlockSpec((None, 1, tn), weights)],
        out_specs=pl.BlockSpec(memory_space=pl.ANY),
        scratch_shapes=[pltpu.VMEM((cap, d), F32), pltpu.SemaphoreType.DMA(())],
    )
    return pl.pallas_call(
        _down_kernel,
        out_shape=jax.ShapeDtypeStruct((n_out_rows, d), F32),
        grid_spec=grid_spec,
        compiler_params=_params(("arbitrary", "arbitrary")),
        name="moe_down",
    )(item_expert, item_rows, num_items, dst, act, w_down, b_down.reshape(N_EXPERTS, 1, d))


COMBINE_TOKENS = 128


def _combine_kernel(prob_ref, x1_ref, g_ref, gate_ref, ys_ref, o_ref):
    prob = prob_ref[...]
    y = prob[:, 0:1] * ys_ref[0]
    for k in range(1, TOP_K):
        y = y + prob[:, k:k + 1] * ys_ref[k]
    o_ref[...] = x1_ref[...] + gate_ref[...] * _rms(y, g_ref[...])


def _moe_combine(ys, probs, x1, g_post, mod, gate_idx):
    s, d = x1.shape
    tc = min(COMBINE_TOKENS, s)
    return pl.pallas_call(
        _combine_kernel,
        out_shape=jax.ShapeDtypeStruct((s, d), F32),
        grid=(s // tc,),
        in_specs=[pl.BlockSpec((tc, LANES), lambda i: (i, 0)),
                  pl.BlockSpec((tc, d), lambda i: (i, 0)),
                  pl.BlockSpec((1, d), lambda i: (0, 0)),
                  pl.BlockSpec((1, d), lambda i: (0, gate_idx)),
                  pl.BlockSpec((TOP_K, tc, d), lambda i: (0, i, 0))],
        out_specs=pl.BlockSpec((tc, d), lambda i: (i, 0)),
        compiler_params=_params(("arbitrary",)),
        name="moe_combine",
    )(probs, x1, g_post.reshape(1, d), mod, ys.reshape(TOP_K, s, d))


def _moe(h2, idx, probs, rank, counts, x1, mod, g_post, w_gate_up, b_gate_up, w_down, b_down):
    s, d = h2.shape
    cap = MOE_CAP
    i32 = jnp.int32
    counts = counts.reshape(N_EXPERTS)
    items_per_expert = (counts + cap - 1) // cap
    item_end = jnp.cumsum(items_per_expert)
    first_item = (item_end - items_per_expert).astype(i32)
    n_items_max = (s * TOP_K + N_EXPERTS * (cap - 1)) // cap
    num_items = item_end[-1].astype(i32).reshape(1)
    item_ids = jnp.arange(n_items_max, dtype=i32)
    item_expert = jnp.minimum(jnp.sum(item_end[None, :] <= item_ids[:, None], axis=1), N_EXPERTS - 1).astype(i32)
    item_rows = jnp.clip(counts[item_expert] - cap * (item_ids - first_item[item_expert]), 0, cap)
    item_rows = jnp.where(item_ids < num_items[0], item_rows, 0).astype(i32)

    pos = (first_item[idx[:, :TOP_K]] * cap + rank[:, :TOP_K]).reshape(-1)
    tok = jnp.repeat(jnp.arange(s, dtype=i32), TOP_K)
    slot = jnp.tile(jnp.arange(TOP_K, dtype=i32), s)
    dst = jnp.zeros((n_items_max * cap,), i32).at[pos].set(slot * s + tok, unique_indices=True)
    dst = dst.reshape(n_items_max, 1, cap)

    act = _moe_gateup(h2, dst, item_expert, item_rows, num_items, w_gate_up, b_gate_up, cap)
    ys = _moe_down(act, dst, item_expert, item_rows, num_items, w_down, b_down, TOP_K * s, cap)
    return _moe_combine(ys, probs, x1, g_post, mod, 5)


def _layer(x, c, positions, w_mod, b_mod, g_pre_mix, g_post_mix, g_pre_ffn, g_post_ffn,
           w_in, b_in, sinks, w_out, b_out, w_router, b_router, w_gate_up, b_gate_up, w_down, b_down):
    mod = _modulation(c, w_mod, b_mod)
    h = _prenorm(x, g_pre_mix, mod, 1, 0)
    proj = _inproj(h, w_in, b_in)
    o_a = _swa_attention(proj, positions, sinks)
    o_f = _fox_attention(proj)
    y = _outproj(o_a, o_f, w_out, b_out)
    x1, h2, idx, probs, rank, counts = _postmix_router(x, y, mod, g_post_mix, g_pre_ffn, w_router, b_router)
    return _moe(h2, idx, probs, rank, counts, x1, mod, g_post_ffn, w_gate_up, b_gate_up, w_down, b_down)


def kernel(x, c, positions, w_mod, b_mod, g_pre_mix, g_post_mix, g_pre_ffn, g_post_ffn, w_in, b_in, sinks,
           w_out, b_out, w_router, b_router, w_gate_up, b_gate_up, w_down, b_down):
    batch, seq, d = x.shape
    assert batch == 1 and w_mod.shape[0] == 1, "one sequence, one layer"
    assert seq % FOX_BLK == 0 or seq < FOX_BLK
    out = _layer(x[0], c[0], positions[0], w_mod[0], b_mod[0], g_pre_mix[0], g_post_mix[0], g_pre_ffn[0],
                 g_post_ffn[0], w_in[0], b_in[0], sinks[0], w_out[0], b_out[0], w_router[0], b_router[0],
                 w_gate_up[0], b_gate_up[0], w_down[0], b_down[0])
    return out[None]
```

```python
import functools

import jax
import jax.numpy as jnp
from jax import lax
from jax.experimental import pallas as pl
from jax.experimental.pallas import tpu as pltpu

HEAD_DIM = 64
SWA_Q_HEADS = 32
SWA_KV_HEADS = 4
SWA_GROUP = SWA_Q_HEADS // SWA_KV_HEADS
WINDOW = 128
FOX_HEADS = 32
ROPE_THETA = 10000.0
N_EXPERTS = 32
TOP_K = 4
D_FF = 1536
SWIGLU_LIMIT = 7.0
SWIGLU_ALPHA = 1.702
RMS_EPS = 1e-6
N_MOD = 6

SWA_Q_W = SWA_Q_HEADS * HEAD_DIM
SWA_KV_W = SWA_KV_HEADS * HEAD_DIM
FOX_W = FOX_HEADS * HEAD_DIM
MIX_W = SWA_Q_W + FOX_W
IN_W = SWA_Q_W + 2 * SWA_KV_W + 3 * FOX_W + FOX_HEADS
OFF_QA = 0
OFF_KA = SWA_Q_W
OFF_VA = OFF_KA + SWA_KV_W
OFF_QF = OFF_VA + SWA_KV_W
OFF_KF = OFF_QF + FOX_W
OFF_VF = OFF_KF + FOX_W
OFF_FL = OFF_VF + FOX_W

LANES = 128
SUBLANES = 8
V7X_VMEM_BYTES = 64 * 1024 * 1024
VMEM_LIMIT = V7X_VMEM_BYTES - 8 * 1024 * 1024

NEG = -1e30
LOG2E = 1.4426950408889634

F32 = jnp.float32
BF16 = jnp.bfloat16


def _params(sem, vmem=VMEM_LIMIT):
    return pltpu.CompilerParams(dimension_semantics=sem, vmem_limit_bytes=vmem)


MOD_ROWS = 256


def _mod_kernel(c_ref, w_ref, b_ref, o_ref):
    d, tn = w_ref.shape

    def body(r, acc):
        rows = pl.ds(pl.multiple_of(r * MOD_ROWS, MOD_ROWS), MOD_ROWS)
        c = c_ref[rows, :]
        sc = c * jax.nn.sigmoid(c)
        prod = w_ref[rows, :] * sc
        return acc + jnp.sum(prod.reshape(MOD_ROWS // SUBLANES, SUBLANES, tn), axis=0)

    acc = lax.fori_loop(0, d // MOD_ROWS, body, jnp.zeros((SUBLANES, tn), F32))
    o_ref[...] = jnp.sum(acc, axis=0, keepdims=True) + b_ref[...]


def _modulation(c, w_mod, b_mod):
    d, n = w_mod.shape
    tn = min(1024, n)
    assert n % tn == 0 and d % MOD_ROWS == 0
    return pl.pallas_call(
        _mod_kernel,
        out_shape=jax.ShapeDtypeStruct((1, n), F32),
        grid=(n // tn,),
        in_specs=[pl.BlockSpec((d, 1), lambda j: (0, 0)),
                  pl.BlockSpec((d, tn), lambda j: (0, j)),
                  pl.BlockSpec((1, tn), lambda j: (0, j))],
        out_specs=pl.BlockSpec((1, tn), lambda j: (0, j)),
        compiler_params=_params(("arbitrary",)),
        name="modulation",
    )(c.reshape(d, 1), w_mod, b_mod.reshape(1, n))


def _rms(x, g):
    inv = lax.rsqrt(jnp.mean(x * x, axis=-1, keepdims=True) + RMS_EPS)
    return x * inv * g


def _prenorm_kernel(x_ref, g_ref, scale_ref, shift_ref, o_ref):
    h = _rms(x_ref[...], g_ref[...]) * (1.0 + scale_ref[...]) + shift_ref[...]
    o_ref[...] = h.astype(o_ref.dtype)


def _prenorm(x, g, mod, scale_idx, shift_idx):
    s, d = x.shape
    tm = min(256, s)
    return pl.pallas_call(
        _prenorm_kernel,
        out_shape=jax.ShapeDtypeStruct((s, d), BF16),
        grid=(s // tm,),
        in_specs=[pl.BlockSpec((tm, d), lambda i: (i, 0)),
                  pl.BlockSpec((1, d), lambda i: (0, 0)),
                  pl.BlockSpec((1, d), lambda i: (0, scale_idx)),
                  pl.BlockSpec((1, d), lambda i: (0, shift_idx))],
        out_specs=pl.BlockSpec((tm, d), lambda i: (i, 0)),
        compiler_params=_params(("arbitrary",)),
        name="prenorm",
    )(x, g.reshape(1, d), mod, mod)


def _inproj_kernel(a_ref, wt_ref, b_ref, o_ref):
    acc = lax.dot_general(a_ref[...], wt_ref[...].astype(BF16), (((1,), (1,)), ((), ())),
                          preferred_element_type=F32)
    o_ref[...] = acc + b_ref[...]


def _inproj(h, w, b):
    m, k = h.shape
    n = w.shape[1]
    tm = min(1024, m)
    tn = 512
    return pl.pallas_call(
        _inproj_kernel,
        out_shape=jax.ShapeDtypeStruct((m, n), F32),
        grid=(m // tm, pl.cdiv(n, tn)),
        in_specs=[pl.BlockSpec((tm, k), lambda i, j: (i, 0)),
                  pl.BlockSpec((tn, k), lambda i, j: (j, 0)),
                  pl.BlockSpec((1, tn), lambda i, j: (0, j))],
        out_specs=pl.BlockSpec((tm, tn), lambda i, j: (i, j)),
        compiler_params=_params(("arbitrary", "arbitrary")),
        name="in_projection",
    )(h, w.T, b.reshape(1, n))


def _outproj_kernel(a1_ref, a2_ref, w_ref, b_ref, o_ref):
    k1 = a1_ref.shape[1]
    w = w_ref[...].astype(BF16)
    acc = jnp.dot(a1_ref[...], w[:k1], preferred_element_type=F32)
    acc = acc + jnp.dot(a2_ref[...], w[k1:], preferred_element_type=F32)
    o_ref[...] = acc + b_ref[...]


def _outproj(o_a, o_f, w, b):
    m, k1 = o_a.shape
    k2 = o_f.shape[1]
    n = w.shape[1]
    tm = min(1024, m)
    tn = min(512, n)
    return pl.pallas_call(
        _outproj_kernel,
        out_shape=jax.ShapeDtypeStruct((m, n), F32),
        grid=(m // tm, n // tn),
        in_specs=[pl.BlockSpec((tm, k1), lambda i, j: (i, 0)),
                  pl.BlockSpec((tm, k2), lambda i, j: (i, 0)),
                  pl.BlockSpec((k1 + k2, tn), lambda i, j: (0, j)),
                  pl.BlockSpec((1, tn), lambda i, j: (0, j))],
        out_specs=pl.BlockSpec((tm, tn), lambda i, j: (i, j)),
        compiler_params=_params(("arbitrary", "arbitrary")),
        name="out_projection",
    )(o_a, o_f, w, b.reshape(1, n))


def _swap_halves(x, first_half):
    return jnp.where(first_half, pltpu.roll(x, LANES - HEAD_DIM // 2, 1),
                     pltpu.roll(x, HEAD_DIM // 2, 1))


ROPE_ROWS = 512


def _rope_table_kernel(pos_ref, invf_ref, cos_ref, sin_ref):
    lane = lax.broadcasted_iota(jnp.int32, cos_ref.shape, 1)
    first_half = (lane % HEAD_DIM) < (HEAD_DIM // 2)
    ang = pos_ref[...].astype(F32) * invf_ref[...]
    sin = jnp.sin(ang)
    cos_ref[...] = jnp.cos(ang)
    sin_ref[...] = jnp.where(first_half, -sin, sin)


def _rope_tables(positions):
    s = positions.shape[0]
    tr = min(ROPE_ROWS, s)
    half = HEAD_DIM // 2
    inv_freq = jnp.power(ROPE_THETA, -jnp.arange(half, dtype=F32) * (2.0 / HEAD_DIM))
    invf = jnp.tile(inv_freq, LANES // half).reshape(1, LANES)
    table = jax.ShapeDtypeStruct((s, LANES), F32)
    return pl.pallas_call(
        _rope_table_kernel,
        out_shape=(table, table),
        grid=(s // tr,),
        in_specs=[pl.BlockSpec((tr, 1), lambda i: (i, 0)), pl.BlockSpec((1, LANES), lambda i: (0, 0))],
        out_specs=(pl.BlockSpec((tr, LANES), lambda i: (i, 0)), pl.BlockSpec((tr, LANES), lambda i: (i, 0))),
        compiler_params=_params(("arbitrary",)),
        name="rope_tables",
    )(positions.reshape(s, 1), invf)


def _swa_kernel(sinks_ref, cosc_ref, sinc_ref, cosp_ref, sinp_ref, q_ref, kp_ref, kc_ref, vp_ref, vc_ref, o_ref):
    n = pl.program_id(0)
    w = WINDOW
    lane = lax.broadcasted_iota(jnp.int32, (w, LANES), 1)
    first_half = (lane % HEAD_DIM) < (HEAD_DIM // 2)
    lane2 = lax.broadcasted_iota(jnp.int32, (2 * w, LANES), 1)
    lo64_2 = lane2 < HEAD_DIM

    cos_c, sin_c = cosc_ref[...], sinc_ref[...]
    cos_p, sin_p = cosp_ref[...], sinp_ref[...]

    def rope(t, cos, sin):
        return t * cos + _swap_halves(t, first_half) * sin

    stacked = (SWA_GROUP // 2) * w
    qi = lax.rem(lax.broadcasted_iota(jnp.int32, (stacked, 2 * w), 0), w)
    kj = lax.broadcasted_iota(jnp.int32, (stacked, 2 * w), 1)
    diff = qi + w - kj
    valid4 = (diff >= 0) & (diff < w) & ((n * w - w + kj) >= 0)

    scale = HEAD_DIM ** -0.5
    nt_dims = (((1,), (1,)), ((), ()))
    for kvp in range(SWA_KV_HEADS // 2):
        cols = slice(kvp * LANES, (kvp + 1) * LANES)
        k2 = jnp.concatenate([rope(kp_ref[:, cols], cos_p, sin_p),
                              rope(kc_ref[:, cols], cos_c, sin_c)], axis=0)
        v2 = jnp.concatenate([vp_ref[:, cols], vc_ref[:, cols]], axis=0)
        k2r = pltpu.roll(k2, HEAD_DIM, 1)
        v2r = pltpu.roll(v2, HEAD_DIM, 1)
        for sub in range(2):
            hkv = 2 * kvp + sub
            src_k, alt_k = (k2, k2r) if sub == 0 else (k2r, k2)
            src_v, alt_v = (v2, v2r) if sub == 0 else (v2r, v2)
            ka = jnp.where(lo64_2, src_k, 0.0).astype(BF16)
            kb = jnp.where(lo64_2, 0.0, alt_k).astype(BF16)
            va = jnp.where(lo64_2, src_v, 0.0).astype(BF16)
            vb = jnp.where(lo64_2, 0.0, alt_v).astype(BF16)
            pairs = [hkv * (SWA_GROUP // 2) + gp for gp in range(SWA_GROUP // 2)]
            q = jnp.concatenate([(rope(q_ref[:, pr * LANES:(pr + 1) * LANES], cos_c, sin_c) * scale).astype(BF16)
                                 for pr in pairs], axis=0)
            scores = [lax.dot_general(q, kk, nt_dims, preferred_element_type=F32) for kk in (ka, kb)]
            o = jnp.zeros((len(pairs) * w, LANES), F32)
            for which, (s, vv) in enumerate(zip(scores, (va, vb))):
                sink = jnp.concatenate([jnp.full((w, 1), sinks_ref[2 * pr + which], F32) for pr in pairs], axis=0)
                s = jnp.where(valid4, s, NEG)
                m = jnp.maximum(jnp.max(s, axis=-1, keepdims=True), sink)
                p = jnp.exp(s - m)
                den = jnp.sum(p, axis=-1, keepdims=True) + jnp.exp(sink - m)
                pv = jnp.dot(p.astype(BF16), vv, preferred_element_type=F32)
                o = o + pv * (1.0 / den)
            for r, pr in enumerate(pairs):
                o_ref[:, pr * LANES:(pr + 1) * LANES] = o[r * w:(r + 1) * w].astype(o_ref.dtype)


def _swa_attention(proj, positions, sinks):
    s = proj.shape[0]
    w = WINDOW
    nb = s // w
    cos, sin = _rope_tables(positions)
    prev = lambda n: jnp.maximum(n - 1, 0)
    ka_blk = OFF_KA // SWA_KV_W
    va_blk = OFF_VA // SWA_KV_W
    cur_tab = pl.BlockSpec((w, LANES), lambda n: (n, 0))
    prev_tab = pl.BlockSpec((w, LANES), lambda n: (prev(n), 0))
    return pl.pallas_call(
        _swa_kernel,
        out_shape=jax.ShapeDtypeStruct((s, SWA_Q_W), BF16),
        grid=(nb,),
        in_specs=[pl.BlockSpec(memory_space=pltpu.SMEM),
                  cur_tab, cur_tab, prev_tab, prev_tab,
                  pl.BlockSpec((w, SWA_Q_W), lambda n: (n, 0)),
                  pl.BlockSpec((w, SWA_KV_W), lambda n: (prev(n), ka_blk)),
                  pl.BlockSpec((w, SWA_KV_W), lambda n: (n, ka_blk)),
                  pl.BlockSpec((w, SWA_KV_W), lambda n: (prev(n), va_blk)),
                  pl.BlockSpec((w, SWA_KV_W), lambda n: (n, va_blk))],
        out_specs=pl.BlockSpec((w, SWA_Q_W), lambda n: (n, 0)),
        compiler_params=_params(("arbitrary",)),
        name="swa_sink_attention",
    )(sinks, cos, sin, cos, sin, proj, proj, proj, proj, proj)


CUM_ROWS = 256


def _cum_kernel(f_ref, o_ref, carry_ref):
    @pl.when(pl.program_id(0) == 0)
    def _():
        carry_ref[...] = jnp.zeros_like(carry_ref)

    f = f_ref[:, :FOX_HEADS]
    ls = jnp.minimum(f, 0.0) - jnp.log1p(jnp.exp(-jnp.abs(f)))
    r = lax.broadcasted_iota(jnp.int32, (CUM_ROWS, CUM_ROWS), 0)
    c = lax.broadcasted_iota(jnp.int32, (CUM_ROWS, CUM_ROWS), 1)
    tri = (c <= r).astype(F32)
    cum = jnp.dot(tri, ls, preferred_element_type=F32, precision=lax.Precision.HIGHEST) + carry_ref[...]
    o_ref[...] = cum
    carry_ref[...] = cum[CUM_ROWS - 1:CUM_ROWS, :]


def _fox_cum(proj):
    s = proj.shape[0]
    return pl.pallas_call(
        _cum_kernel,
        out_shape=jax.ShapeDtypeStruct((s, FOX_HEADS), F32),
        grid=(s // CUM_ROWS,),
        in_specs=[pl.BlockSpec((CUM_ROWS, LANES), lambda i: (i, OFF_FL // LANES))],
        out_specs=pl.BlockSpec((CUM_ROWS, FOX_HEADS), lambda i: (i, 0)),
        scratch_shapes=[pltpu.VMEM((1, FOX_HEADS), F32)],
        compiler_params=_params(("arbitrary",)),
        name="fox_decay_cumsum",
    )(proj)


FOX_BLK = 512
AUG_CQ = HEAD_DIM
AUG_ONE = HEAD_DIM + 3
AUG_END = HEAD_DIM + 6
PREP_HEADS = 8
SKIP_MARGIN = 170.0
NORM_SLACK = 1.001
BF16_ROUND_UP = 1.0 + 2.0 ** -7
assert OFF_QF % (PREP_HEADS * HEAD_DIM) == 0 and OFF_KF % (PREP_HEADS * HEAD_DIM) == 0
assert OFF_VF % (PREP_HEADS * HEAD_DIM) == 0


def _split3(c):
    hi = c.astype(BF16).astype(F32)
    r = c - hi
    mid = r.astype(BF16).astype(F32)
    lo = (r - mid).astype(BF16).astype(F32)
    return hi, mid, lo


def _fox_prep_kernel(q_ref, k_ref, v_ref, cum_ref, qo_ref, ko_ref, vo_ref, qn_ref, kn_ref):
    j = pl.program_id(0)
    tb = q_ref.shape[0]
    lane = lax.broadcasted_iota(jnp.int32, (tb, LANES), 1)
    lo64 = lane < HEAD_DIM
    lane_h = lax.broadcasted_iota(jnp.int32, (tb, FOX_HEADS), 1)
    cum = cum_ref[...]
    scale = HEAD_DIM ** -0.5 * LOG2E
    for local in range(PREP_HEADS):
        pair, hh = divmod(local, 2)
        h = PREP_HEADS * j + local
        col = jnp.sum(jnp.where(lane_h == h, cum, 0.0), axis=-1, keepdims=True)
        hi, mid, lo = _split3((col - col[0:1, :]) * LOG2E)
        cols = slice(pair * LANES, (pair + 1) * LANES)
        q = q_ref[:, cols] * scale
        k = k_ref[:, cols]
        v = v_ref[:, cols]
        if hh == 1:
            q = pltpu.roll(q, HEAD_DIM, 1)
            k = pltpu.roll(k, HEAD_DIM, 1)
            v = pltpu.roll(v, HEAD_DIM, 1)
        q_aug = jnp.where(lane == AUG_CQ, hi, jnp.where(lane == AUG_CQ + 1, mid,
                jnp.where(lane == AUG_CQ + 2, lo, jnp.where(lane < AUG_END, 1.0, 0.0))))
        k_aug = jnp.where(lane < AUG_ONE, 1.0, jnp.where(lane == AUG_ONE, -hi,
                jnp.where(lane == AUG_ONE + 1, -mid, jnp.where(lane == AUG_ONE + 2, -lo, 0.0))))
        v_aug = jnp.where(lane == HEAD_DIM, 1.0, 0.0)
        qb = jnp.where(lo64, q, q_aug).astype(BF16)
        kb = jnp.where(lo64, k, k_aug).astype(BF16)
        qo_ref[local] = qb
        ko_ref[local] = kb
        vo_ref[local, 0] = jnp.where(lo64, v, v_aug).T.astype(BF16)
        ones = jnp.ones((LANES, LANES), BF16)
        for src, norm_ref in ((qb, qn_ref), (kb, kn_ref)):
            x = jnp.where(lo64, src.astype(F32), 0.0)
            sq = jnp.dot((x * x * BF16_ROUND_UP).astype(BF16), ones, preferred_element_type=F32)
            norm_ref[local, 0] = jnp.broadcast_to(jnp.sqrt(jnp.max(sq, axis=0, keepdims=True)),
                                                  norm_ref.shape[2:])


def _fox_prep(proj, cum, tb):
    s = proj.shape[0]
    width = PREP_HEADS * HEAD_DIM
    shp = jax.ShapeDtypeStruct((FOX_HEADS, s, LANES), BF16)
    ospec = pl.BlockSpec((PREP_HEADS, tb, LANES), lambda j, i: (j, i, 0))
    norms = jax.ShapeDtypeStruct((FOX_HEADS, s // tb, SUBLANES, LANES), F32)
    nspec = pl.BlockSpec((PREP_HEADS, 1, SUBLANES, LANES), lambda j, i: (j, i, 0, 0))
    return pl.pallas_call(
        _fox_prep_kernel,
        out_shape=(shp, shp, jax.ShapeDtypeStruct((FOX_HEADS, s // tb, LANES, tb), BF16), norms, norms),
        grid=(FOX_HEADS // PREP_HEADS, s // tb),
        in_specs=[pl.BlockSpec((tb, width), lambda j, i: (i, OFF_QF // width + j)),
                  pl.BlockSpec((tb, width), lambda j, i: (i, OFF_KF // width + j)),
                  pl.BlockSpec((tb, width), lambda j, i: (i, OFF_VF // width + j)),
                  pl.BlockSpec((tb, FOX_HEADS), lambda j, i: (i, 0))],
        out_specs=(ospec, ospec, pl.BlockSpec((PREP_HEADS, 1, LANES, tb), lambda j, i: (j, i, 0, 0)),
                   nspec, nspec),
        compiler_params=_params(("arbitrary", "arbitrary")),
        name="fox_prepare",
    )(proj, proj, proj, cum)


def _fox_kernel(a_ref, e_ref, qk_ref, q_ref, k_ref, vt_ref, o_ref, acc_ref, m_ref, s0_ref):
    j = pl.program_id(0)
    i = pl.program_id(1)
    tb = q_ref.shape[1]
    nt_dims = (((1,), (1,)), ((), ()))
    key = lax.broadcasted_iota(jnp.int32, (tb, tb), 0)
    qry = lax.broadcasted_iota(jnp.int32, (tb, tb), 1)
    causal = key <= qry

    def scores(hh, kt, diagonal=False):
        rows = pl.ds(pl.multiple_of(kt * tb, tb), tb)
        s = lax.dot_general(k_ref[hh, rows, :], q_ref[hh], nt_dims, preferred_element_type=F32)
        return jnp.where(causal, s, NEG) if diagonal else s

    def update(hh, kt, s):
        h = 2 * j + hh
        delta = (a_ref[h, i] - a_ref[h, kt]) * LOG2E
        m_old = m_ref[hh]
        m_new = jnp.maximum(m_old, jnp.max(s, axis=0, keepdims=True) + delta)
        p = jnp.exp2(s - (m_new - delta))
        alpha = jnp.exp2(m_old - m_new)
        acc_ref[hh] = alpha * acc_ref[hh] + jnp.dot(vt_ref[hh, kt], p.astype(BF16), preferred_element_type=F32)
        m_ref[hh] = m_new

    def step(kt, diagonal=False):
        s1 = scores(1, kt, diagonal)
        update(0, kt, s0_ref[...])
        s0_ref[...] = scores(0, jnp.maximum(kt - 1, 0))
        update(1, kt, s1)

    def tiles_needed(hh):
        h = 2 * j + hh
        floor = jnp.min(m_ref[hh]) - SKIP_MARGIN
        top = qk_ref[h, i] + a_ref[h, i] * LOG2E

        def first_kept(kt, lo):
            kept = top - e_ref[h, kt] * LOG2E >= floor
            return jnp.where(kept, jnp.minimum(lo, kt), lo)

        return i - lax.fori_loop(0, i, first_kept, i)

    m_ref[...] = jnp.full(m_ref.shape, NEG, F32)
    acc_ref[...] = jnp.zeros(acc_ref.shape, F32)
    s0_ref[...] = scores(0, i, diagonal=True)
    step(i, diagonal=True)
    n_back = jnp.maximum(tiles_needed(0), tiles_needed(1))
    lax.fori_loop(0, n_back, lambda t, carry: (step(i - 1 - t), carry)[1], 0)
    acc0 = acc_ref[0]
    acc1 = acc_ref[1]
    o0 = (acc0 * (1.0 / acc0[HEAD_DIM:HEAD_DIM + 1, :])).T
    o1 = (acc1 * (1.0 / acc1[HEAD_DIM:HEAD_DIM + 1, :])).T
    lo64 = lax.broadcasted_iota(jnp.int32, (tb, LANES), 1) < HEAD_DIM
    o_ref[...] = jnp.where(lo64, o0, pltpu.roll(o1, HEAD_DIM, 1)).astype(o_ref.dtype)


def _fox_attention(proj):
    s = proj.shape[0]
    tb = min(FOX_BLK, s)
    cum = _fox_cum(proj)
    q_aug, k_aug, vt_aug, q_norm, k_norm = _fox_prep(proj, cum, tb)
    block_first = cum[::tb].T
    block_last = cum[tb - 1::tb].T
    qk_bound = q_norm[:, :, 0, 0] * jnp.max(k_norm[:, :, 0, 0], axis=1, keepdims=True) * NORM_SLACK
    npairs = FOX_HEADS // 2
    smem = pl.BlockSpec(memory_space=pltpu.SMEM)
    return pl.pallas_call(
        _fox_kernel,
        out_shape=jax.ShapeDtypeStruct((s, FOX_W), BF16),
        grid=(npairs, s // tb),
        in_specs=[smem, smem, smem,
                  pl.BlockSpec((2, tb, LANES), lambda j, i: (j, i, 0)),
                  pl.BlockSpec((2, s, LANES), lambda j, i: (j, 0, 0)),
                  pl.BlockSpec((2, s // tb, LANES, tb), lambda j, i: (j, 0, 0, 0))],
        out_specs=pl.BlockSpec((tb, LANES), lambda j, i: (i, j)),
        scratch_shapes=[pltpu.VMEM((2, LANES, tb), F32), pltpu.VMEM((2, 1, tb), F32), pltpu.VMEM((tb, tb), F32)],
        compiler_params=_params(("arbitrary", "arbitrary")),
        name="fox_attention",
    )(block_first, block_last, qk_bound, q_aug, k_aug, vt_aug)


def _lane_pack(cols, width, dtype):
    rows = cols[0].shape[0]
    lane = lax.broadcasted_iota(jnp.int32, (rows, width), 1)
    out = jnp.zeros((rows, width), dtype)
    for k, cval in enumerate(cols):
        out = jnp.where(lane == k, cval.astype(dtype), out)
    return out


def _postmix_kernel(x_ref, y_ref, gpost_ref, gate_ref, gpre_ref, scale_ref, shift_ref, wr_ref, br_ref,
                    x1_ref, h2_ref, idx_ref, prob_ref, rank_ref, cnt_ref, carry_ref):
    step = pl.program_id(0)

    @pl.when(step == 0)
    def _():
        carry_ref[...] = jnp.zeros_like(carry_ref)

    x1 = x_ref[...] + gate_ref[...] * _rms(y_ref[...], gpost_ref[...])
    x1_ref[...] = x1
    h2 = _rms(x1, gpre_ref[...]) * (1.0 + scale_ref[...]) + shift_ref[...]
    h2_ref[...] = h2
    logits = jnp.dot(h2, wr_ref[...], preferred_element_type=F32,
                     precision=lax.Precision.HIGHEST) + br_ref[...]
    tm = logits.shape[0]
    lane_e = lax.broadcasted_iota(jnp.int32, (tm, N_EXPERTS), 1).astype(F32)
    vals, idxs, sels = [], [], []
    cur = logits
    for _ in range(TOP_K):
        mk = jnp.max(cur, axis=-1, keepdims=True)
        ik = jnp.min(jnp.where(cur == mk, lane_e, float(N_EXPERTS)), axis=-1, keepdims=True)
        sel = lane_e == ik
        vals.append(mk)
        idxs.append(ik)
        sels.append(sel)
        cur = jnp.where(sel, -jnp.inf, cur)
    exps = [jnp.exp(v - vals[0]) for v in vals]
    inv = 1.0 / functools.reduce(lambda a, b: a + b, exps)
    probs = [e * inv for e in exps]

    onehot = functools.reduce(lambda a, b: a | b, sels).astype(F32)
    r = lax.broadcasted_iota(jnp.int32, (tm, tm), 0)
    c = lax.broadcasted_iota(jnp.int32, (tm, tm), 1)
    strict = (c < r).astype(BF16)
    before = jnp.dot(strict, onehot.astype(BF16), preferred_element_type=F32) + carry_ref[...]
    ranks = [jnp.sum(jnp.where(sel, before, 0.0), axis=-1, keepdims=True) for sel in sels]
    carry_ref[...] = carry_ref[...] + jnp.sum(onehot, axis=0, keepdims=True)

    idx_ref[...] = _lane_pack(idxs, LANES, jnp.int32)
    prob_ref[...] = _lane_pack(probs, LANES, F32)
    rank_ref[...] = _lane_pack([rk.astype(jnp.int32) for rk in ranks], LANES, jnp.int32)
    cnt_ref[...] = carry_ref[...].astype(jnp.int32)


def _postmix_router(x, y, mod, g_post, g_pre, w_router, b_router):
    s, d = x.shape
    tm = min(256, s)
    row = pl.BlockSpec((tm, d), lambda i: (i, 0))
    vec = lambda idx: pl.BlockSpec((1, d), lambda i: (0, idx))
    lanes = pl.BlockSpec((tm, LANES), lambda i: (i, 0))
    return pl.pallas_call(
        _postmix_kernel,
        out_shape=(jax.ShapeDtypeStruct((s, d), F32), jax.ShapeDtypeStruct((s, d), F32),
                   jax.ShapeDtypeStruct((s, LANES), jnp.int32), jax.ShapeDtypeStruct((s, LANES), F32),
                   jax.ShapeDtypeStruct((s, LANES), jnp.int32),
                   jax.ShapeDtypeStruct((1, N_EXPERTS), jnp.int32)),
        grid=(s // tm,),
        in_specs=[row, row, vec(0), vec(2), vec(0), vec(4), vec(3),
                  pl.BlockSpec((d, N_EXPERTS), lambda i: (0, 0)),
                  pl.BlockSpec((1, N_EXPERTS), lambda i: (0, 0))],
        out_specs=(row, row, lanes, lanes, lanes, pl.BlockSpec((1, N_EXPERTS), lambda i: (0, 0))),
        scratch_shapes=[pltpu.VMEM((1, N_EXPERTS), F32)],
        compiler_params=_params(("arbitrary",)),
        name="postmix_router",
    )(x, y, g_post.reshape(1, d), mod, g_pre.reshape(1, d), mod, mod, w_router, b_router.reshape(1, N_EXPERTS))


MOE_CAP = 1280
MOE_UNIT = 128
assert MOE_CAP % MOE_UNIT == 0
GU_TN = 512
DEINT = 256
DOWN_TN = 1024


def _item_maps(n_j):
    def clamp(i, j, ni):
        used = i < ni[0]
        return jnp.where(used, i, ni[0] - 1), jnp.where(used, j, n_j - 1)

    def item_rows(i, j, ie, ir, ni):
        return clamp(i, j, ni)[0], 0, 0

    def act_rows(i, j, ie, ir, ni):
        return clamp(i, j, ni)[0], 0

    def weights(i, j, ie, ir, ni):
        ii, jj = clamp(i, j, ni)
        return ie[ii], 0, jj

    return item_rows, act_rows, weights


def _dot_mixed(a_bf16, w_f32):
    return lax.dot_general(a_bf16, w_f32, (((1,), (0,)), ((), ())), preferred_element_type=F32)


def _for_rows(n, body, unroll=SUBLANES):
    def trip(t, carry):
        for u in range(unroll):
            body(t * unroll + u, u)
        return carry

    lax.fori_loop(0, n // unroll, trip, 0)
    lax.fori_loop((n // unroll) * unroll, n, lambda r, carry: (body(r, 0), carry)[1], 0)


def _wait_rows(src, dst, sem, n):
    n8 = pl.multiple_of((n // SUBLANES) * SUBLANES, SUBLANES)

    @pl.when(n8 > 0)
    def _():
        pltpu.make_async_copy(src.at[pl.ds(0, n8)], dst.at[pl.ds(0, n8)], sem).wait()

    one = pltpu.make_async_copy(src.at[pl.ds(0, 1)], dst.at[pl.ds(0, 1)], sem)
    lax.fori_loop(n8, n, lambda r, carry: (one.wait(), carry)[1], 0)


def _units(rows):
    return (rows + MOE_UNIT - 1) // MOE_UNIT


def _unit_rows(start_unit, n_units):
    return pl.ds(pl.multiple_of(start_unit * MOE_UNIT, MOE_UNIT), n_units * MOE_UNIT)


def _for_blocks(rows, compute, finish):
    units = _units(rows)
    triple = jnp.logical_and(units % 2 == 1, units >= 3)
    single = units == 1
    n2 = (units - jnp.where(triple, 3, 0) - jnp.where(single, 1, 0)) // 2

    def run(slices):
        zs = [compute(rs) for rs in slices]
        for rs, z in zip(slices, zs):
            finish(rs, z)

    def pair(pb, carry):
        run([_unit_rows(4 * pb, 2), _unit_rows(4 * pb + 2, 2)])
        return carry

    lax.fori_loop(0, n2 // 2, pair, 0)

    @pl.when(n2 % 2 == 1)
    def _():
        run([_unit_rows(2 * (n2 - 1), 2)])

    @pl.when(triple)
    def _():
        run([_unit_rows(2 * n2, 3)])

    @pl.when(single)
    def _():
        run([_unit_rows(0, 1)])


def _gateup_kernel(ie_ref, ir_ref, ni_ref, dst_ref, h_ref, w_ref, b_ref, o_ref, stg_ref, xb_ref, sem):
    i = pl.program_id(0)
    j = pl.program_id(1)
    cap = o_ref.shape[0]

    @pl.when(i >= ni_ref[0])
    def _():
        o_ref[...] = jnp.zeros_like(o_ref)

    @pl.when(i < ni_ref[0])
    def _():
        rows = ir_ref[i]

        def row_copy(r):
            token = lax.rem(dst_ref[0, 0, r].astype(jnp.uint32), jnp.uint32(h_ref.shape[0])).astype(jnp.int32)
            return pltpu.make_async_copy(h_ref.at[pl.ds(token, 1)], stg_ref.at[pl.ds(r, 1)], sem)

        @pl.when(j == 0)
        def _():
            @pl.when(i == 0)
            def _():
                stg_ref[...] = jnp.zeros_like(stg_ref)

            _for_rows(rows, lambda r, u: row_copy(r).start(priority=u % 2))
            _wait_rows(h_ref, stg_ref, sem, rows)

            def to_bf16(u, carry):
                xb_ref[_unit_rows(u, 1), :] = stg_ref[_unit_rows(u, 1), :].astype(BF16)
                return carry

            lax.fori_loop(0, _units(rows), to_bf16, 0)

        r = lax.broadcasted_iota(jnp.int32, (DEINT, DEINT), 0)
        c = lax.broadcasted_iota(jnp.int32, (DEINT, DEINT), 1)
        half = DEINT // 2
        perm = (r == jnp.where(c < half, 2 * c, 2 * (c - half) + 1)).astype(BF16)

        def gate_up(rs):
            return _dot_mixed(xb_ref[rs, :], w_ref[...]) + b_ref[...]

        def activate(rs, z):
            for t in range(GU_TN // DEINT):
                zz = jnp.dot(z[:, t * DEINT:(t + 1) * DEINT].astype(BF16), perm, preferred_element_type=F32)
                glu = jnp.minimum(zz[:, :half], SWIGLU_LIMIT)
                lin = jnp.clip(zz[:, half:], -SWIGLU_LIMIT, SWIGLU_LIMIT)
                act = glu * jax.nn.sigmoid(SWIGLU_ALPHA * glu) * (lin + 1.0)
                o_ref[rs, t * half:(t + 1) * half] = act.astype(o_ref.dtype)

        _for_blocks(rows, gate_up, activate)

        def zero_unit(u, carry):
            o_ref[_unit_rows(u, 1), :] = jnp.zeros((MOE_UNIT, o_ref.shape[1]), o_ref.dtype)
            return carry

        lax.fori_loop(_units(rows), cap // MOE_UNIT, zero_unit, 0)


def _moe_gateup(h2, dst, item_expert, item_rows, num_items, w_gate_up, b_gate_up, cap):
    s, d = h2.shape
    n_items = dst.shape[0]
    n_j = (2 * D_FF) // GU_TN
    item_map, _, weights = _item_maps(n_j)
    grid_spec = pltpu.PrefetchScalarGridSpec(
        num_scalar_prefetch=3,
        grid=(n_items, n_j),
        in_specs=[pl.BlockSpec((1, 1, cap), item_map, memory_space=pltpu.SMEM),
                  pl.BlockSpec(memory_space=pl.ANY),
                  pl.BlockSpec((None, d, GU_TN), weights),
                  pl.BlockSpec((None, 1, GU_TN), weights)],
        out_specs=pl.BlockSpec((cap, GU_TN // 2), lambda i, j, ie, ir, ni: (i, j)),
        scratch_shapes=[pltpu.VMEM((cap, d), F32), pltpu.VMEM((cap, d), BF16), pltpu.SemaphoreType.DMA(())],
    )
    return pl.pallas_call(
        _gateup_kernel,
        out_shape=jax.ShapeDtypeStruct((n_items * cap, D_FF), BF16),
        grid_spec=grid_spec,
        compiler_params=_params(("arbitrary", "arbitrary")),
        name="moe_gate_up",
    )(item_expert, item_rows, num_items, dst, h2, w_gate_up, b_gate_up.reshape(N_EXPERTS, 1, 2 * D_FF))


def _down_kernel(ie_ref, ir_ref, ni_ref, dst_ref, a_ref, w_ref, b_ref, o_ref, y_ref, sem):
    i = pl.program_id(0)
    j = pl.program_id(1)
    tn = w_ref.shape[1]
    n_j = y_ref.shape[1] // tn

    @pl.when(i < ni_ref[0])
    def _():
        rows = ir_ref[i]

        def down(rs):
            return _dot_mixed(a_ref[rs, :], w_ref[...]) + b_ref[...]

        for jj in range(n_j):
            @pl.when(j == jj)
            def _(jj=jj):
                def keep(rs, y):
                    y_ref[rs, jj * tn:(jj + 1) * tn] = y

                _for_blocks(rows, down, keep)

        @pl.when(j == n_j - 1)
        def _():
            def row_copy(r, dst_row):
                return pltpu.make_async_copy(y_ref.at[pl.ds(r, 1)], o_ref.at[pl.ds(dst_row, 1)], sem)

            _for_rows(rows, lambda r, u: row_copy(r, dst_ref[0, 0, r]).start(priority=u % 2))
            _wait_rows(y_ref, o_ref, sem, rows)


def _moe_down(act, dst, item_expert, item_rows, num_items, w_down, b_down, n_out_rows, cap):
    d = w_down.shape[2]
    tn = min(DOWN_TN, d)
    n_items = dst.shape[0]
    n_j = d // tn
    item_map, act_rows, weights = _item_maps(n_j)
    grid_spec = pltpu.PrefetchScalarGridSpec(
        num_scalar_prefetch=3,
        grid=(n_items, n_j),
        in_specs=[pl.BlockSpec((1, 1, cap), item_map, memory_space=pltpu.SMEM),
                  pl.BlockSpec((cap, D_FF), act_rows),
                  pl.BlockSpec((None, D_FF, tn), weights),
                  pl.BlockSpec((None, 1, tn), weights)],
        out_specs=pl.BlockSpec(memory_space=pl.ANY),
        scratch_shapes=[pltpu.VMEM((cap, d), F32), pltpu.SemaphoreType.DMA(())],
    )
    return pl.pallas_call(
        _down_kernel,
        out_shape=jax.ShapeDtypeStruct((n_out_rows, d), F32),
        grid_spec=grid_spec,
        compiler_params=_params(("arbitrary", "arbitrary")),
        name="moe_down",
    )(item_expert, item_rows, num_items, dst, act, w_down, b_down.reshape(N_EXPERTS, 1, d))


COMBINE_TOKENS = 128


def _combine_kernel(prob_ref, x1_ref, g_ref, gate_ref, ys_ref, o_ref):
    prob = prob_ref[...]
    y = prob[:, 0:1] * ys_ref[0]
    for k in range(1, TOP_K):
        y = y + prob[:, k:k + 1] * ys_ref[k]
    o_ref[...] = x1_ref[...] + gate_ref[...] * _rms(y, g_ref[...])


def _moe_combine(ys, probs, x1, g_post, mod, gate_idx):
    s, d = x1.shape
    tc = min(COMBINE_TOKENS, s)
    return pl.pallas_call(
        _combine_kernel,
        out_shape=jax.ShapeDtypeStruct((s, d), F32),
        grid=(s // tc,),
        in_specs=[pl.BlockSpec((tc, LANES), lambda i: (i, 0)),
                  pl.BlockSpec((tc, d), lambda i: (i, 0)),
                  pl.BlockSpec((1, d), lambda i: (0, 0)),
                  pl.BlockSpec((1, d), lambda i: (0, gate_idx)),
                  pl.BlockSpec((TOP_K, tc, d), lambda i: (0, i, 0))],
        out_specs=pl.BlockSpec((tc, d), lambda i: (i, 0)),
        compiler_params=_params(("arbitrary",)),
        name="moe_combine",
    )(probs, x1, g_post.reshape(1, d), mod, ys.reshape(TOP_K, s, d))


def _moe(h2, idx, probs, rank, counts, x1, mod, g_post, w_gate_up, b_gate_up, w_down, b_down):
    s, d = x1.shape
    cap = MOE_CAP
    i32 = jnp.int32
    counts = counts.reshape(N_EXPERTS)
    items_per_expert = (counts + cap - 1) // cap
    item_end = jnp.cumsum(items_per_expert)
    first_item = (item_end - items_per_expert).astype(i32)
    n_items_max = (s * TOP_K + N_EXPERTS * (cap - 1)) // cap
    num_items = item_end[-1].astype(i32).reshape(1)
    item_ids = jnp.arange(n_items_max, dtype=i32)
    item_expert = jnp.minimum(jnp.sum(item_end[None, :] <= item_ids[:, None], axis=1), N_EXPERTS - 1).astype(i32)
    item_rows = jnp.clip(counts[item_expert] - cap * (item_ids - first_item[item_expert]), 0, cap)
    item_rows = jnp.where(item_ids < num_items[0], item_rows, 0).astype(i32)

    pos = (first_item[idx[:, :TOP_K]] * cap + rank[:, :TOP_K]).reshape(-1)
    tok = jnp.repeat(jnp.arange(s, dtype=i32), TOP_K)
    slot = jnp.tile(jnp.arange(TOP_K, dtype=i32), s)
    dst = jnp.zeros((n_items_max * cap,), i32).at[pos].set(slot * s + tok, unique_indices=True)
    dst = dst.reshape(n_items_max, 1, cap)

    act = _moe_gateup(h2, dst, item_expert, item_rows, num_items, w_gate_up, b_gate_up, cap)
    ys = _moe_down(act, dst, item_expert, item_rows, num_items, w_down, b_down, TOP_K * s, cap)
    return _moe_combine(ys, probs, x1, g_post, mod, 5)


def _layer(x, c, positions, w_mod, b_mod, g_pre_mix, g_post_mix, g_pre_ffn, g_post_ffn,
           w_in, b_in, sinks, w_out, b_out, w_router, b_router, w_gate_up, b_gate_up, w_down, b_down):
    mod = _modulation(c, w_mod, b_mod)
    h = _prenorm(x, g_pre_mix, mod, 1, 0)
    proj = _inproj(h, w_in, b_in)
    o_a = _swa_attention(proj, positions, sinks)
    o_f = _fox_attention(proj)
    y = _outproj(o_a, o_f, w_out, b_out)
    x1, h2, idx, probs, rank, counts = _postmix_router(x, y, mod, g_post_mix, g_pre_ffn, w_router, b_router)
    return _moe(h2, idx, probs, rank, counts, x1, mod, g_post_ffn, w_gate_up, b_gate_up, w_down, b_down)


def kernel(x, c, positions, w_mod, b_mod, g_pre_mix, g_post_mix, g_pre_ffn, g_post_ffn, w_in, b_in, sinks,
           w_out, b_out, w_router, b_router, w_gate_up, b_gate_up, w_down, b_down):
    batch, seq, d = x.shape
    assert batch == 1 and w_mod.shape[0] == 1, "one sequence, one layer"
    assert seq % FOX_BLK == 0 or seq < FOX_BLK
    out = _layer(x[0], c[0], positions[0], w_mod[0], b_mod[0], g_pre_mix[0], g_post_mix[0], g_pre_ffn[0],
                 g_post_ffn[0], w_in[0], b_in[0], sinks[0], w_out[0], b_out[0], w_router[0], b_router[0],
                 w_gate_up[0], b_gate_up[0], w_down[0], b_down[0])
    return out[None]
```

```python
import functools

import jax
import jax.numpy as jnp
from jax import lax
from jax.experimental import pallas as pl
from jax.experimental.pallas import tpu as pltpu

HEAD_DIM = 64
SWA_Q_HEADS = 32
SWA_KV_HEADS = 4
SWA_GROUP = SWA_Q_HEADS // SWA_KV_HEADS
WINDOW = 128
FOX_HEADS = 32
ROPE_THETA = 10000.0
N_EXPERTS = 32
TOP_K = 4
D_FF = 1536
SWIGLU_LIMIT = 7.0
SWIGLU_ALPHA = 1.702
RMS_EPS = 1e-6
N_MOD = 6

SWA_Q_W = SWA_Q_HEADS * HEAD_DIM
SWA_KV_W = SWA_KV_HEADS * HEAD_DIM
FOX_W = FOX_HEADS * HEAD_DIM
MIX_W = SWA_Q_W + FOX_W
IN_W = SWA_Q_W + 2 * SWA_KV_W + 3 * FOX_W + FOX_HEADS
OFF_QA = 0
OFF_KA = SWA_Q_W
OFF_VA = OFF_KA + SWA_KV_W
OFF_QF = OFF_VA + SWA_KV_W
OFF_KF = OFF_QF + FOX_W
OFF_VF = OFF_KF + FOX_W
OFF_FL = OFF_VF + FOX_W

LANES = 128
SUBLANES = 8
V7X_VMEM_BYTES = 64 * 1024 * 1024
VMEM_LIMIT = V7X_VMEM_BYTES - 8 * 1024 * 1024

NEG = -1e30
LOG2E = 1.4426950408889634

F32 = jnp.float32
BF16 = jnp.bfloat16


def _params(sem, vmem=VMEM_LIMIT):
    return pltpu.CompilerParams(dimension_semantics=sem, vmem_limit_bytes=vmem)


MOD_ROWS = 256


def _mod_kernel(c_ref, w_ref, b_ref, o_ref):
    d, tn = w_ref.shape

    def body(r, acc):
        rows = pl.ds(pl.multiple_of(r * MOD_ROWS, MOD_ROWS), MOD_ROWS)
        c = c_ref[rows, :]
        sc = c * jax.nn.sigmoid(c)
        prod = w_ref[rows, :] * sc
        return acc + jnp.sum(prod.reshape(MOD_ROWS // SUBLANES, SUBLANES, tn), axis=0)

    acc = lax.fori_loop(0, d // MOD_ROWS, body, jnp.zeros((SUBLANES, tn), F32))
    o_ref[...] = jnp.sum(acc, axis=0, keepdims=True) + b_ref[...]


def _modulation(c, w_mod, b_mod):
    d, n = w_mod.shape
    tn = min(1024, n)
    assert n % tn == 0 and d % MOD_ROWS == 0
    return pl.pallas_call(
        _mod_kernel,
        out_shape=jax.ShapeDtypeStruct((1, n), F32),
        grid=(n // tn,),
        in_specs=[pl.BlockSpec((d, 1), lambda j: (0, 0)),
                  pl.BlockSpec((d, tn), lambda j: (0, j)),
                  pl.BlockSpec((1, tn), lambda j: (0, j))],
        out_specs=pl.BlockSpec((1, tn), lambda j: (0, j)),
        compiler_params=_params(("arbitrary",)),
        name="modulation",
    )(c.reshape(d, 1), w_mod, b_mod.reshape(1, n))


def _rms(x, g):
    inv = lax.rsqrt(jnp.mean(x * x, axis=-1, keepdims=True) + RMS_EPS)
    return x * inv * g


def _prenorm_kernel(x_ref, g_ref, scale_ref, shift_ref, o_ref):
    h = _rms(x_ref[...], g_ref[...]) * (1.0 + scale_ref[...]) + shift_ref[...]
    o_ref[...] = h.astype(o_ref.dtype)


def _prenorm(x, g, mod, scale_idx, shift_idx):
    s, d = x.shape
    tm = min(256, s)
    return pl.pallas_call(
        _prenorm_kernel,
        out_shape=jax.ShapeDtypeStruct((s, d), BF16),
        grid=(s // tm,),
        in_specs=[pl.BlockSpec((tm, d), lambda i: (i, 0)),
                  pl.BlockSpec((1, d), lambda i: (0, 0)),
                  pl.BlockSpec((1, d), lambda i: (0, scale_idx)),
                  pl.BlockSpec((1, d), lambda i: (0, shift_idx))],
        out_specs=pl.BlockSpec((tm, d), lambda i: (i, 0)),
        compiler_params=_params(("arbitrary",)),
        name="prenorm",
    )(x, g.reshape(1, d), mod, mod)


def _inproj_kernel(a_ref, wt_ref, b_ref, o_ref):
    acc = lax.dot_general(a_ref[...], wt_ref[...].astype(BF16), (((1,), (1,)), ((), ())),
                          preferred_element_type=F32)
    o_ref[...] = acc + b_ref[...]


def _inproj(h, w, b):
    m, k = h.shape
    n = w.shape[1]
    tm = min(1024, m)
    tn = 512
    return pl.pallas_call(
        _inproj_kernel,
        out_shape=jax.ShapeDtypeStruct((m, n), F32),
        grid=(m // tm, pl.cdiv(n, tn)),
        in_specs=[pl.BlockSpec((tm, k), lambda i, j: (i, 0)),
                  pl.BlockSpec((tn, k), lambda i, j: (j, 0)),
                  pl.BlockSpec((1, tn), lambda i, j: (0, j))],
        out_specs=pl.BlockSpec((tm, tn), lambda i, j: (i, j)),
        compiler_params=_params(("arbitrary", "arbitrary")),
        name="in_projection",
    )(h, w.T, b.reshape(1, n))


def _outproj_kernel(a1_ref, a2_ref, w_ref, b_ref, o_ref):
    k1 = a1_ref.shape[1]
    w = w_ref[...].astype(BF16)
    acc = jnp.dot(a1_ref[...], w[:k1], preferred_element_type=F32)
    acc = acc + jnp.dot(a2_ref[...], w[k1:], preferred_element_type=F32)
    o_ref[...] = acc + b_ref[...]


def _outproj(o_a, o_f, w, b):
    m, k1 = o_a.shape
    k2 = o_f.shape[1]
    n = w.shape[1]
    tm = min(1024, m)
    tn = min(512, n)
    return pl.pallas_call(
        _outproj_kernel,
        out_shape=jax.ShapeDtypeStruct((m, n), F32),
        grid=(m // tm, n // tn),
        in_specs=[pl.BlockSpec((tm, k1), lambda i, j: (i, 0)),
                  pl.BlockSpec((tm, k2), lambda i, j: (i, 0)),
                  pl.BlockSpec((k1 + k2, tn), lambda i, j: (0, j)),
                  pl.BlockSpec((1, tn), lambda i, j: (0, j))],
        out_specs=pl.BlockSpec((tm, tn), lambda i, j: (i, j)),
        compiler_params=_params(("arbitrary", "arbitrary")),
        name="out_projection",
    )(o_a, o_f, w, b.reshape(1, n))


def _swap_halves(x, first_half):
    return jnp.where(first_half, pltpu.roll(x, LANES - HEAD_DIM // 2, 1),
                     pltpu.roll(x, HEAD_DIM // 2, 1))


ROPE_ROWS = 512


def _rope_table_kernel(pos_ref, invf_ref, cos_ref, sin_ref):
    lane = lax.broadcasted_iota(jnp.int32, cos_ref.shape, 1)
    first_half = (lane % HEAD_DIM) < (HEAD_DIM // 2)
    ang = pos_ref[...].astype(F32) * invf_ref[...]
    sin = jnp.sin(ang)
    cos_ref[...] = jnp.cos(ang)
    sin_ref[...] = jnp.where(first_half, -sin, sin)


def _rope_tables(positions):
    s = positions.shape[0]
    tr = min(ROPE_ROWS, s)
    half = HEAD_DIM // 2
    inv_freq = jnp.power(ROPE_THETA, -jnp.arange(half, dtype=F32) * (2.0 / HEAD_DIM))
    invf = jnp.tile(inv_freq, LANES // half).reshape(1, LANES)
    table = jax.ShapeDtypeStruct((s, LANES), F32)
    return pl.pallas_call(
        _rope_table_kernel,
        out_shape=(table, table),
        grid=(s // tr,),
        in_specs=[pl.BlockSpec((tr, 1), lambda i: (i, 0)), pl.BlockSpec((1, LANES), lambda i: (0, 0))],
        out_specs=(pl.BlockSpec((tr, LANES), lambda i: (i, 0)), pl.BlockSpec((tr, LANES), lambda i: (i, 0))),
        compiler_params=_params(("arbitrary",)),
        name="rope_tables",
    )(positions.reshape(s, 1), invf)


def _swa_kernel(sinks_ref, cosc_ref, sinc_ref, cosp_ref, sinp_ref, q_ref, kp_ref, kc_ref, vp_ref, vc_ref, o_ref):
    n = pl.program_id(0)
    w = WINDOW
    lane = lax.broadcasted_iota(jnp.int32, (w, LANES), 1)
    first_half = (lane % HEAD_DIM) < (HEAD_DIM // 2)
    lane2 = lax.broadcasted_iota(jnp.int32, (2 * w, LANES), 1)
    lo64_2 = lane2 < HEAD_DIM

    cos_c, sin_c = cosc_ref[...], sinc_ref[...]
    cos_p, sin_p = cosp_ref[...], sinp_ref[...]

    def rope(t, cos, sin):
        return t * cos + _swap_halves(t, first_half) * sin

    stacked = (SWA_GROUP // 2) * w
    qi = lax.rem(lax.broadcasted_iota(jnp.int32, (stacked, 2 * w), 0), w)
    kj = lax.broadcasted_iota(jnp.int32, (stacked, 2 * w), 1)
    diff = qi + w - kj
    valid4 = (diff >= 0) & (diff < w) & ((n * w - w + kj) >= 0)

    scale = HEAD_DIM ** -0.5
    nt_dims = (((1,), (1,)), ((), ()))
    for kvp in range(SWA_KV_HEADS // 2):
        cols = slice(kvp * LANES, (kvp + 1) * LANES)
        k2 = jnp.concatenate([rope(kp_ref[:, cols], cos_p, sin_p),
                              rope(kc_ref[:, cols], cos_c, sin_c)], axis=0)
        v2 = jnp.concatenate([vp_ref[:, cols], vc_ref[:, cols]], axis=0)
        k2r = pltpu.roll(k2, HEAD_DIM, 1)
        v2r = pltpu.roll(v2, HEAD_DIM, 1)
        for sub in range(2):
            hkv = 2 * kvp + sub
            src_k, alt_k = (k2, k2r) if sub == 0 else (k2r, k2)
            src_v, alt_v = (v2, v2r) if sub == 0 else (v2r, v2)
            ka = jnp.where(lo64_2, src_k, 0.0).astype(BF16)
            kb = jnp.where(lo64_2, 0.0, alt_k).astype(BF16)
            va = jnp.where(lo64_2, src_v, 0.0).astype(BF16)
            vb = jnp.where(lo64_2, 0.0, alt_v).astype(BF16)
            pairs = [hkv * (SWA_GROUP // 2) + gp for gp in range(SWA_GROUP // 2)]
            q = jnp.concatenate([(rope(q_ref[:, pr * LANES:(pr + 1) * LANES], cos_c, sin_c) * scale).astype(BF16)
                                 for pr in pairs], axis=0)
            scores = [lax.dot_general(q, kk, nt_dims, preferred_element_type=F32) for kk in (ka, kb)]
            o = jnp.zeros((len(pairs) * w, LANES), F32)
            for which, (s, vv) in enumerate(zip(scores, (va, vb))):
                sink = jnp.concatenate([jnp.full((w, 1), sinks_ref[2 * pr + which], F32) for pr in pairs], axis=0)
                s = jnp.where(valid4, s, NEG)
                m = jnp.maximum(jnp.max(s, axis=-1, keepdims=True), sink)
                p = jnp.exp(s - m)
                den = jnp.sum(p, axis=-1, keepdims=True) + jnp.exp(sink - m)
                pv = jnp.dot(p.astype(BF16), vv, preferred_element_type=F32)
                o = o + pv * (1.0 / den)
            for r, pr in enumerate(pairs):
                o_ref[:, pr * LANES:(pr + 1) * LANES] = o[r * w:(r + 1) * w].astype(o_ref.dtype)


def _swa_attention(proj, positions, sinks):
    s = proj.shape[0]
    w = WINDOW
    nb = s // w
    cos, sin = _rope_tables(positions)
    prev = lambda n: jnp.maximum(n - 1, 0)
    ka_blk = OFF_KA // SWA_KV_W
    va_blk = OFF_VA // SWA_KV_W
    cur_tab = pl.BlockSpec((w, LANES), lambda n: (n, 0))
    prev_tab = pl.BlockSpec((w, LANES), lambda n: (prev(n), 0))
    return pl.pallas_call(
        _swa_kernel,
        out_shape=jax.ShapeDtypeStruct((s, SWA_Q_W), BF16),
        grid=(nb,),
        in_specs=[pl.BlockSpec(memory_space=pltpu.SMEM),
                  cur_tab, cur_tab, prev_tab, prev_tab,
                  pl.BlockSpec((w, SWA_Q_W), lambda n: (n, 0)),
                  pl.BlockSpec((w, SWA_KV_W), lambda n: (prev(n), ka_blk)),
                  pl.BlockSpec((w, SWA_KV_W), lambda n: (n, ka_blk)),
                  pl.BlockSpec((w, SWA_KV_W), lambda n: (prev(n), va_blk)),
                  pl.BlockSpec((w, SWA_KV_W), lambda n: (n, va_blk))],
        out_specs=pl.BlockSpec((w, SWA_Q_W), lambda n: (n, 0)),
        compiler_params=_params(("arbitrary",)),
        name="swa_sink_attention",
    )(sinks, cos, sin, cos, sin, proj, proj, proj, proj, proj)


CUM_ROWS = 256


def _cum_kernel(f_ref, o_ref, carry_ref):
    @pl.when(pl.program_id(0) == 0)
    def _():
        carry_ref[...] = jnp.zeros_like(carry_ref)

    f = f_ref[:, :FOX_HEADS]
    ls = jnp.minimum(f, 0.0) - jnp.log1p(jnp.exp(-jnp.abs(f)))
    r = lax.broadcasted_iota(jnp.int32, (CUM_ROWS, CUM_ROWS), 0)
    c = lax.broadcasted_iota(jnp.int32, (CUM_ROWS, CUM_ROWS), 1)
    tri = (c <= r).astype(F32)
    cum = jnp.dot(tri, ls, preferred_element_type=F32, precision=lax.Precision.HIGHEST) + carry_ref[...]
    o_ref[...] = cum
    carry_ref[...] = cum[CUM_ROWS - 1:CUM_ROWS, :]


def _fox_cum(proj):
    s = proj.shape[0]
    return pl.pallas_call(
        _cum_kernel,
        out_shape=jax.ShapeDtypeStruct((s, FOX_HEADS), F32),
        grid=(s // CUM_ROWS,),
        in_specs=[pl.BlockSpec((CUM_ROWS, LANES), lambda i: (i, OFF_FL // LANES))],
        out_specs=pl.BlockSpec((CUM_ROWS, FOX_HEADS), lambda i: (i, 0)),
        scratch_shapes=[pltpu.VMEM((1, FOX_HEADS), F32)],
        compiler_params=_params(("arbitrary",)),
        name="fox_decay_cumsum",
    )(proj)


FOX_BLK = 512
AUG_CQ = HEAD_DIM
AUG_ONE = HEAD_DIM + 3
AUG_END = HEAD_DIM + 6
PREP_HEADS = 8
SKIP_MARGIN = 170.0
NORM_SLACK = 1.001
BF16_ROUND_UP = 1.0 + 2.0 ** -7
assert OFF_QF % (PREP_HEADS * HEAD_DIM) == 0 and OFF_KF % (PREP_HEADS * HEAD_DIM) == 0
assert OFF_VF % (PREP_HEADS * HEAD_DIM) == 0


def _split3(c):
    hi = c.astype(BF16).astype(F32)
    r = c - hi
    mid = r.astype(BF16).astype(F32)
    lo = (r - mid).astype(BF16).astype(F32)
    return hi, mid, lo


def _fox_prep_kernel(q_ref, k_ref, v_ref, cum_ref, qo_ref, ko_ref, vo_ref, qn_ref, kn_ref):
    j = pl.program_id(0)
    tb = q_ref.shape[0]
    lane = lax.broadcasted_iota(jnp.int32, (tb, LANES), 1)
    lo64 = lane < HEAD_DIM
    lane_h = lax.broadcasted_iota(jnp.int32, (tb, FOX_HEADS), 1)
    cum = cum_ref[...]
    scale = HEAD_DIM ** -0.5 * LOG2E
    for local in range(PREP_HEADS):
        pair, hh = divmod(local, 2)
        h = PREP_HEADS * j + local
        col = jnp.sum(jnp.where(lane_h == h, cum, 0.0), axis=-1, keepdims=True)
        hi, mid, lo = _split3((col - col[0:1, :]) * LOG2E)
        cols = slice(pair * LANES, (pair + 1) * LANES)
        q = q_ref[:, cols] * scale
        k = k_ref[:, cols]
        v = v_ref[:, cols]
        if hh == 1:
            q = pltpu.roll(q, HEAD_DIM, 1)
            k = pltpu.roll(k, HEAD_DIM, 1)
            v = pltpu.roll(v, HEAD_DIM, 1)
        q_aug = jnp.where(lane == AUG_CQ, hi, jnp.where(lane == AUG_CQ + 1, mid,
                jnp.where(lane == AUG_CQ + 2, lo, jnp.where(lane < AUG_END, 1.0, 0.0))))
        k_aug = jnp.where(lane < AUG_ONE, 1.0, jnp.where(lane == AUG_ONE, -hi,
                jnp.where(lane == AUG_ONE + 1, -mid, jnp.where(lane == AUG_ONE + 2, -lo, 0.0))))
        v_aug = jnp.where(lane == HEAD_DIM, 1.0, 0.0)
        qb = jnp.where(lo64, q, q_aug).astype(BF16)
        kb = jnp.where(lo64, k, k_aug).astype(BF16)
        qo_ref[local] = qb
        ko_ref[local] = kb
        vo_ref[local, 0] = jnp.where(lo64, v, v_aug).T.astype(BF16)
        ones = jnp.ones((LANES, LANES), BF16)
        for src, norm_ref in ((qb, qn_ref), (kb, kn_ref)):
            x = jnp.where(lo64, src.astype(F32), 0.0)
            sq = jnp.dot((x * x * BF16_ROUND_UP).astype(BF16), ones, preferred_element_type=F32)
            norm_ref[local, 0] = jnp.broadcast_to(jnp.sqrt(jnp.max(sq, axis=0, keepdims=True)),
                                                  norm_ref.shape[2:])


def _fox_prep(proj, cum, tb):
    s = proj.shape[0]
    width = PREP_HEADS * HEAD_DIM
    shp = jax.ShapeDtypeStruct((FOX_HEADS, s, LANES), BF16)
    ospec = pl.BlockSpec((PREP_HEADS, tb, LANES), lambda j, i: (j, i, 0))
    norms = jax.ShapeDtypeStruct((FOX_HEADS, s // tb, SUBLANES, LANES), F32)
    nspec = pl.BlockSpec((PREP_HEADS, 1, SUBLANES, LANES), lambda j, i: (j, i, 0, 0))
    return pl.pallas_call(
        _fox_prep_kernel,
        out_shape=(shp, shp, jax.ShapeDtypeStruct((FOX_HEADS, s // tb, LANES, tb), BF16), norms, norms),
        grid=(FOX_HEADS // PREP_HEADS, s // tb),
        in_specs=[pl.BlockSpec((tb, width), lambda j, i: (i, OFF_QF // width + j)),
                  pl.BlockSpec((tb, width), lambda j, i: (i, OFF_KF // width + j)),
                  pl.BlockSpec((tb, width), lambda j, i: (i, OFF_VF // width + j)),
                  pl.BlockSpec((tb, FOX_HEADS), lambda j, i: (i, 0))],
        out_specs=(ospec, ospec, pl.BlockSpec((PREP_HEADS, 1, LANES, tb), lambda j, i: (j, i, 0, 0)),
                   nspec, nspec),
        compiler_params=_params(("arbitrary", "arbitrary")),
        name="fox_prepare",
    )(proj, proj, proj, cum)


def _fox_kernel(a_ref, e_ref, qk_ref, q_ref, k_ref, vt_ref, o_ref, acc_ref, m_ref, s0_ref):
    j = pl.program_id(0)
    i = pl.program_id(1)
    tb = q_ref.shape[1]
    nt_dims = (((1,), (1,)), ((), ()))
    key = lax.broadcasted_iota(jnp.int32, (tb, tb), 0)
    qry = lax.broadcasted_iota(jnp.int32, (tb, tb), 1)
    causal = key <= qry

    def scores(hh, kt, diagonal=False):
        rows = pl.ds(pl.multiple_of(kt * tb, tb), tb)
        s = lax.dot_general(k_ref[hh, rows, :], q_ref[hh], nt_dims, preferred_element_type=F32)
        return jnp.where(causal, s, NEG) if diagonal else s

    def update(hh, kt, s):
        h = 2 * j + hh
        delta = (a_ref[h, i] - a_ref[h, kt]) * LOG2E
        m_old = m_ref[hh]
        m_new = jnp.maximum(m_old, jnp.max(s, axis=0, keepdims=True) + delta)
        p = jnp.exp2(s - (m_new - delta))
        alpha = jnp.exp2(m_old - m_new)
        acc_ref[hh] = alpha * acc_ref[hh] + jnp.dot(vt_ref[hh, kt], p.astype(BF16), preferred_element_type=F32)
        m_ref[hh] = m_new

    def step(kt, diagonal=False):
        s1 = scores(1, kt, diagonal)
        update(0, kt, s0_ref[...])
        s0_ref[...] = scores(0, jnp.maximum(kt - 1, 0))
        update(1, kt, s1)

    def tiles_needed(hh):
        h = 2 * j + hh
        floor = jnp.min(m_ref[hh]) - SKIP_MARGIN
        top = qk_ref[h, i] + a_ref[h, i] * LOG2E

        def first_kept(kt, lo):
            kept = top - e_ref[h, kt] * LOG2E >= floor
            return jnp.where(kept, jnp.minimum(lo, kt), lo)

        return i - lax.fori_loop(0, i, first_kept, i)

    m_ref[...] = jnp.full(m_ref.shape, NEG, F32)
    acc_ref[...] = jnp.zeros(acc_ref.shape, F32)
    s0_ref[...] = scores(0, i, diagonal=True)
    step(i, diagonal=True)
    n_back = jnp.maximum(tiles_needed(0), tiles_needed(1))
    lax.fori_loop(0, n_back, lambda t, carry: (step(i - 1 - t), carry)[1], 0)
    acc0 = acc_ref[0]
    acc1 = acc_ref[1]
    o0 = (acc0 * (1.0 / acc0[HEAD_DIM:HEAD_DIM + 1, :])).T
    o1 = (acc1 * (1.0 / acc1[HEAD_DIM:HEAD_DIM + 1, :])).T
    lo64 = lax.broadcasted_iota(jnp.int32, (tb, LANES), 1) < HEAD_DIM
    o_ref[...] = jnp.where(lo64, o0, pltpu.roll(o1, HEAD_DIM, 1)).astype(o_ref.dtype)


def _fox_attention(proj):
    s = proj.shape[0]
    tb = min(FOX_BLK, s)
    cum = _fox_cum(proj)
    q_aug, k_aug, vt_aug, q_norm, k_norm = _fox_prep(proj, cum, tb)
    block_first = cum[::tb].T
    block_last = cum[tb - 1::tb].T
    qk_bound = q_norm[:, :, 0, 0] * jnp.max(k_norm[:, :, 0, 0], axis=1, keepdims=True) * NORM_SLACK
    npairs = FOX_HEADS // 2
    smem = pl.BlockSpec(memory_space=pltpu.SMEM)
    return pl.pallas_call(
        _fox_kernel,
        out_shape=jax.ShapeDtypeStruct((s, FOX_W), BF16),
        grid=(npairs, s // tb),
        in_specs=[smem, smem, smem,
                  pl.BlockSpec((2, tb, LANES), lambda j, i: (j, i, 0)),
                  pl.BlockSpec((2, s, LANES), lambda j, i: (j, 0, 0)),
                  pl.BlockSpec((2, s // tb, LANES, tb), lambda j, i: (j, 0, 0, 0))],
        out_specs=pl.BlockSpec((tb, LANES), lambda j, i: (i, j)),
        scratch_shapes=[pltpu.VMEM((2, LANES, tb), F32), pltpu.VMEM((2, 1, tb), F32), pltpu.VMEM((tb, tb), F32)],
        compiler_params=_params(("arbitrary", "arbitrary")),
        name="fox_attention",
    )(block_first, block_last, qk_bound, q_aug, k_aug, vt_aug)


def _lane_pack(cols, width, dtype):
    rows = cols[0].shape[0]
    lane = lax.broadcasted_iota(jnp.int32, (rows, width), 1)
    out = jnp.zeros((rows, width), dtype)
    for k, cval in enumerate(cols):
        out = jnp.where(lane == k, cval.astype(dtype), out)
    return out


def _postmix_kernel(x_ref, y_ref, gpost_ref, gate_ref, gpre_ref, scale_ref, shift_ref, wr_ref, br_ref,
                    x1_ref, h2_ref, idx_ref, prob_ref, rank_ref, cnt_ref, carry_ref):
    step = pl.program_id(0)

    @pl.when(step == 0)
    def _():
        carry_ref[...] = jnp.zeros_like(carry_ref)

    x1 = x_ref[...] + gate_ref[...] * _rms(y_ref[...], gpost_ref[...])
    x1_ref[...] = x1
    h2 = _rms(x1, gpre_ref[...]) * (1.0 + scale_ref[...]) + shift_ref[...]
    h2_ref[...] = h2
    def two_pieces(v):
        hi = v.astype(BF16)
        return hi, (v - hi.astype(F32)).astype(BF16)

    h_hi, h_lo = two_pieces(h2)
    w_hi, w_lo = two_pieces(wr_ref[...])
    logits = (jnp.dot(h_hi, w_hi, preferred_element_type=F32) + jnp.dot(h_hi, w_lo, preferred_element_type=F32)
              + jnp.dot(h_lo, w_hi, preferred_element_type=F32) + br_ref[...])
    tm = logits.shape[0]
    lane_e = lax.broadcasted_iota(jnp.int32, (tm, N_EXPERTS), 1).astype(F32)
    vals, idxs, sels = [], [], []
    cur = logits
    for _ in range(TOP_K):
        mk = jnp.max(cur, axis=-1, keepdims=True)
        ik = jnp.min(jnp.where(cur == mk, lane_e, float(N_EXPERTS)), axis=-1, keepdims=True)
        sel = lane_e == ik
        vals.append(mk)
        idxs.append(ik)
        sels.append(sel)
        cur = jnp.where(sel, -jnp.inf, cur)
    exps = [jnp.exp(v - vals[0]) for v in vals]
    inv = 1.0 / functools.reduce(lambda a, b: a + b, exps)
    probs = [e * inv for e in exps]

    onehot = functools.reduce(lambda a, b: a | b, sels).astype(F32)
    r = lax.broadcasted_iota(jnp.int32, (tm, tm), 0)
    c = lax.broadcasted_iota(jnp.int32, (tm, tm), 1)
    strict = (c < r).astype(BF16)
    before = jnp.dot(strict, onehot.astype(BF16), preferred_element_type=F32) + carry_ref[...]
    ranks = [jnp.sum(jnp.where(sel, before, 0.0), axis=-1, keepdims=True) for sel in sels]
    carry_ref[...] = carry_ref[...] + jnp.sum(onehot, axis=0, keepdims=True)

    idx_ref[...] = _lane_pack(idxs, LANES, jnp.int32)
    prob_ref[...] = _lane_pack(probs, LANES, F32)
    rank_ref[...] = _lane_pack([rk.astype(jnp.int32) for rk in ranks], LANES, jnp.int32)
    cnt_ref[...] = carry_ref[...].astype(jnp.int32)


def _postmix_router(x, y, mod, g_post, g_pre, w_router, b_router):
    s, d = x.shape
    tm = min(256, s)
    row = pl.BlockSpec((tm, d), lambda i: (i, 0))
    vec = lambda idx: pl.BlockSpec((1, d), lambda i: (0, idx))
    lanes = pl.BlockSpec((tm, LANES), lambda i: (i, 0))
    return pl.pallas_call(
        _postmix_kernel,
        out_shape=(jax.ShapeDtypeStruct((s, d), F32), jax.ShapeDtypeStruct((s, d), F32),
                   jax.ShapeDtypeStruct((s, LANES), jnp.int32), jax.ShapeDtypeStruct((s, LANES), F32),
                   jax.ShapeDtypeStruct((s, LANES), jnp.int32),
                   jax.ShapeDtypeStruct((1, N_EXPERTS), jnp.int32)),
        grid=(s // tm,),
        in_specs=[row, row, vec(0), vec(2), vec(0), vec(4), vec(3),
                  pl.BlockSpec((d, N_EXPERTS), lambda i: (0, 0)),
                  pl.BlockSpec((1, N_EXPERTS), lambda i: (0, 0))],
        out_specs=(row, row, lanes, lanes, lanes, pl.BlockSpec((1, N_EXPERTS), lambda i: (0, 0))),
        scratch_shapes=[pltpu.VMEM((1, N_EXPERTS), F32)],
        compiler_params=_params(("arbitrary",)),
        name="postmix_router",
    )(x, y, g_post.reshape(1, d), mod, g_pre.reshape(1, d), mod, mod, w_router, b_router.reshape(1, N_EXPERTS))


MOE_CAP = 1280
MOE_UNIT = 128
assert MOE_CAP % MOE_UNIT == 0
GU_TN = 512
DEINT = 256
DOWN_TN = 1024


def _item_maps(n_j):
    def clamp(i, j, ni):
        used = i < ni[0]
        return jnp.where(used, i, ni[0] - 1), jnp.where(used, j, n_j - 1)

    def item_rows(i, j, ie, ir, ni):
        return clamp(i, j, ni)[0], 0, 0

    def act_rows(i, j, ie, ir, ni):
        return clamp(i, j, ni)[0], 0

    def weights(i, j, ie, ir, ni):
        ii, jj = clamp(i, j, ni)
        return ie[ii], 0, jj

    return item_rows, act_rows, weights


def _dot_mixed(a_bf16, w_f32):
    return lax.dot_general(a_bf16, w_f32, (((1,), (0,)), ((), ())), preferred_element_type=F32)


def _for_rows(n, body, unroll=SUBLANES):
    def trip(t, carry):
        for u in range(unroll):
            body(t * unroll + u, u)
        return carry

    lax.fori_loop(0, n // unroll, trip, 0)
    lax.fori_loop((n // unroll) * unroll, n, lambda r, carry: (body(r, 0), carry)[1], 0)


def _wait_rows(src, dst, sem, n):
    n8 = pl.multiple_of((n // SUBLANES) * SUBLANES, SUBLANES)

    @pl.when(n8 > 0)
    def _():
        pltpu.make_async_copy(src.at[pl.ds(0, n8)], dst.at[pl.ds(0, n8)], sem).wait()

    one = pltpu.make_async_copy(src.at[pl.ds(0, 1)], dst.at[pl.ds(0, 1)], sem)
    lax.fori_loop(n8, n, lambda r, carry: (one.wait(), carry)[1], 0)


def _units(rows):
    return (rows + MOE_UNIT - 1) // MOE_UNIT


def _unit_rows(start_unit, n_units):
    return pl.ds(pl.multiple_of(start_unit * MOE_UNIT, MOE_UNIT), n_units * MOE_UNIT)


def _for_blocks(rows, compute, finish):
    units = _units(rows)
    triple = jnp.logical_and(units % 2 == 1, units >= 3)
    single = units == 1
    n2 = (units - jnp.where(triple, 3, 0) - jnp.where(single, 1, 0)) // 2

    def run(slices):
        zs = [compute(rs) for rs in slices]
        for rs, z in zip(slices, zs):
            finish(rs, z)

    def pair(pb, carry):
        run([_unit_rows(4 * pb, 2), _unit_rows(4 * pb + 2, 2)])
        return carry

    lax.fori_loop(0, n2 // 2, pair, 0)

    @pl.when(n2 % 2 == 1)
    def _():
        run([_unit_rows(2 * (n2 - 1), 2)])

    @pl.when(triple)
    def _():
        run([_unit_rows(2 * n2, 3)])

    @pl.when(single)
    def _():
        run([_unit_rows(0, 1)])


def _gateup_kernel(ie_ref, ir_ref, ni_ref, dst_ref, h_ref, w_ref, b_ref, o_ref, stg_ref, xb_ref, sem):
    i = pl.program_id(0)
    j = pl.program_id(1)
    cap = o_ref.shape[0]

    @pl.when(i >= ni_ref[0])
    def _():
        o_ref[...] = jnp.zeros_like(o_ref)

    @pl.when(i < ni_ref[0])
    def _():
        rows = ir_ref[i]

        def row_copy(r):
            token = lax.rem(dst_ref[0, 0, r].astype(jnp.uint32), jnp.uint32(h_ref.shape[0])).astype(jnp.int32)
            return pltpu.make_async_copy(h_ref.at[pl.ds(token, 1)], stg_ref.at[pl.ds(r, 1)], sem)

        @pl.when(j == 0)
        def _():
            @pl.when(i == 0)
            def _():
                stg_ref[...] = jnp.zeros_like(stg_ref)

            _for_rows(rows, lambda r, u: row_copy(r).start(priority=u % 2))
            _wait_rows(h_ref, stg_ref, sem, rows)

            def to_bf16(u, carry):
                xb_ref[_unit_rows(u, 1), :] = stg_ref[_unit_rows(u, 1), :].astype(BF16)
                return carry

            lax.fori_loop(0, _units(rows), to_bf16, 0)

        r = lax.broadcasted_iota(jnp.int32, (DEINT, DEINT), 0)
        c = lax.broadcasted_iota(jnp.int32, (DEINT, DEINT), 1)
        half = DEINT // 2
        perm = (r == jnp.where(c < half, 2 * c, 2 * (c - half) + 1)).astype(BF16)

        def gate_up(rs):
            return _dot_mixed(xb_ref[rs, :], w_ref[...]) + b_ref[...]

        def activate(rs, z):
            for t in range(GU_TN // DEINT):
                zz = jnp.dot(z[:, t * DEINT:(t + 1) * DEINT].astype(BF16), perm, preferred_element_type=F32)
                glu = jnp.minimum(zz[:, :half], SWIGLU_LIMIT)
                lin = jnp.clip(zz[:, half:], -SWIGLU_LIMIT, SWIGLU_LIMIT)
                act = glu * jax.nn.sigmoid(SWIGLU_ALPHA * glu) * (lin + 1.0)
                o_ref[rs, t * half:(t + 1) * half] = act.astype(o_ref.dtype)

        _for_blocks(rows, gate_up, activate)

        def zero_unit(u, carry):
            o_ref[_unit_rows(u, 1), :] = jnp.zeros((MOE_UNIT, o_ref.shape[1]), o_ref.dtype)
            return carry

        lax.fori_loop(_units(rows), cap // MOE_UNIT, zero_unit, 0)


def _moe_gateup(h2, dst, item_expert, item_rows, num_items, w_gate_up, b_gate_up, cap):
    s, d = h2.shape
    n_items = dst.shape[0]
    n_j = (2 * D_FF) // GU_TN
    item_map, _, weights = _item_maps(n_j)
    grid_spec = pltpu.PrefetchScalarGridSpec(
        num_scalar_prefetch=3,
        grid=(n_items, n_j),
        in_specs=[pl.BlockSpec((1, 1, cap), item_map, memory_space=pltpu.SMEM),
                  pl.BlockSpec(memory_space=pl.ANY),
                  pl.BlockSpec((None, d, GU_TN), weights),
                  pl.BlockSpec((None, 1, GU_TN), weights)],
        out_specs=pl.BlockSpec((cap, GU_TN // 2), lambda i, j, ie, ir, ni: (i, j)),
        scratch_shapes=[pltpu.VMEM((cap, d), F32), pltpu.VMEM((cap, d), BF16), pltpu.SemaphoreType.DMA(())],
    )
    return pl.pallas_call(
        _gateup_kernel,
        out_shape=jax.ShapeDtypeStruct((n_items * cap, D_FF), BF16),
        grid_spec=grid_spec,
        compiler_params=_params(("arbitrary", "arbitrary")),
        name="moe_gate_up",
    )(item_expert, item_rows, num_items, dst, h2, w_gate_up, b_gate_up.reshape(N_EXPERTS, 1, 2 * D_FF))


def _down_kernel(ie_ref, ir_ref, ni_ref, dst_ref, a_ref, w_ref, b_ref, o_ref, y_ref, sem):
    i = pl.program_id(0)
    j = pl.program_id(1)
    tn = w_ref.shape[1]
    n_j = y_ref.shape[1] // tn

    @pl.when(i < ni_ref[0])
    def _():
        rows = ir_ref[i]

        def down(rs):
            return _dot_mixed(a_ref[rs, :], w_ref[...]) + b_ref[...]

        for jj in range(n_j):
            @pl.when(j == jj)
            def _(jj=jj):
                def keep(rs, y):
                    y_ref[rs, jj * tn:(jj + 1) * tn] = y

                _for_blocks(rows, down, keep)

        @pl.when(j == n_j - 1)
        def _():
            def row_copy(r, dst_row):
                return pltpu.make_async_copy(y_ref.at[pl.ds(r, 1)], o_ref.at[pl.ds(dst_row, 1)], sem)

            _for_rows(rows, lambda r, u: row_copy(r, dst_ref[0, 0, r]).start(priority=u % 2))
            _wait_rows(y_ref, o_ref, sem, rows)


def _moe_down(act, dst, item_expert, item_rows, num_items, w_down, b_down, n_out_rows, cap):
    d = w_down.shape[2]
    tn = min(DOWN_TN, d)
    n_items = dst.shape[0]
    n_j = d // tn
    item_map, act_rows, weights = _item_maps(n_j)
    grid_spec = pltpu.PrefetchScalarGridSpec(
        num_scalar_prefetch=3,
        grid=(n_items, n_j),
        in_specs=[pl.BlockSpec((1, 1, cap), item_map, memory_space=pltpu.SMEM),
                  pl.BlockSpec((cap, D_FF), act_rows),
                  pl.BlockSpec((None, D_FF, tn), weights),
                  pl.BlockSpec((None, 1, tn), weights)],
        out_specs=pl.BlockSpec(memory_space=pl.ANY),
        scratch_shapes=[pltpu.VMEM((cap, d), F32), pltpu.SemaphoreType.DMA(())],
    )
    return pl.pallas_call(
        _down_kernel,
        out_shape=jax.ShapeDtypeStruct((n_out_rows, d), F32),
        grid_spec=grid_spec,
        compiler_params=_params(("arbitrary", "arbitrary")),
        name="moe_down",
    )(item_expert, item_rows, num_items, dst, act, w_down, b_down.reshape(N_EXPERTS, 1, d))


COMBINE_TOKENS = 128


def _combine_kernel(prob_ref, x1_ref, g_ref, gate_ref, ys_ref, o_ref):
    prob = prob_ref[...]
    y = prob[:, 0:1] * ys_ref[0]
    for k in range(1, TOP_K):
        y = y + prob[:, k:k + 1] * ys_ref[k]
    o_ref[...] = x1_ref[...] + gate_ref[...] * _rms(y, g_ref[...])


def _moe_combine(ys, probs, x1, g_post, mod, gate_idx):
    s, d = x1.shape
    tc = min(COMBINE_TOKENS, s)
    return pl.pallas_call(
        _combine_kernel,
        out_shape=jax.ShapeDtypeStruct((s, d), F32),
        grid=(s // tc,),
        in_specs=[pl.BlockSpec((tc, LANES), lambda i: (i, 0)),
                  pl.BlockSpec((tc, d), lambda i: (i, 0)),
                  pl.BlockSpec((1, d), lambda i: (0, 0)),
                  pl.BlockSpec((1, d), lambda i: (0, gate_idx)),
                  pl.BlockSpec((TOP_K, tc, d), lambda i: (0, i, 0))],
        out_specs=pl.BlockSpec((tc, d), lambda i: (i, 0)),
        compiler_params=_params(("arbitrary",)),
        name="moe_combine",
    )(probs, x1, g_post.reshape(1, d), mod, ys.reshape(TOP_K, s, d))


def _moe(h2, idx, probs, rank, counts, x1, mod, g_post, w_gate_up, b_gate_up, w_down, b_down):
    s, d = x1.shape
    cap = MOE_CAP
    i32 = jnp.int32
    counts = counts.reshape(N_EXPERTS)
    items_per_expert = (counts + cap - 1) // cap
    item_end = jnp.cumsum(items_per_expert)
    first_item = (item_end - items_per_expert).astype(i32)
    n_items_max = (s * TOP_K + N_EXPERTS * (cap - 1)) // cap
    num_items = item_end[-1].astype(i32).reshape(1)
    item_ids = jnp.arange(n_items_max, dtype=i32)
    item_expert = jnp.minimum(jnp.sum(item_end[None, :] <= item_ids[:, None], axis=1), N_EXPERTS - 1).astype(i32)
    item_rows = jnp.clip(counts[item_expert] - cap * (item_ids - first_item[item_expert]), 0, cap)
    item_rows = jnp.where(item_ids < num_items[0], item_rows, 0).astype(i32)

    pos = (first_item[idx[:, :TOP_K]] * cap + rank[:, :TOP_K]).reshape(-1)
    tok = jnp.repeat(jnp.arange(s, dtype=i32), TOP_K)
    slot = jnp.tile(jnp.arange(TOP_K, dtype=i32), s)
    dst = jnp.zeros((n_items_max * cap,), i32).at[pos].set(slot * s + tok, unique_indices=True)
    dst = dst.reshape(n_items_max, 1, cap)

    act = _moe_gateup(h2, dst, item_expert, item_rows, num_items, w_gate_up, b_gate_up, cap)
    ys = _moe_down(act, dst, item_expert, item_rows, num_items, w_down, b_down, TOP_K * s, cap)
    return _moe_combine(ys, probs, x1, g_post, mod, 5)


def _layer(x, c, positions, w_mod, b_mod, g_pre_mix, g_post_mix, g_pre_ffn, g_post_ffn,
           w_in, b_in, sinks, w_out, b_out, w_router, b_router, w_gate_up, b_gate_up, w_down, b_down):
    mod = _modulation(c, w_mod, b_mod)
    h = _prenorm(x, g_pre_mix, mod, 1, 0)
    proj = _inproj(h, w_in, b_in)
    o_a = _swa_attention(proj, positions, sinks)
    o_f = _fox_attention(proj)
    y = _outproj(o_a, o_f, w_out, b_out)
    x1, h2, idx, probs, rank, counts = _postmix_router(x, y, mod, g_post_mix, g_pre_ffn, w_router, b_router)
    return _moe(h2, idx, probs, rank, counts, x1, mod, g_post_ffn, w_gate_up, b_gate_up, w_down, b_down)


def kernel(x, c, positions, w_mod, b_mod, g_pre_mix, g_post_mix, g_pre_ffn, g_post_ffn, w_in, b_in, sinks,
           w_out, b_out, w_router, b_router, w_gate_up, b_gate_up, w_down, b_down):
    batch, seq, d = x.shape
    assert batch == 1 and w_mod.shape[0] == 1, "one sequence, one layer"
    assert seq % FOX_BLK == 0 or seq < FOX_BLK
    out = _layer(x[0], c[0], positions[0], w_mod[0], b_mod[0], g_pre_mix[0], g_post_mix[0], g_pre_ffn[0],
                 g_post_ffn[0], w_in[0], b_in[0], sinks[0], w_out[0], b_out[0], w_router[0], b_router[0],
                 w_gate_up[0], b_gate_up[0], w_down[0], b_down[0])
    return out[None]
```

```python
import functools

import jax
import jax.numpy as jnp
from jax import lax
from jax.experimental import pallas as pl
from jax.experimental.pallas import tpu as pltpu

HEAD_DIM = 64
SWA_Q_HEADS = 32
SWA_KV_HEADS = 4
SWA_GROUP = SWA_Q_HEADS // SWA_KV_HEADS
WINDOW = 128
FOX_HEADS = 32
ROPE_THETA = 10000.0
N_EXPERTS = 32
TOP_K = 4
D_FF = 1536
SWIGLU_LIMIT = 7.0
SWIGLU_ALPHA = 1.702
RMS_EPS = 1e-6
N_MOD = 6

SWA_Q_W = SWA_Q_HEADS * HEAD_DIM
SWA_KV_W = SWA_KV_HEADS * HEAD_DIM
FOX_W = FOX_HEADS * HEAD_DIM
MIX_W = SWA_Q_W + FOX_W
IN_W = SWA_Q_W + 2 * SWA_KV_W + 3 * FOX_W + FOX_HEADS
OFF_QA = 0
OFF_KA = SWA_Q_W
OFF_VA = OFF_KA + SWA_KV_W
OFF_QF = OFF_VA + SWA_KV_W
OFF_KF = OFF_QF + FOX_W
OFF_VF = OFF_KF + FOX_W
OFF_FL = OFF_VF + FOX_W

LANES = 128
SUBLANES = 8
V7X_VMEM_BYTES = 64 * 1024 * 1024
VMEM_LIMIT = V7X_VMEM_BYTES - 8 * 1024 * 1024

NEG = -1e30
LOG2E = 1.4426950408889634

F32 = jnp.float32
BF16 = jnp.bfloat16


def _params(sem, vmem=VMEM_LIMIT):
    return pltpu.CompilerParams(dimension_semantics=sem, vmem_limit_bytes=vmem)


MOD_ROWS = 256


def _mod_kernel(c_ref, w_ref, b_ref, o_ref):
    d, tn = w_ref.shape

    def body(r, acc):
        rows = pl.ds(pl.multiple_of(r * MOD_ROWS, MOD_ROWS), MOD_ROWS)
        c = c_ref[rows, :]
        sc = c * jax.nn.sigmoid(c)
        prod = w_ref[rows, :] * sc
        return acc + jnp.sum(prod.reshape(MOD_ROWS // SUBLANES, SUBLANES, tn), axis=0)

    acc = lax.fori_loop(0, d // MOD_ROWS, body, jnp.zeros((SUBLANES, tn), F32))
    o_ref[...] = jnp.sum(acc, axis=0, keepdims=True) + b_ref[...]


def _modulation(c, w_mod, b_mod):
    d, n = w_mod.shape
    tn = min(1024, n)
    assert n % tn == 0 and d % MOD_ROWS == 0
    return pl.pallas_call(
        _mod_kernel,
        out_shape=jax.ShapeDtypeStruct((1, n), F32),
        grid=(n // tn,),
        in_specs=[pl.BlockSpec((d, 1), lambda j: (0, 0)),
                  pl.BlockSpec((d, tn), lambda j: (0, j)),
                  pl.BlockSpec((1, tn), lambda j: (0, j))],
        out_specs=pl.BlockSpec((1, tn), lambda j: (0, j)),
        compiler_params=_params(("arbitrary",)),
        name="modulation",
    )(c.reshape(d, 1), w_mod, b_mod.reshape(1, n))


def _rms(x, g):
    inv = lax.rsqrt(jnp.mean(x * x, axis=-1, keepdims=True) + RMS_EPS)
    return x * inv * g


def _prenorm_kernel(x_ref, g_ref, scale_ref, shift_ref, o_ref):
    h = _rms(x_ref[...], g_ref[...]) * (1.0 + scale_ref[...]) + shift_ref[...]
    o_ref[...] = h.astype(o_ref.dtype)


def _prenorm(x, g, mod, scale_idx, shift_idx):
    s, d = x.shape
    tm = min(256, s)
    return pl.pallas_call(
        _prenorm_kernel,
        out_shape=jax.ShapeDtypeStruct((s, d), BF16),
        grid=(s // tm,),
        in_specs=[pl.BlockSpec((tm, d), lambda i: (i, 0)),
                  pl.BlockSpec((1, d), lambda i: (0, 0)),
                  pl.BlockSpec((1, d), lambda i: (0, scale_idx)),
                  pl.BlockSpec((1, d), lambda i: (0, shift_idx))],
        out_specs=pl.BlockSpec((tm, d), lambda i: (i, 0)),
        compiler_params=_params(("arbitrary",)),
        name="prenorm",
    )(x, g.reshape(1, d), mod, mod)


def _inproj_kernel(a_ref, wt_ref, b_ref, o_ref):
    acc = lax.dot_general(a_ref[...], wt_ref[...].astype(BF16), (((1,), (1,)), ((), ())),
                          preferred_element_type=F32)
    o_ref[...] = acc + b_ref[...]


def _inproj(h, w, b):
    m, k = h.shape
    n = w.shape[1]
    tm = min(1024, m)
    tn = 512
    return pl.pallas_call(
        _inproj_kernel,
        out_shape=jax.ShapeDtypeStruct((m, n), F32),
        grid=(m // tm, pl.cdiv(n, tn)),
        in_specs=[pl.BlockSpec((tm, k), lambda i, j: (i, 0)),
                  pl.BlockSpec((tn, k), lambda i, j: (j, 0)),
                  pl.BlockSpec((1, tn), lambda i, j: (0, j))],
        out_specs=pl.BlockSpec((tm, tn), lambda i, j: (i, j)),
        compiler_params=_params(("arbitrary", "arbitrary")),
        name="in_projection",
    )(h, w.T, b.reshape(1, n))


def _outproj_kernel(a1_ref, a2_ref, w_ref, b_ref, o_ref):
    k1 = a1_ref.shape[1]
    w = w_ref[...].astype(BF16)
    acc = jnp.dot(a1_ref[...], w[:k1], preferred_element_type=F32)
    acc = acc + jnp.dot(a2_ref[...], w[k1:], preferred_element_type=F32)
    o_ref[...] = acc + b_ref[...]


def _outproj(o_a, o_f, w, b):
    m, k1 = o_a.shape
    k2 = o_f.shape[1]
    n = w.shape[1]
    tm = min(1024, m)
    tn = min(512, n)
    return pl.pallas_call(
        _outproj_kernel,
        out_shape=jax.ShapeDtypeStruct((m, n), F32),
        grid=(m // tm, n // tn),
        in_specs=[pl.BlockSpec((tm, k1), lambda i, j: (i, 0)),
                  pl.BlockSpec((tm, k2), lambda i, j: (i, 0)),
                  pl.BlockSpec((k1 + k2, tn), lambda i, j: (0, j)),
                  pl.BlockSpec((1, tn), lambda i, j: (0, j))],
        out_specs=pl.BlockSpec((tm, tn), lambda i, j: (i, j)),
        compiler_params=_params(("arbitrary", "arbitrary")),
        name="out_projection",
    )(o_a, o_f, w, b.reshape(1, n))


def _swap_halves(x, first_half):
    return jnp.where(first_half, pltpu.roll(x, LANES - HEAD_DIM // 2, 1),
                     pltpu.roll(x, HEAD_DIM // 2, 1))


ROPE_ROWS = 512


def _rope_table_kernel(pos_ref, invf_ref, cos_ref, sin_ref):
    lane = lax.broadcasted_iota(jnp.int32, cos_ref.shape, 1)
    first_half = (lane % HEAD_DIM) < (HEAD_DIM // 2)
    ang = pos_ref[...].astype(F32) * invf_ref[...]
    sin = jnp.sin(ang)
    cos_ref[...] = jnp.cos(ang)
    sin_ref[...] = jnp.where(first_half, -sin, sin)


def _rope_tables(positions):
    s = positions.shape[0]
    tr = min(ROPE_ROWS, s)
    half = HEAD_DIM // 2
    inv_freq = jnp.power(ROPE_THETA, -jnp.arange(half, dtype=F32) * (2.0 / HEAD_DIM))
    invf = jnp.tile(inv_freq, LANES // half).reshape(1, LANES)
    table = jax.ShapeDtypeStruct((s, LANES), F32)
    return pl.pallas_call(
        _rope_table_kernel,
        out_shape=(table, table),
        grid=(s // tr,),
        in_specs=[pl.BlockSpec((tr, 1), lambda i: (i, 0)), pl.BlockSpec((1, LANES), lambda i: (0, 0))],
        out_specs=(pl.BlockSpec((tr, LANES), lambda i: (i, 0)), pl.BlockSpec((tr, LANES), lambda i: (i, 0))),
        compiler_params=_params(("arbitrary",)),
        name="rope_tables",
    )(positions.reshape(s, 1), invf)


def _swa_kernel(sinks_ref, cosc_ref, sinc_ref, cosp_ref, sinp_ref, q_ref, kp_ref, kc_ref, vp_ref, vc_ref, o_ref):
    n = pl.program_id(0)
    w = WINDOW
    lane = lax.broadcasted_iota(jnp.int32, (w, LANES), 1)
    first_half = (lane % HEAD_DIM) < (HEAD_DIM // 2)
    lane2 = lax.broadcasted_iota(jnp.int32, (2 * w, LANES), 1)
    lo64_2 = lane2 < HEAD_DIM

    cos_c, sin_c = cosc_ref[...], sinc_ref[...]
    cos_p, sin_p = cosp_ref[...], sinp_ref[...]

    def rope(t, cos, sin):
        return t * cos + _swap_halves(t, first_half) * sin

    stacked = (SWA_GROUP // 2) * w
    qi = lax.rem(lax.broadcasted_iota(jnp.int32, (stacked, 2 * w), 0), w)
    kj = lax.broadcasted_iota(jnp.int32, (stacked, 2 * w), 1)
    diff = qi + w - kj
    valid4 = (diff >= 0) & (diff < w) & ((n * w - w + kj) >= 0)

    scale = HEAD_DIM ** -0.5
    nt_dims = (((1,), (1,)), ((), ()))
    for kvp in range(SWA_KV_HEADS // 2):
        cols = slice(kvp * LANES, (kvp + 1) * LANES)
        k2 = jnp.concatenate([rope(kp_ref[:, cols], cos_p, sin_p),
                              rope(kc_ref[:, cols], cos_c, sin_c)], axis=0)
        v2 = jnp.concatenate([vp_ref[:, cols], vc_ref[:, cols]], axis=0)
        k2r = pltpu.roll(k2, HEAD_DIM, 1)
        v2r = pltpu.roll(v2, HEAD_DIM, 1)
        for sub in range(2):
            hkv = 2 * kvp + sub
            src_k, alt_k = (k2, k2r) if sub == 0 else (k2r, k2)
            src_v, alt_v = (v2, v2r) if sub == 0 else (v2r, v2)
            ka = jnp.where(lo64_2, src_k, 0.0).astype(BF16)
            kb = jnp.where(lo64_2, 0.0, alt_k).astype(BF16)
            va = jnp.where(lo64_2, src_v, 0.0).astype(BF16)
            vb = jnp.where(lo64_2, 0.0, alt_v).astype(BF16)
            pairs = [hkv * (SWA_GROUP // 2) + gp for gp in range(SWA_GROUP // 2)]
            q = jnp.concatenate([(rope(q_ref[:, pr * LANES:(pr + 1) * LANES], cos_c, sin_c) * scale).astype(BF16)
                                 for pr in pairs], axis=0)
            scores = [lax.dot_general(q, kk, nt_dims, preferred_element_type=F32) for kk in (ka, kb)]
            o = jnp.zeros((len(pairs) * w, LANES), F32)
            for which, (s, vv) in enumerate(zip(scores, (va, vb))):
                sink = jnp.concatenate([jnp.full((w, 1), sinks_ref[2 * pr + which], F32) for pr in pairs], axis=0)
                s = jnp.where(valid4, s, NEG)
                m = jnp.maximum(jnp.max(s, axis=-1, keepdims=True), sink)
                p = jnp.exp(s - m)
                den = jnp.sum(p, axis=-1, keepdims=True) + jnp.exp(sink - m)
                pv = jnp.dot(p.astype(BF16), vv, preferred_element_type=F32)
                o = o + pv * (1.0 / den)
            for r, pr in enumerate(pairs):
                o_ref[:, pr * LANES:(pr + 1) * LANES] = o[r * w:(r + 1) * w].astype(o_ref.dtype)


def _swa_attention(proj, positions, sinks):
    s = proj.shape[0]
    w = WINDOW
    nb = s // w
    cos, sin = _rope_tables(positions)
    prev = lambda n: jnp.maximum(n - 1, 0)
    ka_blk = OFF_KA // SWA_KV_W
    va_blk = OFF_VA // SWA_KV_W
    cur_tab = pl.BlockSpec((w, LANES), lambda n: (n, 0))
    prev_tab = pl.BlockSpec((w, LANES), lambda n: (prev(n), 0))
    return pl.pallas_call(
        _swa_kernel,
        out_shape=jax.ShapeDtypeStruct((s, SWA_Q_W), BF16),
        grid=(nb,),
        in_specs=[pl.BlockSpec(memory_space=pltpu.SMEM),
                  cur_tab, cur_tab, prev_tab, prev_tab,
                  pl.BlockSpec((w, SWA_Q_W), lambda n: (n, 0)),
                  pl.BlockSpec((w, SWA_KV_W), lambda n: (prev(n), ka_blk)),
                  pl.BlockSpec((w, SWA_KV_W), lambda n: (n, ka_blk)),
                  pl.BlockSpec((w, SWA_KV_W), lambda n: (prev(n), va_blk)),
                  pl.BlockSpec((w, SWA_KV_W), lambda n: (n, va_blk))],
        out_specs=pl.BlockSpec((w, SWA_Q_W), lambda n: (n, 0)),
        compiler_params=_params(("arbitrary",)),
        name="swa_sink_attention",
    )(sinks, cos, sin, cos, sin, proj, proj, proj, proj, proj)


CUM_ROWS = 256


def _cum_kernel(f_ref, o_ref, carry_ref):
    @pl.when(pl.program_id(0) == 0)
    def _():
        carry_ref[...] = jnp.zeros_like(carry_ref)

    f = f_ref[:, :FOX_HEADS]
    ls = jnp.minimum(f, 0.0) - jnp.log1p(jnp.exp(-jnp.abs(f)))
    r = lax.broadcasted_iota(jnp.int32, (CUM_ROWS, CUM_ROWS), 0)
    c = lax.broadcasted_iota(jnp.int32, (CUM_ROWS, CUM_ROWS), 1)
    tri = (c <= r).astype(F32)
    cum = jnp.dot(tri, ls, preferred_element_type=F32, precision=lax.Precision.HIGHEST) + carry_ref[...]
    o_ref[...] = cum
    carry_ref[...] = cum[CUM_ROWS - 1:CUM_ROWS, :]


def _fox_cum(proj):
    s = proj.shape[0]
    return pl.pallas_call(
        _cum_kernel,
        out_shape=jax.ShapeDtypeStruct((s, FOX_HEADS), F32),
        grid=(s // CUM_ROWS,),
        in_specs=[pl.BlockSpec((CUM_ROWS, LANES), lambda i: (i, OFF_FL // LANES))],
        out_specs=pl.BlockSpec((CUM_ROWS, FOX_HEADS), lambda i: (i, 0)),
        scratch_shapes=[pltpu.VMEM((1, FOX_HEADS), F32)],
        compiler_params=_params(("arbitrary",)),
        name="fox_decay_cumsum",
    )(proj)


FOX_BLK = 512
AUG_CQ = HEAD_DIM
AUG_ONE = HEAD_DIM + 3
AUG_END = HEAD_DIM + 6
PREP_HEADS = 8
SKIP_MARGIN = 170.0
NORM_SLACK = 1.001
BF16_ROUND_UP = 1.0 + 2.0 ** -7
assert OFF_QF % (PREP_HEADS * HEAD_DIM) == 0 and OFF_KF % (PREP_HEADS * HEAD_DIM) == 0
assert OFF_VF % (PREP_HEADS * HEAD_DIM) == 0


def _split3(c):
    hi = c.astype(BF16).astype(F32)
    r = c - hi
    mid = r.astype(BF16).astype(F32)
    lo = (r - mid).astype(BF16).astype(F32)
    return hi, mid, lo


def _fox_prep_kernel(q_ref, k_ref, v_ref, cum_ref, qo_ref, ko_ref, vo_ref, qn_ref, kn_ref):
    j = pl.program_id(0)
    tb = q_ref.shape[0]
    lane = lax.broadcasted_iota(jnp.int32, (tb, LANES), 1)
    lo64 = lane < HEAD_DIM
    lane_h = lax.broadcasted_iota(jnp.int32, (tb, FOX_HEADS), 1)
    cum = cum_ref[...]
    scale = HEAD_DIM ** -0.5 * LOG2E
    for local in range(PREP_HEADS):
        pair, hh = divmod(local, 2)
        h = PREP_HEADS * j + local
        col = jnp.sum(jnp.where(lane_h == h, cum, 0.0), axis=-1, keepdims=True)
        hi, mid, lo = _split3((col - col[0:1, :]) * LOG2E)
        cols = slice(pair * LANES, (pair + 1) * LANES)
        q = q_ref[:, cols] * scale
        k = k_ref[:, cols]
        v = v_ref[:, cols]
        if hh == 1:
            q = pltpu.roll(q, HEAD_DIM, 1)
            k = pltpu.roll(k, HEAD_DIM, 1)
            v = pltpu.roll(v, HEAD_DIM, 1)
        q_aug = jnp.where(lane == AUG_CQ, hi, jnp.where(lane == AUG_CQ + 1, mid,
                jnp.where(lane == AUG_CQ + 2, lo, jnp.where(lane < AUG_END, 1.0, 0.0))))
        k_aug = jnp.where(lane < AUG_ONE, 1.0, jnp.where(lane == AUG_ONE, -hi,
                jnp.where(lane == AUG_ONE + 1, -mid, jnp.where(lane == AUG_ONE + 2, -lo, 0.0))))
        v_aug = jnp.where(lane == HEAD_DIM, 1.0, 0.0)
        qb = jnp.where(lo64, q, q_aug).astype(BF16)
        kb = jnp.where(lo64, k, k_aug).astype(BF16)
        qo_ref[local] = qb
        ko_ref[local] = kb
        vo_ref[local, 0] = jnp.where(lo64, v, v_aug).T.astype(BF16)
        ones = jnp.ones((LANES, LANES), BF16)
        for src, norm_ref in ((qb, qn_ref), (kb, kn_ref)):
            x = jnp.where(lo64, src.astype(F32), 0.0)
            sq = jnp.dot((x * x * BF16_ROUND_UP).astype(BF16), ones, preferred_element_type=F32)
            norm_ref[local, 0] = jnp.broadcast_to(jnp.sqrt(jnp.max(sq, axis=0, keepdims=True)),
                                                  norm_ref.shape[2:])


def _fox_prep(proj, cum, tb):
    s = proj.shape[0]
    width = PREP_HEADS * HEAD_DIM
    shp = jax.ShapeDtypeStruct((FOX_HEADS, s, LANES), BF16)
    ospec = pl.BlockSpec((PREP_HEADS, tb, LANES), lambda j, i: (j, i, 0))
    norms = jax.ShapeDtypeStruct((FOX_HEADS, s // tb, SUBLANES, LANES), F32)
    nspec = pl.BlockSpec((PREP_HEADS, 1, SUBLANES, LANES), lambda j, i: (j, i, 0, 0))
    return pl.pallas_call(
        _fox_prep_kernel,
        out_shape=(shp, shp, jax.ShapeDtypeStruct((FOX_HEADS, s // tb, LANES, tb), BF16), norms, norms),
        grid=(FOX_HEADS // PREP_HEADS, s // tb),
        in_specs=[pl.BlockSpec((tb, width), lambda j, i: (i, OFF_QF // width + j)),
                  pl.BlockSpec((tb, width), lambda j, i: (i, OFF_KF // width + j)),
                  pl.BlockSpec((tb, width), lambda j, i: (i, OFF_VF // width + j)),
                  pl.BlockSpec((tb, FOX_HEADS), lambda j, i: (i, 0))],
        out_specs=(ospec, ospec, pl.BlockSpec((PREP_HEADS, 1, LANES, tb), lambda j, i: (j, i, 0, 0)),
                   nspec, nspec),
        compiler_params=_params(("arbitrary", "arbitrary")),
        name="fox_prepare",
    )(proj, proj, proj, cum)


def _fox_kernel(a_ref, nb_ref, q_ref, k_ref, vt_ref, o_ref, acc_ref, m_ref, s0_ref):
    j = pl.program_id(0)
    i = pl.program_id(1)
    tb = q_ref.shape[1]
    nt_dims = (((1,), (1,)), ((), ()))
    key = lax.broadcasted_iota(jnp.int32, (tb, tb), 0)
    qry = lax.broadcasted_iota(jnp.int32, (tb, tb), 1)
    causal = key <= qry

    def scores(hh, kt, diagonal=False):
        rows = pl.ds(pl.multiple_of(kt * tb, tb), tb)
        s = lax.dot_general(k_ref[hh, rows, :], q_ref[hh], nt_dims, preferred_element_type=F32)
        return jnp.where(causal, s, NEG) if diagonal else s

    def update(hh, kt, s):
        h = 2 * j + hh
        delta = (a_ref[h, i] - a_ref[h, kt]) * LOG2E
        m_old = m_ref[hh]
        m_new = jnp.maximum(m_old, jnp.max(s, axis=0, keepdims=True) + delta)
        p = jnp.exp2(s - (m_new - delta))
        alpha = jnp.exp2(m_old - m_new)
        acc_ref[hh] = alpha * acc_ref[hh] + jnp.dot(vt_ref[hh, kt], p.astype(BF16), preferred_element_type=F32)
        m_ref[hh] = m_new

    def step(kt, diagonal=False):
        s1 = scores(1, kt, diagonal)
        update(0, kt, s0_ref[...])
        s0_ref[...] = scores(0, jnp.maximum(kt - 1, 0))
        update(1, kt, s1)

    m_ref[...] = jnp.full(m_ref.shape, NEG, F32)
    acc_ref[...] = jnp.zeros(acc_ref.shape, F32)
    s0_ref[...] = scores(0, i, diagonal=True)
    step(i, diagonal=True)
    n_back = jnp.maximum(nb_ref[2 * j, i], nb_ref[2 * j + 1, i])
    lax.fori_loop(0, n_back, lambda t, carry: (step(i - 1 - t), carry)[1], 0)
    acc0 = acc_ref[0]
    acc1 = acc_ref[1]
    o0 = (acc0 * (1.0 / acc0[HEAD_DIM:HEAD_DIM + 1, :])).T
    o1 = (acc1 * (1.0 / acc1[HEAD_DIM:HEAD_DIM + 1, :])).T
    lo64 = lax.broadcasted_iota(jnp.int32, (tb, LANES), 1) < HEAD_DIM
    o_ref[...] = jnp.where(lo64, o0, pltpu.roll(o1, HEAD_DIM, 1)).astype(o_ref.dtype)


def _tiles_back(block_first, block_last, q_norm, k_norm):
    n = block_first.shape[1]
    qk_bound = q_norm * jnp.max(k_norm, axis=1, keepdims=True) * NORM_SLACK
    top = qk_bound + block_first * LOG2E
    kept = (top[:, :, None] - block_last[:, None, :] * LOG2E) >= (-qk_bound[:, :, None] - SKIP_MARGIN)
    tile = jnp.arange(n, dtype=jnp.int32)
    earlier = tile[None, None, :] < tile[None, :, None]
    first_kept = jnp.min(jnp.where(kept & earlier, tile[None, None, :], tile[None, :, None]), axis=2)
    return (tile[None, :] - first_kept).astype(jnp.int32)


def _fox_attention(proj):
    s = proj.shape[0]
    tb = min(FOX_BLK, s)
    cum = _fox_cum(proj)
    q_aug, k_aug, vt_aug, q_norm, k_norm = _fox_prep(proj, cum, tb)
    block_first = cum[::tb].T
    block_last = cum[tb - 1::tb].T
    n_back = _tiles_back(block_first, block_last, q_norm[:, :, 0, 0], k_norm[:, :, 0, 0])
    npairs = FOX_HEADS // 2
    smem = pl.BlockSpec(memory_space=pltpu.SMEM)
    return pl.pallas_call(
        _fox_kernel,
        out_shape=jax.ShapeDtypeStruct((s, FOX_W), BF16),
        grid=(npairs, s // tb),
        in_specs=[smem, smem,
                  pl.BlockSpec((2, tb, LANES), lambda j, i: (j, i, 0)),
                  pl.BlockSpec((2, s, LANES), lambda j, i: (j, 0, 0)),
                  pl.BlockSpec((2, s // tb, LANES, tb), lambda j, i: (j, 0, 0, 0))],
        out_specs=pl.BlockSpec((tb, LANES), lambda j, i: (i, j)),
        scratch_shapes=[pltpu.VMEM((2, LANES, tb), F32), pltpu.VMEM((2, 1, tb), F32), pltpu.VMEM((tb, tb), F32)],
        compiler_params=_params(("arbitrary", "arbitrary")),
        name="fox_attention",
    )(block_first, n_back, q_aug, k_aug, vt_aug)


def _lane_pack(cols, width, dtype):
    rows = cols[0].shape[0]
    lane = lax.broadcasted_iota(jnp.int32, (rows, width), 1)
    out = jnp.zeros((rows, width), dtype)
    for k, cval in enumerate(cols):
        out = jnp.where(lane == k, cval.astype(dtype), out)
    return out


def _postmix_kernel(x_ref, y_ref, gpost_ref, gate_ref, gpre_ref, scale_ref, shift_ref, wr_ref, br_ref,
                    x1_ref, h2_ref, idx_ref, prob_ref, rank_ref, cnt_ref, carry_ref):
    step = pl.program_id(0)

    @pl.when(step == 0)
    def _():
        carry_ref[...] = jnp.zeros_like(carry_ref)

    x1 = x_ref[...] + gate_ref[...] * _rms(y_ref[...], gpost_ref[...])
    x1_ref[...] = x1
    h2 = _rms(x1, gpre_ref[...]) * (1.0 + scale_ref[...]) + shift_ref[...]
    h2_ref[...] = h2
    def two_pieces(v):
        hi = v.astype(BF16)
        return hi, (v - hi.astype(F32)).astype(BF16)

    h_hi, h_lo = two_pieces(h2)
    w_hi, w_lo = two_pieces(wr_ref[...])
    logits = (jnp.dot(h_hi, w_hi, preferred_element_type=F32) + jnp.dot(h_hi, w_lo, preferred_element_type=F32)
              + jnp.dot(h_lo, w_hi, preferred_element_type=F32) + br_ref[...])
    tm = logits.shape[0]
    lane_e = lax.broadcasted_iota(jnp.int32, (tm, N_EXPERTS), 1).astype(F32)
    vals, idxs, sels = [], [], []
    cur = logits
    for _ in range(TOP_K):
        mk = jnp.max(cur, axis=-1, keepdims=True)
        ik = jnp.min(jnp.where(cur == mk, lane_e, float(N_EXPERTS)), axis=-1, keepdims=True)
        sel = lane_e == ik
        vals.append(mk)
        idxs.append(ik)
        sels.append(sel)
        cur = jnp.where(sel, -jnp.inf, cur)
    exps = [jnp.exp(v - vals[0]) for v in vals]
    inv = 1.0 / functools.reduce(lambda a, b: a + b, exps)
    probs = [e * inv for e in exps]

    onehot = functools.reduce(lambda a, b: a | b, sels).astype(F32)
    r = lax.broadcasted_iota(jnp.int32, (tm, tm), 0)
    c = lax.broadcasted_iota(jnp.int32, (tm, tm), 1)
    strict = (c < r).astype(BF16)
    before = jnp.dot(strict, onehot.astype(BF16), preferred_element_type=F32) + carry_ref[...]
    ranks = [jnp.sum(jnp.where(sel, before, 0.0), axis=-1, keepdims=True) for sel in sels]
    carry_ref[...] = carry_ref[...] + jnp.sum(onehot, axis=0, keepdims=True)

    idx_ref[...] = _lane_pack(idxs, LANES, jnp.int32)
    prob_ref[...] = _lane_pack(probs, LANES, F32)
    rank_ref[...] = _lane_pack([rk.astype(jnp.int32) for rk in ranks], LANES, jnp.int32)
    cnt_ref[...] = carry_ref[...].astype(jnp.int32)


def _postmix_router(x, y, mod, g_post, g_pre, w_router, b_router):
    s, d = x.shape
    tm = min(256, s)
    row = pl.BlockSpec((tm, d), lambda i: (i, 0))
    vec = lambda idx: pl.BlockSpec((1, d), lambda i: (0, idx))
    lanes = pl.BlockSpec((tm, LANES), lambda i: (i, 0))
    return pl.pallas_call(
        _postmix_kernel,
        out_shape=(jax.ShapeDtypeStruct((s, d), F32), jax.ShapeDtypeStruct((s, d), F32),
                   jax.ShapeDtypeStruct((s, LANES), jnp.int32), jax.ShapeDtypeStruct((s, LANES), F32),
                   jax.ShapeDtypeStruct((s, LANES), jnp.int32),
                   jax.ShapeDtypeStruct((1, N_EXPERTS), jnp.int32)),
        grid=(s // tm,),
        in_specs=[row, row, vec(0), vec(2), vec(0), vec(4), vec(3),
                  pl.BlockSpec((d, N_EXPERTS), lambda i: (0, 0)),
                  pl.BlockSpec((1, N_EXPERTS), lambda i: (0, 0))],
        out_specs=(row, row, lanes, lanes, lanes, pl.BlockSpec((1, N_EXPERTS), lambda i: (0, 0))),
        scratch_shapes=[pltpu.VMEM((1, N_EXPERTS), F32)],
        compiler_params=_params(("arbitrary",)),
        name="postmix_router",
    )(x, y, g_post.reshape(1, d), mod, g_pre.reshape(1, d), mod, mod, w_router, b_router.reshape(1, N_EXPERTS))


MOE_CAP = 1280
MOE_UNIT = 128
assert MOE_CAP % MOE_UNIT == 0
GU_TN = 512
DEINT = 256
DOWN_TN = 1024


def _item_maps(n_j):
    def clamp(i, j, ni):
        used = i < ni[0]
        return jnp.where(used, i, ni[0] - 1), jnp.where(used, j, n_j - 1)

    def item_rows(i, j, ie, ir, ni):
        return clamp(i, j, ni)[0], 0, 0

    def act_rows(i, j, ie, ir, ni):
        return clamp(i, j, ni)[0], 0

    def weights(i, j, ie, ir, ni):
        ii, jj = clamp(i, j, ni)
        return ie[ii], 0, jj

    return item_rows, act_rows, weights


def _dot_mixed(a_bf16, w_f32):
    return lax.dot_general(a_bf16, w_f32, (((1,), (0,)), ((), ())), preferred_element_type=F32)


def _for_rows(n, body, unroll=SUBLANES):
    def trip(t, carry):
        for u in range(unroll):
            body(t * unroll + u, u)
        return carry

    lax.fori_loop(0, n // unroll, trip, 0)
    lax.fori_loop((n // unroll) * unroll, n, lambda r, carry: (body(r, 0), carry)[1], 0)


def _wait_rows(src, dst, sem, n):
    n8 = pl.multiple_of((n // SUBLANES) * SUBLANES, SUBLANES)

    @pl.when(n8 > 0)
    def _():
        pltpu.make_async_copy(src.at[pl.ds(0, n8)], dst.at[pl.ds(0, n8)], sem).wait()

    one = pltpu.make_async_copy(src.at[pl.ds(0, 1)], dst.at[pl.ds(0, 1)], sem)
    lax.fori_loop(n8, n, lambda r, carry: (one.wait(), carry)[1], 0)


def _units(rows):
    return (rows + MOE_UNIT - 1) // MOE_UNIT


def _unit_rows(start_unit, n_units):
    return pl.ds(pl.multiple_of(start_unit * MOE_UNIT, MOE_UNIT), n_units * MOE_UNIT)


def _for_blocks(rows, compute, finish):
    units = _units(rows)
    triple = jnp.logical_and(units % 2 == 1, units >= 3)
    single = units == 1
    n2 = (units - jnp.where(triple, 3, 0) - jnp.where(single, 1, 0)) // 2

    def run(slices):
        zs = [compute(rs) for rs in slices]
        for rs, z in zip(slices, zs):
            finish(rs, z)

    def pair(pb, carry):
        run([_unit_rows(4 * pb, 2), _unit_rows(4 * pb + 2, 2)])
        return carry

    lax.fori_loop(0, n2 // 2, pair, 0)

    @pl.when(n2 % 2 == 1)
    def _():
        run([_unit_rows(2 * (n2 - 1), 2)])

    @pl.when(triple)
    def _():
        run([_unit_rows(2 * n2, 3)])

    @pl.when(single)
    def _():
        run([_unit_rows(0, 1)])


def _gateup_kernel(ie_ref, ir_ref, ni_ref, dst_ref, h_ref, w_ref, b_ref, o_ref, stg_ref, xb_ref, sem):
    i = pl.program_id(0)
    j = pl.program_id(1)
    cap = o_ref.shape[0]

    @pl.when(i >= ni_ref[0])
    def _():
        o_ref[...] = jnp.zeros_like(o_ref)

    @pl.when(i < ni_ref[0])
    def _():
        rows = ir_ref[i]

        def row_copy(r):
            token = lax.rem(dst_ref[0, 0, r].astype(jnp.uint32), jnp.uint32(h_ref.shape[0])).astype(jnp.int32)
            return pltpu.make_async_copy(h_ref.at[pl.ds(token, 1)], stg_ref.at[pl.ds(r, 1)], sem)

        @pl.when(j == 0)
        def _():
            @pl.when(i == 0)
            def _():
                stg_ref[...] = jnp.zeros_like(stg_ref)

            _for_rows(rows, lambda r, u: row_copy(r).start(priority=u % 2))
            _wait_rows(h_ref, stg_ref, sem, rows)

            def to_bf16(u, carry):
                xb_ref[_unit_rows(u, 1), :] = stg_ref[_unit_rows(u, 1), :].astype(BF16)
                return carry

            lax.fori_loop(0, _units(rows), to_bf16, 0)

        r = lax.broadcasted_iota(jnp.int32, (DEINT, DEINT), 0)
        c = lax.broadcasted_iota(jnp.int32, (DEINT, DEINT), 1)
        half = DEINT // 2
        perm = (r == jnp.where(c < half, 2 * c, 2 * (c - half) + 1)).astype(BF16)

        def gate_up(rs):
            return _dot_mixed(xb_ref[rs, :], w_ref[...]) + b_ref[...]

        def activate(rs, z):
            for t in range(GU_TN // DEINT):
                zz = jnp.dot(z[:, t * DEINT:(t + 1) * DEINT].astype(BF16), perm, preferred_element_type=F32)
                glu = jnp.minimum(zz[:, :half], SWIGLU_LIMIT)
                lin = jnp.clip(zz[:, half:], -SWIGLU_LIMIT, SWIGLU_LIMIT)
                act = glu * jax.nn.sigmoid(SWIGLU_ALPHA * glu) * (lin + 1.0)
                o_ref[rs, t * half:(t + 1) * half] = act.astype(o_ref.dtype)

        _for_blocks(rows, gate_up, activate)

        def zero_unit(u, carry):
            o_ref[_unit_rows(u, 1), :] = jnp.zeros((MOE_UNIT, o_ref.shape[1]), o_ref.dtype)
            return carry

        lax.fori_loop(_units(rows), cap // MOE_UNIT, zero_unit, 0)


def _moe_gateup(h2, dst, item_expert, item_rows, num_items, w_gate_up, b_gate_up, cap):
    s, d = h2.shape
    n_items = dst.shape[0]
    n_j = (2 * D_FF) // GU_TN
    item_map, _, weights = _item_maps(n_j)
    grid_spec = pltpu.PrefetchScalarGridSpec(
        num_scalar_prefetch=3,
        grid=(n_items, n_j),
        in_specs=[pl.BlockSpec((1, 1, cap), item_map, memory_space=pltpu.SMEM),
                  pl.BlockSpec(memory_space=pl.ANY),
                  pl.BlockSpec((None, d, GU_TN), weights),
                  pl.BlockSpec((None, 1, GU_TN), weights)],
        out_specs=pl.BlockSpec((cap, GU_TN // 2), lambda i, j, ie, ir, ni: (i, j)),
        scratch_shapes=[pltpu.VMEM((cap, d), F32), pltpu.VMEM((cap, d), BF16), pltpu.SemaphoreType.DMA(())],
    )
    return pl.pallas_call(
        _gateup_kernel,
        out_shape=jax.ShapeDtypeStruct((n_items * cap, D_FF), BF16),
        grid_spec=grid_spec,
        compiler_params=_params(("arbitrary", "arbitrary")),
        name="moe_gate_up",
    )(item_expert, item_rows, num_items, dst, h2, w_gate_up, b_gate_up.reshape(N_EXPERTS, 1, 2 * D_FF))


def _down_kernel(ie_ref, ir_ref, ni_ref, dst_ref, a_ref, w_ref, b_ref, o_ref, y_ref, sem):
    i = pl.program_id(0)
    j = pl.program_id(1)
    tn = w_ref.shape[1]
    n_j = y_ref.shape[1] // tn

    @pl.when(i < ni_ref[0])
    def _():
        rows = ir_ref[i]

        def down(rs):
            return _dot_mixed(a_ref[rs, :], w_ref[...]) + b_ref[...]

        for jj in range(n_j):
            @pl.when(j == jj)
            def _(jj=jj):
                def keep(rs, y):
                    y_ref[rs, jj * tn:(jj + 1) * tn] = y

                _for_blocks(rows, down, keep)

        @pl.when(j == n_j - 1)
        def _():
            def row_copy(r, dst_row):
                return pltpu.make_async_copy(y_ref.at[pl.ds(r, 1)], o_ref.at[pl.ds(dst_row, 1)], sem)

            _for_rows(rows, lambda r, u: row_copy(r, dst_ref[0, 0, r]).start(priority=u % 2))
            _wait_rows(y_ref, o_ref, sem, rows)


def _moe_down(act, dst, item_expert, item_rows, num_items, w_down, b_down, n_out_rows, cap):
    d = w_down.shape[2]
    tn = min(DOWN_TN, d)
    n_items = dst.shape[0]
    n_j = d // tn
    item_map, act_rows, weights = _item_maps(n_j)
    grid_spec = pltpu.PrefetchScalarGridSpec(
        num_scalar_prefetch=3,
        grid=(n_items, n_j),
        in_specs=[pl.BlockSpec((1, 1, cap), item_map, memory_space=pltpu.SMEM),
                  pl.BlockSpec((cap, D_FF), act_rows),
                  pl.BlockSpec((None, D_FF, tn), weights),
                  pl.BlockSpec((None, 1, tn), weights)],
        out_specs=pl.BlockSpec(memory_space=pl.ANY),
        scratch_shapes=[pltpu.VMEM((cap, d), F32), pltpu.SemaphoreType.DMA(())],
    )
    return pl.pallas_call(
        _down_kernel,
        out_shape=jax.ShapeDtypeStruct((n_out_rows, d), F32),
        grid_spec=grid_spec,
        compiler_params=_params(("arbitrary", "arbitrary")),
        name="moe_down",
    )(item_expert, item_rows, num_items, dst, act, w_down, b_down.reshape(N_EXPERTS, 1, d))


COMBINE_TOKENS = 128


def _combine_kernel(prob_ref, x1_ref, g_ref, gate_ref, ys_ref, o_ref):
    prob = prob_ref[...]
    y = prob[:, 0:1] * ys_ref[0]
    for k in range(1, TOP_K):
        y = y + prob[:, k:k + 1] * ys_ref[k]
    o_ref[...] = x1_ref[...] + gate_ref[...] * _rms(y, g_ref[...])


def _moe_combine(ys, probs, x1, g_post, mod, gate_idx):
    s, d = x1.shape
    tc = min(COMBINE_TOKENS, s)
    return pl.pallas_call(
        _combine_kernel,
        out_shape=jax.ShapeDtypeStruct((s, d), F32),
        grid=(s // tc,),
        in_specs=[pl.BlockSpec((tc, LANES), lambda i: (i, 0)),
                  pl.BlockSpec((tc, d), lambda i: (i, 0)),
                  pl.BlockSpec((1, d), lambda i: (0, 0)),
                  pl.BlockSpec((1, d), lambda i: (0, gate_idx)),
                  pl.BlockSpec((TOP_K, tc, d), lambda i: (0, i, 0))],
        out_specs=pl.BlockSpec((tc, d), lambda i: (i, 0)),
        compiler_params=_params(("arbitrary",)),
        name="moe_combine",
    )(probs, x1, g_post.reshape(1, d), mod, ys.reshape(TOP_K, s, d))


def _moe(h2, idx, probs, rank, counts, x1, mod, g_post, w_gate_up, b_gate_up, w_down, b_down):
    s, d = x1.shape
    cap = MOE_CAP
    i32 = jnp.int32
    counts = counts.reshape(N_EXPERTS)
    items_per_expert = (counts + cap - 1) // cap
    item_end = jnp.cumsum(items_per_expert)
    first_item = (item_end - items_per_expert).astype(i32)
    n_items_max = (s * TOP_K + N_EXPERTS * (cap - 1)) // cap
    num_items = item_end[-1].astype(i32).reshape(1)
    item_ids = jnp.arange(n_items_max, dtype=i32)
    item_expert = jnp.minimum(jnp.sum(item_end[None, :] <= item_ids[:, None], axis=1), N_EXPERTS - 1).astype(i32)
    item_rows = jnp.clip(counts[item_expert] - cap * (item_ids - first_item[item_expert]), 0, cap)
    item_rows = jnp.where(item_ids < num_items[0], item_rows, 0).astype(i32)

    pos = (first_item[idx[:, :TOP_K]] * cap + rank[:, :TOP_K]).reshape(-1)
    tok = jnp.repeat(jnp.arange(s, dtype=i32), TOP_K)
    slot = jnp.tile(jnp.arange(TOP_K, dtype=i32), s)
    dst = jnp.zeros((n_items_max * cap,), i32).at[pos].set(slot * s + tok, unique_indices=True)
    dst = dst.reshape(n_items_max, 1, cap)

    act = _moe_gateup(h2, dst, item_expert, item_rows, num_items, w_gate_up, b_gate_up, cap)
    ys = _moe_down(act, dst, item_expert, item_rows, num_items, w_down, b_down, TOP_K * s, cap)
    return _moe_combine(ys, probs, x1, g_post, mod, 5)


def _layer(x, c, positions, w_mod, b_mod, g_pre_mix, g_post_mix, g_pre_ffn, g_post_ffn,
           w_in, b_in, sinks, w_out, b_out, w_router, b_router, w_gate_up, b_gate_up, w_down, b_down):
    mod = _modulation(c, w_mod, b_mod)
    h = _prenorm(x, g_pre_mix, mod, 1, 0)
    proj = _inproj(h, w_in, b_in)
    o_a = _swa_attention(proj, positions, sinks)
    o_f = _fox_attention(proj)
    y = _outproj(o_a, o_f, w_out, b_out)
    x1, h2, idx, probs, rank, counts = _postmix_router(x, y, mod, g_post_mix, g_pre_ffn, w_router, b_router)
    return _moe(h2, idx, probs, rank, counts, x1, mod, g_post_ffn, w_gate_up, b_gate_up, w_down, b_down)


def kernel(x, c, positions, w_mod, b_mod, g_pre_mix, g_post_mix, g_pre_ffn, g_post_ffn, w_in, b_in, sinks,
           w_out, b_out, w_router, b_router, w_gate_up, b_gate_up, w_down, b_down):
    batch, seq, d = x.shape
    assert batch == 1 and w_mod.shape[0] == 1, "one sequence, one layer"
    assert seq % FOX_BLK == 0 or seq < FOX_BLK
    out = _layer(x[0], c[0], positions[0], w_mod[0], b_mod[0], g_pre_mix[0], g_post_mix[0], g_pre_ffn[0],
                 g_post_ffn[0], w_in[0], b_in[0], sinks[0], w_out[0], b_out[0], w_router[0], b_router[0],
                 w_gate_up[0], b_gate_up[0], w_down[0], b_down[0])
    return out[None]
```
